```python
import jax, jax.numpy as jnp
from jax import lax
import numpy as np

D_MODEL = 1024
BATCH = 16
SEQ = 2048
DEPTH = 1

FOX_HEADS = 8
FOX_HEAD_DIM = 128
FOX_WIDTH = FOX_HEADS * FOX_HEAD_DIM
Q_BLOCK = 128
ML_HEADS = 4
ML_QK_DIM = 128
ML_V_DIM = 256
ML_QK_WIDTH = ML_HEADS * ML_QK_DIM
ML_WIDTH = ML_HEADS * ML_V_DIM
ML_CHUNK = 64
CONV_WIDTH = 4
EPS = 1e-6

SPLIT_SIZES = (FOX_WIDTH, FOX_WIDTH, FOX_WIDTH, FOX_HEADS, FOX_WIDTH,
               ML_QK_WIDTH, ML_QK_WIDTH, ML_WIDTH, ML_HEADS, ML_HEADS, ML_WIDTH, ML_WIDTH,
               D_MODEL, D_MODEL)
IN_WIDTH = sum(SPLIT_SIZES)

kernel_name = "fox_mlstm_gated_hybrid"


def rmsnorm(x, g):
    xf = x.astype(jnp.float32)
    y = xf * lax.rsqrt(jnp.mean(xf * xf, axis=-1, keepdims=True) + EPS)
    return (y * g.astype(jnp.float32)).astype(x.dtype)


def split_columns(z):
    parts, off = [], 0
    for size in SPLIT_SIZES:
        parts.append(z[..., off:off + size])
        off += size
    return parts


def fox_attention(q, k, v, logf):
    S = q.shape[1]
    c = jnp.transpose(jnp.cumsum(logf, axis=1), (0, 2, 1))
    q = q * (FOX_HEAD_DIM ** -0.5)
    outs = []
    for blk in range(S // Q_BLOCK):
        lo, hi = blk * Q_BLOCK, (blk + 1) * Q_BLOCK
        s = jnp.einsum('bqhd,bkhd->bhqk', q[:, lo:hi], k[:, :hi]).astype(jnp.float32)
        s = s + c[:, :, lo:hi, None] - c[:, :, None, :hi]
        causal = (lo + jnp.arange(Q_BLOCK))[:, None] >= jnp.arange(hi)[None, :]
        p = jax.nn.softmax(jnp.where(causal, s, -jnp.inf), axis=-1)
        outs.append(jnp.einsum('bhqk,bkhd->bqhd', p.astype(v.dtype), v[:, :hi]))
    return jnp.concatenate(outs, axis=1)


def causal_depthwise_conv(x, w, b):
    C = x.shape[-1]
    y = lax.conv_general_dilated(x, w[:, None, :].astype(x.dtype), window_strides=(1,),
                                 padding=[(CONV_WIDTH - 1, 0)],
                                 dimension_numbers=('NWC', 'WIO', 'NWC'),
                                 feature_group_count=C)
    return y + b.astype(x.dtype)


def mlstm_chunkwise(q, k, v, ig, lf):
    B, S, H, Dk = q.shape
    Dv = v.shape[-1]
    L = ML_CHUNK
    NC = S // L

    def chunks(a):
        return jnp.transpose(a.reshape(B, NC, L, H, a.shape[-1]), (1, 0, 3, 2, 4))

    qc, kc, vc = chunks(q * (Dk ** -0.5)), chunks(k), chunks(v)
    igc, lfc = chunks(ig[..., None])[..., 0], chunks(lf[..., None])[..., 0]
    causal = jnp.tril(jnp.ones((L, L), dtype=bool))

    def step(carry, inp):
        C, n, m = carry
        qb, kb, vb, ib, fb = inp
        b = jnp.cumsum(fb, axis=-1)
        D = jnp.where(causal, b[..., :, None] - b[..., None, :] + ib[..., None, :], -jnp.inf)
        inter = b + m[..., None]
        m_t = jnp.maximum(inter, jnp.max(D, axis=-1))
        w_intra = jnp.exp(D - m_t[..., None])
        w_inter = jnp.exp(inter - m_t)
        s = jnp.einsum('bhtd,bhsd->bhts', qb, kb) * w_intra
        num = jnp.einsum('bhts,bhsv->bhtv', s, vb) + w_inter[..., None] * jnp.einsum('bhtd,bhdv->bhtv', qb, C)
        den = jnp.sum(s, axis=-1) + w_inter * jnp.einsum('bhtd,bhd->bht', qb, n)
        h = num / jnp.maximum(jnp.abs(den), jnp.exp(-m_t))[..., None]
        bL = b[..., -1]
        logw = bL[..., None] - b + ib
        m_new = jnp.maximum(bL + m, jnp.max(logw, axis=-1))
        w = jnp.exp(logw - m_new[..., None])
        decay = jnp.exp(bL + m - m_new)
        C_new = decay[..., None, None] * C + jnp.einsum('bhs,bhsd,bhsv->bhdv', w, kb, vb)
        n_new = decay[..., None] * n + jnp.einsum('bhs,bhsd->bhd', w, kb)
        return (C_new, n_new, m_new), h

    init = (jnp.zeros((B, H, Dk, Dv), jnp.float32), jnp.zeros((B, H, Dk), jnp.float32),
            jnp.zeros((B, H), jnp.float32))
    _, h = lax.scan(step, init, (qc, kc, vc, igc, lfc))
    return jnp.transpose(h, (1, 0, 3, 2, 4)).reshape(B, S, H, Dv)


def setup_inputs(seed: int = 0) -> dict:
    key = jax.random.key(seed)
    ks = jax.random.split(key, 16)
    nrm = lambda k, shape, scale: jax.random.normal(k, shape, jnp.float32) * scale
    return {
        "x": nrm(ks[0], (BATCH, SEQ, D_MODEL), 1.0),
        "norm_g": 1.0 + nrm(ks[1], (DEPTH, D_MODEL), 0.05),
        "w_in": nrm(ks[2], (DEPTH, D_MODEL, IN_WIDTH), D_MODEL ** -0.5),
        "b_fox_f": jax.random.uniform(ks[3], (DEPTH, FOX_HEADS), jnp.float32, 1.0, 4.0),
        "conv_w": nrm(ks[4], (DEPTH, CONV_WIDTH, 2 * ML_QK_WIDTH), CONV_WIDTH ** -0.5),
        "conv_b": nrm(ks[5], (DEPTH, 2 * ML_QK_WIDTH), 0.01),
        "b_ml_i": nrm(ks[6], (DEPTH, ML_HEADS), 0.1),
        "b_ml_f": jax.random.uniform(ks[7], (DEPTH, ML_HEADS), jnp.float32, 3.0, 6.0),
        "ml_norm_g": 1.0 + nrm(ks[8], (DEPTH, ML_WIDTH), 0.05),
        "b_gate": nrm(ks[9], (DEPTH, 2 * D_MODEL), 0.1),
        "w_fox_down": nrm(ks[10], (DEPTH, FOX_WIDTH, D_MODEL), FOX_WIDTH ** -0.5),
        "w_ml_down": nrm(ks[11], (DEPTH, ML_WIDTH, D_MODEL), ML_WIDTH ** -0.5),
        "w_out": nrm(ks[12], (DEPTH, D_MODEL, D_MODEL), D_MODEL ** -0.5),
        "final_g": 1.0 + nrm(ks[13], (D_MODEL,), 0.05),
    }


def reference(x, norm_g, w_in, b_fox_f, conv_w, conv_b, b_ml_i, b_ml_f, ml_norm_g, b_gate,
              w_fox_down, w_ml_down, w_out, final_g):
    B, S, _ = x.shape
    f32 = jnp.float32
    for l in range(DEPTH):
        xn = rmsnorm(x, norm_g[l])
        z = jnp.einsum('bsd,de->bse', xn, w_in[l])
        fq, fk, fv, ff, fz, mq, mk, mv, mi, mf, mo, mz, ga, gb = split_columns(z)

        logf_a = jax.nn.log_sigmoid(ff.astype(f32) + b_fox_f[l].astype(f32))
        o_a = fox_attention(fq.reshape(B, S, FOX_HEADS, FOX_HEAD_DIM),
                            fk.reshape(B, S, FOX_HEADS, FOX_HEAD_DIM),
                            fv.reshape(B, S, FOX_HEADS, FOX_HEAD_DIM), logf_a)
        o_a = o_a.reshape(B, S, FOX_WIDTH) * jax.nn.silu(fz)
        y_a = jnp.einsum('bse,ed->bsd', o_a, w_fox_down[l])

        qk = jax.nn.silu(causal_depthwise_conv(jnp.concatenate([mq, mk], axis=-1), conv_w[l], conv_b[l]))
        q_b = qk[..., :ML_QK_WIDTH].reshape(B, S, ML_HEADS, ML_QK_DIM).astype(f32)
        k_b = qk[..., ML_QK_WIDTH:].reshape(B, S, ML_HEADS, ML_QK_DIM).astype(f32)
        v_b = mv.reshape(B, S, ML_HEADS, ML_V_DIM).astype(f32)
        ig = mi.astype(f32) + b_ml_i[l].astype(f32)
        lf = jax.nn.log_sigmoid(mf.astype(f32) + b_ml_f[l].astype(f32))
        h_b = mlstm_chunkwise(q_b, k_b, v_b, ig, lf)
        h_b = rmsnorm(h_b, ml_norm_g[l].reshape(ML_HEADS, ML_V_DIM)).reshape(B, S, ML_WIDTH).astype(x.dtype)
        h_b = h_b * jax.nn.sigmoid(mo) * jax.nn.silu(mz)
        y_b = jnp.einsum('bse,ed->bsd', h_b, w_ml_down[l])

        gates = jax.nn.sigmoid(jnp.concatenate([ga, gb], axis=-1) + b_gate[l])
        y = gates[..., :D_MODEL] * y_a + gates[..., D_MODEL:] * y_b
        x = x + jnp.einsum('bsd,de->bse', y, w_out[l])
    return rmsnorm(x, final_g)
```

```python
import functools

import jax
import jax.numpy as jnp
from jax import lax
from jax.experimental import pallas as pl
from jax.experimental.pallas import tpu as pltpu

F32 = jnp.float32
BF16 = jnp.bfloat16

D_MODEL = 1024
FOX_HEADS = 8
FOX_HEAD_DIM = 128
FOX_WIDTH = FOX_HEADS * FOX_HEAD_DIM
ML_HEADS = 4
ML_QK_DIM = 128
ML_V_DIM = 256
ML_QK_WIDTH = ML_HEADS * ML_QK_DIM
ML_WIDTH = ML_HEADS * ML_V_DIM
CONV_WIDTH = 4
EPS = 1e-6

LANES = 128
Z_WIDTH = 10240
GATE_WIDTH = LANES
VMEM_LIMIT = 56 * 1024 * 1024

ZB_FQ, ZB_FK, ZB_FV, ZB_FZ = 0, 8, 16, 24
ZB_MQ, ZB_MK, ZB_MV, ZB_MO, ZB_MZ = 32, 36, 40, 48, 56
ZB_GA, ZB_GB = 64, 72

GL_FOX, GL_IG, GL_LF = 0, 8, 16
CL_C, CL_BETA, CL_G, CL_M = 0, 8, 16, 24
RS_C, RS_BETA, RS_G = 0, 8, 16

TM_IN, TN_IN = 1024, 2048
TQ = 256
ML_L = 256
TM_OUT = 512

NT_DIMS = (((1,), (1,)), ((), ()))
TN_DIMS = (((0,), (0,)), ((), ()))


def _dot(a, b):
    return jnp.dot(a, b, preferred_element_type=F32)


def _inproj_kernel(x_ref, g_ref, w_ref, cs_ref, wgh_ref, wgl_ref, z_ref, zg_ref, xh_ref):
    @pl.when(pl.program_id(1) == 0)
    def _():
        x = x_ref[...]
        ms = jnp.mean(x * x, axis=-1, keepdims=True)
        xn = x * lax.rsqrt(ms + EPS) * g_ref[...]
        xh = xn.astype(BF16)
        xh_ref[...] = xh
        xl = (xn - xh.astype(F32)).astype(BF16)
        zg_ref[...] = _dot(xh, wgh_ref[...]) + _dot(xh, wgl_ref[...]) + _dot(xl, wgh_ref[...])

    z_ref[...] = (_dot(xh_ref[...], w_ref[...]) * cs_ref[...]).astype(BF16)


def _inproj(x2, norm_g, w_big, col_scale, wg_hi, wg_lo):
    T = x2.shape[0]
    grid = (T // TM_IN, Z_WIDTH // TN_IN)
    return pl.pallas_call(
        _inproj_kernel,
        grid=grid,
        in_specs=[
            pl.BlockSpec((TM_IN, D_MODEL), lambda i, j: (i, 0)),
            pl.BlockSpec((1, D_MODEL), lambda i, j: (0, 0)),
            pl.BlockSpec((D_MODEL, TN_IN), lambda i, j: (0, j)),
            pl.BlockSpec((1, TN_IN), lambda i, j: (0, j)),
            pl.BlockSpec((D_MODEL, GATE_WIDTH), lambda i, j: (0, 0)),
            pl.BlockSpec((D_MODEL, GATE_WIDTH), lambda i, j: (0, 0)),
        ],
        out_specs=[
            pl.BlockSpec((TM_IN, TN_IN), lambda i, j: (i, j)),
            pl.BlockSpec((TM_IN, GATE_WIDTH), lambda i, j: (i, 0)),
        ],
        out_shape=[
            jax.ShapeDtypeStruct((T, Z_WIDTH), BF16),
            jax.ShapeDtypeStruct((T, GATE_WIDTH), F32),
        ],
        scratch_shapes=[pltpu.VMEM((TM_IN, D_MODEL), BF16)],
        compiler_params=pltpu.CompilerParams(
            dimension_semantics=("arbitrary", "arbitrary"), vmem_limit_bytes=VMEM_LIMIT),
        name="inproj",
    )(x2, norm_g, w_big, col_scale, wg_hi, wg_lo)


def _scan_lanes(x, op, ident):
    n = x.shape[1]
    lane = lax.broadcasted_iota(jnp.int32, x.shape, 1)
    k = 1
    while k < n:
        shifted = pltpu.roll(x, k, 1)
        x = op(x, jnp.where(lane >= k, shifted, ident))
        k *= 2
    return x


def _log_sigmoid(x):
    return jnp.minimum(x, 0.0) - jnp.log1p(jnp.exp(-jnp.abs(x)))


def _gate_kernel(zg_ref, bias_ref, row_ref, col_ref):
    pre = zg_ref[...] + bias_ref[...]
    t = jnp.transpose(pre)
    c = _scan_lanes(_log_sigmoid(t[GL_FOX:GL_FOX + 8]), jnp.add, 0.0)
    ig = t[GL_IG:GL_IG + 8]
    fsum = _scan_lanes(_log_sigmoid(t[GL_LF:GL_LF + 8]), jnp.add, 0.0)
    beta = ig - fsum
    gmax = jnp.maximum(_scan_lanes(beta, jnp.maximum, -jnp.inf), 0.0)
    m = fsum + gmax
    row_ref[RS_C:RS_C + 8, :] = c
    row_ref[RS_BETA:RS_BETA + 8, :] = beta
    row_ref[RS_G:RS_G + 8, :] = gmax
    S = c.shape[1]
    packed = jnp.concatenate([c, beta, gmax, m, jnp.zeros((LANES - 32, S), F32)], axis=0)
    col_ref[...] = jnp.transpose(packed)


def _gates(zg, bias_row, B, S):
    return pl.pallas_call(
        _gate_kernel,
        grid=(B,),
        in_specs=[
            pl.BlockSpec((S, GATE_WIDTH), lambda b: (b, 0)),
            pl.BlockSpec((1, GATE_WIDTH), lambda b: (0, 0)),
        ],
        out_specs=[
            pl.BlockSpec((None, 24, S), lambda b: (b, 0, 0)),
            pl.BlockSpec((S, LANES), lambda b: (b, 0)),
        ],
        out_shape=[
            jax.ShapeDtypeStruct((B, 24, S), F32),
            jax.ShapeDtypeStruct((B * S, LANES), F32),
        ],
        compiler_params=pltpu.CompilerParams(
            dimension_semantics=("arbitrary",), vmem_limit_bytes=VMEM_LIMIT),
        name="gates",
    )(zg, bias_row)


def _pick_lane(tile, lane_idx):
    lane = lax.broadcasted_iota(jnp.int32, tile.shape, 1)
    return jnp.sum(jnp.where(lane == lane_idx, tile, 0.0), axis=1, keepdims=True)


def _fox_kernel(q_ref, k_ref, v_ref, fz_ref, ccol_ref, crow_ref, o_ref):
    h = pl.program_id(1)
    qi = pl.program_id(2)
    q = q_ref[...]
    cq = _pick_lane(ccol_ref[...], CL_C + h)

    def tile(kj, carry, masked):
        m, l, acc = carry
        r0 = pl.multiple_of(kj * TQ, TQ)
        k = k_ref[pl.ds(r0, TQ), :]
        v = v_ref[pl.ds(r0, TQ), :]
        s = lax.dot_general(q, k, NT_DIMS, preferred_element_type=F32)
        s = s + (cq - crow_ref[pl.ds(kj, 1), :])
        if masked:
            ti = lax.broadcasted_iota(jnp.int32, s.shape, 0)
            si = lax.broadcasted_iota(jnp.int32, s.shape, 1)
            s = jnp.where(si <= ti, s, -jnp.inf)
        m_new = jnp.maximum(m, jnp.max(s, axis=1, keepdims=True))
        alpha = jnp.exp(m - m_new)
        p = jnp.exp(s - m_new)
        l = alpha * l + jnp.sum(p, axis=1, keepdims=True)
        acc = alpha * acc + _dot(p.astype(BF16), v)
        return m_new, l, acc

    init = (jnp.full((TQ, 1), -1e30, F32), jnp.zeros((TQ, 1), F32), jnp.zeros((TQ, FOX_HEAD_DIM), F32))
    carry = lax.fori_loop(0, qi, lambda kj, c: tile(kj, c, False), init)
    _, l, acc = tile(qi, carry, True)
    fz = fz_ref[...].astype(F32)
    o_ref[...] = (acc * (1.0 / l) * (fz * jax.nn.sigmoid(fz))).astype(BF16)


def _fox(z, colpack, rowpack4, B, S):
    nq = S // TQ
    return pl.pallas_call(
        _fox_kernel,
        grid=(B, FOX_HEADS, nq),
        in_specs=[
            pl.BlockSpec((TQ, LANES), lambda b, h, i: (b * nq + i, ZB_FQ + h)),
            pl.BlockSpec((S, LANES), lambda b, h, i: (b, ZB_FK + h)),
            pl.BlockSpec((S, LANES), lambda b, h, i: (b, ZB_FV + h)),
            pl.BlockSpec((TQ, LANES), lambda b, h, i: (b * nq + i, ZB_FZ + h)),
            pl.BlockSpec((TQ, LANES), lambda b, h, i: (b * nq + i, 0)),
            pl.BlockSpec((None, None, nq, TQ), lambda b, h, i: (b, RS_C + h, 0, 0)),
        ],
        out_specs=pl.BlockSpec((TQ, LANES), lambda b, h, i: (b * nq + i, h)),
        out_shape=jax.ShapeDtypeStruct((B * S, FOX_WIDTH), BF16),
        compiler_params=pltpu.CompilerParams(
            dimension_semantics=("arbitrary", "arbitrary", "arbitrary"), vmem_limit_bytes=VMEM_LIMIT),
        name="fox",
    )(z, z, z, z, colpack, rowpack4)


def _conv_silu(x_ref, w_ref, b_ref):
    x = x_ref[...].astype(F32)
    row = lax.broadcasted_iota(jnp.int32, x.shape, 0)
    w = w_ref[...]
    y = x * w[CONV_WIDTH - 1:CONV_WIDTH, :] + b_ref[...]
    for r in range(1, CONV_WIDTH):
        xs = jnp.where(row >= r, pltpu.roll(x, r, 0), 0.0)
        y = y + xs * w[CONV_WIDTH - 1 - r:CONV_WIDTH - r, :]
    return y * jax.nn.sigmoid(y)


def _mlstm_kernel(q_ref, k_ref, v_ref, mo_ref, mz_ref, cwq_ref, cwk_ref, cbq_ref, cbk_ref,
                  col_ref, brow_ref, ng_ref, o_ref, qs_ref, kf_ref, c_ref, n_ref):
    h = pl.program_id(1)
    S = q_ref.shape[0]
    L = ML_L
    qs_ref[...] = (_conv_silu(q_ref, cwq_ref, cbq_ref) * (ML_QK_DIM ** -0.5)).astype(BF16)
    kf_ref[...] = _conv_silu(k_ref, cwk_ref, cbk_ref)
    c_ref[...] = jnp.zeros_like(c_ref)
    n_ref[...] = jnp.zeros_like(n_ref)
    ti = lax.broadcasted_iota(jnp.int32, (L, L), 0)
    si = lax.broadcasted_iota(jnp.int32, (L, L), 1)

    def chunk(c, g_prev):
        r0 = pl.multiple_of(c * L, L)
        q = qs_ref[pl.ds(r0, L), :]
        kf = kf_ref[pl.ds(r0, L), :]
        v = v_ref[pl.ds(r0, L), :]
        colp = col_ref[pl.ds(r0, L), :]
        b_col = _pick_lane(colp, CL_BETA + h)
        g_col = _pick_lane(colp, CL_G + h)
        m_col = _pick_lane(colp, CL_M + h)
        b_row = brow_ref[pl.ds(c, 1), :]
        sqk = lax.dot_general(q, kf.astype(BF16), NT_DIMS, preferred_element_type=F32)
        sw = sqk * jnp.where(si <= ti, jnp.exp(b_row - g_col), 0.0)
        w_inter = jnp.exp(g_prev - g_col)
        num = _dot(sw.astype(BF16), v) + w_inter * _dot(q, c_ref[...].astype(BF16))
        den = (jnp.sum(sw, axis=1, keepdims=True)
               + w_inter * jnp.sum(q.astype(F32) * n_ref[...], axis=1, keepdims=True))
        hh = num * (1.0 / jnp.maximum(jnp.abs(den), jnp.exp(-m_col)))
        ms = jnp.mean(hh * hh, axis=1, keepdims=True)
        hn = hh * lax.rsqrt(ms + EPS) * ng_ref[...]
        mo = mo_ref[pl.ds(r0, L), :].astype(F32)
        mz = mz_ref[pl.ds(r0, L), :].astype(F32)
        o_ref[pl.ds(r0, L), :] = (hn * jax.nn.sigmoid(mo) * (mz * jax.nn.sigmoid(mz))).astype(BF16)
        g_last = g_col[L - 1:L, :]
        kw = kf * jnp.exp(b_col - g_last)
        decay = jnp.exp(g_prev - g_last)
        c_ref[...] = decay * c_ref[...] + lax.dot_general(
            kw.astype(BF16), v, TN_DIMS, preferred_element_type=F32)
        n_ref[...] = decay * n_ref[...] + jnp.sum(kw, axis=0, keepdims=True)
        return g_last

    lax.fori_loop(0, S // L, chunk, jnp.zeros((1, 1), F32))


def _mlstm(z, conv_w, conv_b, colpack, rowpack4, ml_norm_g, B, S):
    nc = S // ML_L
    return pl.pallas_call(
        _mlstm_kernel,
        grid=(B, ML_HEADS),
        in_specs=[
            pl.BlockSpec((S, LANES), lambda b, h: (b, ZB_MQ + h)),
            pl.BlockSpec((S, LANES), lambda b, h: (b, ZB_MK + h)),
            pl.BlockSpec((S, ML_V_DIM), lambda b, h: (b, ZB_MV // 2 + h)),
            pl.BlockSpec((S, ML_V_DIM), lambda b, h: (b, ZB_MO // 2 + h)),
            pl.BlockSpec((S, ML_V_DIM), lambda b, h: (b, ZB_MZ // 2 + h)),
            pl.BlockSpec((CONV_WIDTH, LANES), lambda b, h: (0, h)),
            pl.BlockSpec((CONV_WIDTH, LANES), lambda b, h: (0, ML_HEADS + h)),
            pl.BlockSpec((1, LANES), lambda b, h: (0, h)),
            pl.BlockSpec((1, LANES), lambda b, h: (0, ML_HEADS + h)),
            pl.BlockSpec((S, LANES), lambda b, h: (b, 0)),
            pl.BlockSpec((None, None, nc, ML_L), lambda b, h: (b, RS_BETA + h, 0, 0)),
            pl.BlockSpec((1, ML_V_DIM), lambda b, h: (0, h)),
        ],
        out_specs=pl.BlockSpec((S, ML_V_DIM), lambda b, h: (b, h)),
        out_shape=jax.ShapeDtypeStruct((B * S, ML_WIDTH), BF16),
        scratch_shapes=[
            pltpu.VMEM((S, ML_QK_DIM), BF16),
            pltpu.VMEM((S, ML_QK_DIM), F32),
            pltpu.VMEM((ML_QK_DIM, ML_V_DIM), F32),
            pltpu.VMEM((1, ML_QK_DIM), F32),
        ],
        compiler_params=pltpu.CompilerParams(
            dimension_semantics=("arbitrary", "arbitrary"), vmem_limit_bytes=VMEM_LIMIT),
        name="mlstm",
    )(z, z, z, z, z, conv_w, conv_w, conv_b, conv_b, colpack, rowpack4, ml_norm_g)


def _out_kernel(oa_ref, hb_ref, ga_ref, gb_ref, x_ref, wfd_ref, wmd_ref, wo_ref, bga_ref, bgb_ref,
                fg_ref, o_ref):
    ya = _dot(oa_ref[...], wfd_ref[...])
    yb = _dot(hb_ref[...], wmd_ref[...])
    gate_a = jax.nn.sigmoid(ga_ref[...].astype(F32) + bga_ref[...])
    gate_b = jax.nn.sigmoid(gb_ref[...].astype(F32) + bgb_ref[...])
    y = gate_a * ya + gate_b * yb
    r = x_ref[...] + _dot(y.astype(BF16), wo_ref[...])
    ms = jnp.mean(r * r, axis=-1, keepdims=True)
    o_ref[...] = r * lax.rsqrt(ms + EPS) * fg_ref[...]


def _out_stage(oa, hb, z, x2, wfd, wmd, wo, b_gate, final_g):
    T = x2.shape[0]
    row = lambda i: (i, 0)
    const = lambda i: (0, 0)
    return pl.pallas_call(
        _out_kernel,
        grid=(T // TM_OUT,),
        in_specs=[
            pl.BlockSpec((TM_OUT, FOX_WIDTH), row),
            pl.BlockSpec((TM_OUT, ML_WIDTH), row),
            pl.BlockSpec((TM_OUT, D_MODEL), lambda i: (i, ZB_GA // 8)),
            pl.BlockSpec((TM_OUT, D_MODEL), lambda i: (i, ZB_GB // 8)),
            pl.BlockSpec((TM_OUT, D_MODEL), row),
            pl.BlockSpec((FOX_WIDTH, D_MODEL), const),
            pl.BlockSpec((ML_WIDTH, D_MODEL), const),
            pl.BlockSpec((D_MODEL, D_MODEL), const),
            pl.BlockSpec((1, D_MODEL), lambda i: (0, 0)),
            pl.BlockSpec((1, D_MODEL), lambda i: (0, 1)),
            pl.BlockSpec((1, D_MODEL), const),
        ],
        out_specs=pl.BlockSpec((TM_OUT, D_MODEL), row),
        out_shape=jax.ShapeDtypeStruct((T, D_MODEL), F32),
        compiler_params=pltpu.CompilerParams(
            dimension_semantics=("arbitrary",), vmem_limit_bytes=VMEM_LIMIT),
        name="outstage",
    )(oa, hb, z, z, x2, wfd, wmd, wo, b_gate, b_gate, final_g)


def _layer(x2, B, S, norm_g, w_in, b_fox_f, conv_w, conv_b, b_ml_i, b_ml_f, ml_norm_g, b_gate,
           w_fox_down, w_ml_down, w_out, out_g):
    o_ff = 3 * FOX_WIDTH
    o_fz = o_ff + FOX_HEADS
    o_mi = o_fz + FOX_WIDTH + 2 * ML_QK_WIDTH + ML_WIDTH
    o_mf = o_mi + ML_HEADS
    o_mo = o_mf + ML_HEADS
    w_big = jnp.concatenate([w_in[:, :o_ff], w_in[:, o_fz:o_mi], w_in[:, o_mo:]], axis=1).astype(BF16)
    wg = jnp.zeros((D_MODEL, GATE_WIDTH), F32)
    wg = wg.at[:, GL_FOX:GL_FOX + FOX_HEADS].set(w_in[:, o_ff:o_fz])
    wg = wg.at[:, GL_IG:GL_IG + ML_HEADS].set(w_in[:, o_mi:o_mf])
    wg = wg.at[:, GL_LF:GL_LF + ML_HEADS].set(w_in[:, o_mf:o_mo])
    wg_hi = wg.astype(BF16)
    wg_lo = (wg - wg_hi.astype(F32)).astype(BF16)
    bias_row = jnp.zeros((1, GATE_WIDTH), F32)
    bias_row = bias_row.at[0, GL_FOX:GL_FOX + FOX_HEADS].set(b_fox_f)
    bias_row = bias_row.at[0, GL_IG:GL_IG + ML_HEADS].set(b_ml_i)
    bias_row = bias_row.at[0, GL_LF:GL_LF + ML_HEADS].set(b_ml_f)
    col_scale = jnp.ones((1, Z_WIDTH), F32).at[0, :FOX_WIDTH].set(FOX_HEAD_DIM ** -0.5)

    z, zg = _inproj(x2, norm_g.reshape(1, D_MODEL), w_big, col_scale, wg_hi, wg_lo)
    rowpack, colpack = _gates(zg, bias_row, B, S)
    rowpack4 = rowpack.reshape(B, 24, S // TQ, TQ)
    oa = _fox(z, colpack, rowpack4, B, S)
    hb = _mlstm(z, conv_w, conv_b.reshape(1, 2 * ML_QK_WIDTH), colpack, rowpack4,
                ml_norm_g.reshape(1, ML_WIDTH), B, S)
    return _out_stage(oa, hb, z, x2, w_fox_down.astype(BF16), w_ml_down.astype(BF16),
                      w_out.astype(BF16), b_gate.reshape(1, 2 * D_MODEL), out_g.reshape(1, D_MODEL))


def kernel(x, norm_g, w_in, b_fox_f, conv_w, conv_b, b_ml_i, b_ml_f, ml_norm_g, b_gate,
           w_fox_down, w_ml_down, w_out, final_g):
    assert TQ == ML_L
    B, S, _ = x.shape
    depth = norm_g.shape[0]
    assert depth == 1, "the final RMSNorm is fused into the (single) layer's output stage"
    out = _layer(x.reshape(B * S, D_MODEL), B, S, norm_g[0], w_in[0], b_fox_f[0], conv_w[0], conv_b[0],
                 b_ml_i[0], b_ml_f[0], ml_norm_g[0], b_gate[0], w_fox_down[0], w_ml_down[0],
                 w_out[0], final_g)
    return out.reshape(B, S, D_MODEL)
```

```python
import functools

import jax
import jax.numpy as jnp
from jax import lax
from jax.experimental import pallas as pl
from jax.experimental.pallas import tpu as pltpu

F32 = jnp.float32
BF16 = jnp.bfloat16

D_MODEL = 1024
FOX_HEADS = 8
FOX_HEAD_DIM = 128
FOX_WIDTH = FOX_HEADS * FOX_HEAD_DIM
ML_HEADS = 4
ML_QK_DIM = 128
ML_V_DIM = 256
ML_QK_WIDTH = ML_HEADS * ML_QK_DIM
ML_WIDTH = ML_HEADS * ML_V_DIM
CONV_WIDTH = 4
EPS = 1e-6

LANES = 128
Z_WIDTH = 8192
GATE_WIDTH = LANES
VMEM_LIMIT = 56 * 1024 * 1024
LOG2E = 1.4426950408889634
FOX_Q_SCALE = FOX_HEAD_DIM ** -0.5 * LOG2E

ZB_FK, ZB_FZ = 0, 8
ZB_MQ, ZB_MK, ZB_MV, ZB_MO, ZB_MZ = 16, 20, 24, 32, 40
ZB_GA, ZB_GB = 48, 56

GL_FOX, GL_IG, GL_LF = 0, 8, 16
CL_BETA, CL_G, CL_M = 0, 8, 16
RS_BETA = 0
AUG_LANES = 6 * FOX_HEADS

TM_IN, TN_IN = 1024, 2048
TQ = 256
FOX_LOOKAHEAD = 4
FOX_DEN_ROWS = 16
ML_L = 256
TM_OUT = 512

NT_DIMS = (((1,), (1,)), ((), ()))
TN_DIMS = (((0,), (0,)), ((), ()))


def _dot(a, b):
    return jnp.dot(a, b, preferred_element_type=F32)


def _inproj_kernel(x_ref, g_ref, w_ref, wqt_ref, wvt_ref, wgh_ref, wgl_ref,
                   z_ref, qt_ref, vt_ref, zg_ref, xh_ref):
    @pl.when(pl.program_id(1) == 0)
    def _():
        x = x_ref[...]
        ms = jnp.mean(x * x, axis=-1, keepdims=True)
        xn = x * lax.rsqrt(ms + EPS) * g_ref[...]
        xh = xn.astype(BF16)
        xh_ref[...] = xh
        qt = lax.dot_general(wqt_ref[...], xh, NT_DIMS, preferred_element_type=F32)
        qt_ref[...] = (qt * FOX_Q_SCALE).astype(BF16)
        vt_ref[...] = lax.dot_general(wvt_ref[...], xh, NT_DIMS, preferred_element_type=F32).astype(BF16)
        xl = (xn - xh.astype(F32)).astype(BF16)
        zg_ref[...] = _dot(xh, wgh_ref[...]) + _dot(xh, wgl_ref[...]) + _dot(xl, wgh_ref[...])

    z_ref[...] = _dot(xh_ref[...], w_ref[...]).astype(BF16)


def _inproj(x2, norm_g, w_big, wq_t, wv_t, wg_hi, wg_lo):
    T = x2.shape[0]
    grid = (T // TM_IN, Z_WIDTH // TN_IN)
    return pl.pallas_call(
        _inproj_kernel,
        grid=grid,
        in_specs=[
            pl.BlockSpec((TM_IN, D_MODEL), lambda i, j: (i, 0)),
            pl.BlockSpec((1, D_MODEL), lambda i, j: (0, 0)),
            pl.BlockSpec((D_MODEL, TN_IN), lambda i, j: (0, j)),
            pl.BlockSpec((FOX_WIDTH, D_MODEL), lambda i, j: (0, 0)),
            pl.BlockSpec((FOX_WIDTH, D_MODEL), lambda i, j: (0, 0)),
            pl.BlockSpec((D_MODEL, GATE_WIDTH), lambda i, j: (0, 0)),
            pl.BlockSpec((D_MODEL, GATE_WIDTH), lambda i, j: (0, 0)),
        ],
        out_specs=[
            pl.BlockSpec((TM_IN, TN_IN), lambda i, j: (i, j)),
            pl.BlockSpec((FOX_WIDTH, TM_IN), lambda i, j: (0, i)),
            pl.BlockSpec((FOX_WIDTH, TM_IN), lambda i, j: (0, i)),
            pl.BlockSpec((TM_IN, GATE_WIDTH), lambda i, j: (i, 0)),
        ],
        out_shape=[
            jax.ShapeDtypeStruct((T, Z_WIDTH), BF16),
            jax.ShapeDtypeStruct((FOX_WIDTH, T), BF16),
            jax.ShapeDtypeStruct((FOX_WIDTH, T), BF16),
            jax.ShapeDtypeStruct((T, GATE_WIDTH), F32),
        ],
        scratch_shapes=[pltpu.VMEM((TM_IN, D_MODEL), BF16)],
        compiler_params=pltpu.CompilerParams(
            dimension_semantics=("arbitrary", "arbitrary"), vmem_limit_bytes=VMEM_LIMIT),
        name="inproj",
    )(x2, norm_g, w_big, wq_t, wv_t, wg_hi, wg_lo)


def _scan_lanes(x, op, ident):
    n = x.shape[1]
    lane = lax.broadcasted_iota(jnp.int32, x.shape, 1)
    k = 1
    while k < n:
        shifted = pltpu.roll(x, k, 1)
        x = op(x, jnp.where(lane >= k, shifted, ident))
        k *= 2
    return x


def _log_sigmoid(x):
    return jnp.minimum(x, 0.0) - jnp.log1p(jnp.exp(-jnp.abs(x)))


def _split3(x):
    hi = x.astype(BF16).astype(F32)
    r = x - hi
    mid = r.astype(BF16).astype(F32)
    lo = (r - mid).astype(BF16).astype(F32)
    return hi, mid, lo


def _gate_kernel(zg_ref, bias_ref, row_ref, col_ref, aq_ref, ak_ref):
    pre = zg_ref[...] + bias_ref[...]
    t = jnp.transpose(pre)
    S = t.shape[1]
    c = _scan_lanes(_log_sigmoid(t[GL_FOX:GL_FOX + 8]), jnp.add, 0.0)
    ig = t[GL_IG:GL_IG + 8]
    fsum = _scan_lanes(_log_sigmoid(t[GL_LF:GL_LF + 8]), jnp.add, 0.0)
    beta = ig - fsum
    gmax = jnp.maximum(_scan_lanes(beta, jnp.maximum, -jnp.inf), 0.0)
    m = fsum + gmax
    row_ref[...] = beta
    packed = jnp.concatenate([beta, gmax, m, jnp.zeros((LANES - 24, S), F32)], axis=0)
    col_ref[...] = jnp.transpose(packed)
    hi, mid, lo = _split3(c * LOG2E)
    ones = jnp.ones((8, S), F32)
    pad = jnp.zeros((LANES - AUG_LANES, S), F32)
    aq_ref[...] = jnp.concatenate([hi, mid, lo, ones, ones, ones, pad], axis=0).astype(BF16)
    ak_ref[...] = jnp.transpose(jnp.concatenate([ones, ones, ones, -hi, -mid, -lo, pad], axis=0)).astype(BF16)


def _gates(zg, bias_row, B, S):
    return pl.pallas_call(
        _gate_kernel,
        grid=(B,),
        in_specs=[
            pl.BlockSpec((S, GATE_WIDTH), lambda b: (b, 0)),
            pl.BlockSpec((1, GATE_WIDTH), lambda b: (0, 0)),
        ],
        out_specs=[
            pl.BlockSpec((None, 8, S), lambda b: (b, 0, 0)),
            pl.BlockSpec((S, LANES), lambda b: (b, 0)),
            pl.BlockSpec((None, LANES, S), lambda b: (b, 0, 0)),
            pl.BlockSpec((S, LANES), lambda b: (b, 0)),
        ],
        out_shape=[
            jax.ShapeDtypeStruct((B, 8, S), F32),
            jax.ShapeDtypeStruct((B * S, LANES), F32),
            jax.ShapeDtypeStruct((B, LANES, S), BF16),
            jax.ShapeDtypeStruct((B * S, LANES), BF16),
        ],
        compiler_params=pltpu.CompilerParams(
            dimension_semantics=("arbitrary",), vmem_limit_bytes=VMEM_LIMIT),
        name="gates",
    )(zg, bias_row)


def _pick_lane(tile, lane_idx):
    lane = lax.broadcasted_iota(jnp.int32, tile.shape, 1)
    return jnp.sum(jnp.where(lane == lane_idx, tile, 0.0), axis=1, keepdims=True)


def _fox_kernel(qt_ref, k_ref, vt_ref, fz_ref, aqt_ref, ak_ref, o_ref, qaug_ref, kaug_ref, vaug_ref):
    h = pl.program_id(1)
    S = k_ref.shape[0]
    d = FOX_HEAD_DIM
    lane = lax.broadcasted_iota(jnp.int32, (1, LANES), 1)
    keep = jnp.where(((lane & (FOX_HEADS - 1)) == h) & (lane < AUG_LANES), 1.0, 0.0).astype(BF16)
    qaug_ref[:d, :] = qt_ref[...]
    qaug_ref[d:, :] = aqt_ref[...]
    kaug_ref[:, :LANES] = k_ref[...]
    kaug_ref[:, LANES:] = ak_ref[...] * keep
    vaug_ref[:d, :] = vt_ref[...]
    vaug_ref[d:, :] = jnp.ones((FOX_DEN_ROWS, S), BF16)
    s_idx = lax.broadcasted_iota(jnp.int32, (TQ, TQ), 0)
    t_idx = lax.broadcasted_iota(jnp.int32, (TQ, TQ), 1)
    causal = s_idx <= t_idx

    ng = S // TQ
    m = [None] * ng
    acc = [None] * ng
    tiles = [(kj, g) for kj in range(ng) for g in range(kj, ng)]

    def scores(kj, g):
        ka = kaug_ref[kj * TQ:(kj + 1) * TQ, :]
        qa = qaug_ref[:, g * TQ:(g + 1) * TQ]
        return _dot(ka, qa)

    def finish(kj, g, st):
        va = vaug_ref[:, kj * TQ:(kj + 1) * TQ]
        if kj == g:
            st = jnp.where(causal, st, -jnp.inf)
        cm = jnp.max(st, axis=0, keepdims=True)
        if kj == 0:
            m[g] = cm
            acc[g] = _dot(va, jnp.exp2(st - cm).astype(BF16))
        else:
            m_new = jnp.maximum(m[g], cm)
            alpha = jnp.exp2(m[g] - m_new)
            acc[g] = alpha * acc[g] + _dot(va, jnp.exp2(st - m_new).astype(BF16))
            m[g] = m_new
        if kj == g:
            a = acc[g]
            o = jnp.transpose(a[:d] * (1.0 / a[d:d + 1]))
            fz = fz_ref[g * TQ:(g + 1) * TQ, :].astype(F32)
            o_ref[g * TQ:(g + 1) * TQ, :] = (o * (fz * jax.nn.sigmoid(fz))).astype(BF16)

    pending = [scores(*tiles[i]) for i in range(FOX_LOOKAHEAD)]
    for i, (kj, g) in enumerate(tiles):
        if i + FOX_LOOKAHEAD < len(tiles):
            pending.append(scores(*tiles[i + FOX_LOOKAHEAD]))
        finish(kj, g, pending.pop(0))


def _fox(qt, z, vt, aqt, ak, B, S):
    return pl.pallas_call(
        _fox_kernel,
        grid=(B, FOX_HEADS),
        in_specs=[
            pl.BlockSpec((FOX_HEAD_DIM, S), lambda b, h: (h, b)),
            pl.BlockSpec((S, LANES), lambda b, h: (b, ZB_FK + h)),
            pl.BlockSpec((FOX_HEAD_DIM, S), lambda b, h: (h, b)),
            pl.BlockSpec((S, LANES), lambda b, h: (b, ZB_FZ + h)),
            pl.BlockSpec((None, LANES, S), lambda b, h: (b, 0, 0)),
            pl.BlockSpec((S, LANES), lambda b, h: (b, 0)),
        ],
        out_specs=pl.BlockSpec((S, LANES), lambda b, h: (b, h)),
        out_shape=jax.ShapeDtypeStruct((B * S, FOX_WIDTH), BF16),
        scratch_shapes=[
            pltpu.VMEM((2 * LANES, S), BF16),
            pltpu.VMEM((S, 2 * LANES), BF16),
            pltpu.VMEM((FOX_HEAD_DIM + FOX_DEN_ROWS, S), BF16),
        ],
        compiler_params=pltpu.CompilerParams(
            dimension_semantics=("arbitrary", "arbitrary"), vmem_limit_bytes=VMEM_LIMIT),
        name="fox",
    )(qt, z, vt, z, aqt, ak)


def _conv_silu(x_ref, w_ref, b_ref):
    x = x_ref[...].astype(F32)
    row = lax.broadcasted_iota(jnp.int32, x.shape, 0)
    w = w_ref[...]
    y = x * w[CONV_WIDTH - 1:CONV_WIDTH, :] + b_ref[...]
    for r in range(1, CONV_WIDTH):
        xs = jnp.where(row >= r, pltpu.roll(x, r, 0), 0.0)
        y = y + xs * w[CONV_WIDTH - 1 - r:CONV_WIDTH - r, :]
    return y * jax.nn.sigmoid(y)


def _mlstm_kernel(q_ref, k_ref, v_ref, mo_ref, mz_ref, cwq_ref, cwk_ref, cbq_ref, cbk_ref,
                  col_ref, brow_ref, ng_ref, o_ref, qs_ref, kf_ref, c_ref, n_ref):
    h = pl.program_id(1)
    S = q_ref.shape[0]
    L = ML_L
    qs_ref[...] = (_conv_silu(q_ref, cwq_ref, cbq_ref) * (ML_QK_DIM ** -0.5)).astype(BF16)
    kf_ref[...] = _conv_silu(k_ref, cwk_ref, cbk_ref)
    c_ref[...] = jnp.zeros_like(c_ref)
    n_ref[...] = jnp.zeros_like(n_ref)
    ti = lax.broadcasted_iota(jnp.int32, (L, L), 0)
    si = lax.broadcasted_iota(jnp.int32, (L, L), 1)

    def chunk(c, g_prev):
        r0 = pl.multiple_of(c * L, L)
        q = qs_ref[pl.ds(r0, L), :]
        kf = kf_ref[pl.ds(r0, L), :]
        v = v_ref[pl.ds(r0, L), :]
        colp = col_ref[pl.ds(r0, L), :]
        b_col = _pick_lane(colp, CL_BETA + h)
        g_col = _pick_lane(colp, CL_G + h)
        m_col = _pick_lane(colp, CL_M + h)
        b_row = brow_ref[pl.ds(c, 1), :]
        sqk = lax.dot_general(q, kf.astype(BF16), NT_DIMS, preferred_element_type=F32)
        sw = sqk * jnp.where(si <= ti, jnp.exp(b_row - g_col), 0.0)
        w_inter = jnp.exp(g_prev - g_col)
        num = _dot(sw.astype(BF16), v) + w_inter * _dot(q, c_ref[...].astype(BF16))
        den = (jnp.sum(sw, axis=1, keepdims=True)
               + w_inter * jnp.sum(q.astype(F32) * n_ref[...], axis=1, keepdims=True))
        hh = num * (1.0 / jnp.maximum(jnp.abs(den), jnp.exp(-m_col)))
        ms = jnp.mean(hh * hh, axis=1, keepdims=True)
        hn = hh * lax.rsqrt(ms + EPS) * ng_ref[...]
        mo = mo_ref[pl.ds(r0, L), :].astype(F32)
        mz = mz_ref[pl.ds(r0, L), :].astype(F32)
        o_ref[pl.ds(r0, L), :] = (hn * jax.nn.sigmoid(mo) * (mz * jax.nn.sigmoid(mz))).astype(BF16)
        g_last = g_col[L - 1:L, :]
        kw = kf * jnp.exp(b_col - g_last)
        decay = jnp.exp(g_prev - g_last)
        c_ref[...] = decay * c_ref[...] + lax.dot_general(
            kw.astype(BF16), v, TN_DIMS, preferred_element_type=F32)
        n_ref[...] = decay * n_ref[...] + jnp.sum(kw, axis=0, keepdims=True)
        return g_last

    lax.fori_loop(0, S // L, chunk, jnp.zeros((1, 1), F32))


def _mlstm(z, conv_w, conv_b, colpack, rowpack4, ml_norm_g, B, S):
    nc = S // ML_L
    return pl.pallas_call(
        _mlstm_kernel,
        grid=(B, ML_HEADS),
        in_specs=[
            pl.BlockSpec((S, LANES), lambda b, h: (b, ZB_MQ + h)),
            pl.BlockSpec((S, LANES), lambda b, h: (b, ZB_MK + h)),
            pl.BlockSpec((S, ML_V_DIM), lambda b, h: (b, ZB_MV // 2 + h)),
            pl.BlockSpec((S, ML_V_DIM), lambda b, h: (b, ZB_MO // 2 + h)),
            pl.BlockSpec((S, ML_V_DIM), lambda b, h: (b, ZB_MZ // 2 + h)),
            pl.BlockSpec((CONV_WIDTH, LANES), lambda b, h: (0, h)),
            pl.BlockSpec((CONV_WIDTH, LANES), lambda b, h: (0, ML_HEADS + h)),
            pl.BlockSpec((1, LANES), lambda b, h: (0, h)),
            pl.BlockSpec((1, LANES), lambda b, h: (0, ML_HEADS + h)),
            pl.BlockSpec((S, LANES), lambda b, h: (b, 0)),
            pl.BlockSpec((None, None, nc, ML_L), lambda b, h: (b, RS_BETA + h, 0, 0)),
            pl.BlockSpec((1, ML_V_DIM), lambda b, h: (0, h)),
        ],
        out_specs=pl.BlockSpec((S, ML_V_DIM), lambda b, h: (b, h)),
        out_shape=jax.ShapeDtypeStruct((B * S, ML_WIDTH), BF16),
        scratch_shapes=[
            pltpu.VMEM((S, ML_QK_DIM), BF16),
            pltpu.VMEM((S, ML_QK_DIM), F32),
            pltpu.VMEM((ML_QK_DIM, ML_V_DIM), F32),
            pltpu.VMEM((1, ML_QK_DIM), F32),
        ],
        compiler_params=pltpu.CompilerParams(
            dimension_semantics=("arbitrary", "arbitrary"), vmem_limit_bytes=VMEM_LIMIT),
        name="mlstm",
    )(z, z, z, z, z, conv_w, conv_w, conv_b, conv_b, colpack, rowpack4, ml_norm_g)


def _out_kernel(oa_ref, hb_ref, ga_ref, gb_ref, x_ref, wfd_ref, wmd_ref, wo_ref, bga_ref, bgb_ref,
                fg_ref, o_ref):
    ya = _dot(oa_ref[...], wfd_ref[...])
    yb = _dot(hb_ref[...], wmd_ref[...])
    gate_a = jax.nn.sigmoid(ga_ref[...].astype(F32) + bga_ref[...])
    gate_b = jax.nn.sigmoid(gb_ref[...].astype(F32) + bgb_ref[...])
    y = gate_a * ya + gate_b * yb
    r = x_ref[...] + _dot(y.astype(BF16), wo_ref[...])
    ms = jnp.mean(r * r, axis=-1, keepdims=True)
    o_ref[...] = r * lax.rsqrt(ms + EPS) * fg_ref[...]


def _out_stage(oa, hb, z, x2, wfd, wmd, wo, b_gate, final_g):
    T = x2.shape[0]
    row = lambda i: (i, 0)
    const = lambda i: (0, 0)
    return pl.pallas_call(
        _out_kernel,
        grid=(T // TM_OUT,),
        in_specs=[
            pl.BlockSpec((TM_OUT, FOX_WIDTH), row),
            pl.BlockSpec((TM_OUT, ML_WIDTH), row),
            pl.BlockSpec((TM_OUT, D_MODEL), lambda i: (i, ZB_GA // 8)),
            pl.BlockSpec((TM_OUT, D_MODEL), lambda i: (i, ZB_GB // 8)),
            pl.BlockSpec((TM_OUT, D_MODEL), row),
            pl.BlockSpec((FOX_WIDTH, D_MODEL), const),
            pl.BlockSpec((ML_WIDTH, D_MODEL), const),
            pl.BlockSpec((D_MODEL, D_MODEL), const),
            pl.BlockSpec((1, D_MODEL), lambda i: (0, 0)),
            pl.BlockSpec((1, D_MODEL), lambda i: (0, 1)),
            pl.BlockSpec((1, D_MODEL), const),
        ],
        out_specs=pl.BlockSpec((TM_OUT, D_MODEL), row),
        out_shape=jax.ShapeDtypeStruct((T, D_MODEL), F32),
        compiler_params=pltpu.CompilerParams(
            dimension_semantics=("arbitrary",), vmem_limit_bytes=VMEM_LIMIT),
        name="outstage",
    )(oa, hb, z, z, x2, wfd, wmd, wo, b_gate, b_gate, final_g)


def _layer(x2, B, S, norm_g, w_in, b_fox_f, conv_w, conv_b, b_ml_i, b_ml_f, ml_norm_g, b_gate,
           w_fox_down, w_ml_down, w_out, out_g):
    o_ff = 3 * FOX_WIDTH
    o_fz = o_ff + FOX_HEADS
    o_mi = o_fz + FOX_WIDTH + 2 * ML_QK_WIDTH + ML_WIDTH
    o_mf = o_mi + ML_HEADS
    o_mo = o_mf + ML_HEADS
    w_big = jnp.concatenate([w_in[:, FOX_WIDTH:2 * FOX_WIDTH], w_in[:, o_fz:o_mi], w_in[:, o_mo:]],
                            axis=1).astype(BF16)
    wq_t = jnp.transpose(w_in[:, :FOX_WIDTH]).astype(BF16)
    wv_t = jnp.transpose(w_in[:, 2 * FOX_WIDTH:o_ff]).astype(BF16)
    wg = jnp.zeros((D_MODEL, GATE_WIDTH), F32)
    wg = wg.at[:, GL_FOX:GL_FOX + FOX_HEADS].set(w_in[:, o_ff:o_fz])
    wg = wg.at[:, GL_IG:GL_IG + ML_HEADS].set(w_in[:, o_mi:o_mf])
    wg = wg.at[:, GL_LF:GL_LF + ML_HEADS].set(w_in[:, o_mf:o_mo])
    wg_hi = wg.astype(BF16)
    wg_lo = (wg - wg_hi.astype(F32)).astype(BF16)
    bias_row = jnp.zeros((1, GATE_WIDTH), F32)
    bias_row = bias_row.at[0, GL_FOX:GL_FOX + FOX_HEADS].set(b_fox_f)
    bias_row = bias_row.at[0, GL_IG:GL_IG + ML_HEADS].set(b_ml_i)
    bias_row = bias_row.at[0, GL_LF:GL_LF + ML_HEADS].set(b_ml_f)

    z, qt, vt, zg = _inproj(x2, norm_g.reshape(1, D_MODEL), w_big, wq_t, wv_t, wg_hi, wg_lo)
    rowpack, colpack, aqt, ak = _gates(zg, bias_row, B, S)
    rowpack4 = rowpack.reshape(B, 8, S // ML_L, ML_L)
    oa = _fox(qt, z, vt, aqt, ak, B, S)
    hb = _mlstm(z, conv_w, conv_b.reshape(1, 2 * ML_QK_WIDTH), colpack, rowpack4,
                ml_norm_g.reshape(1, ML_WIDTH), B, S)
    return _out_stage(oa, hb, z, x2, w_fox_down.astype(BF16), w_ml_down.astype(BF16),
                      w_out.astype(BF16), b_gate.reshape(1, 2 * D_MODEL), out_g.reshape(1, D_MODEL))


def kernel(x, norm_g, w_in, b_fox_f, conv_w, conv_b, b_ml_i, b_ml_f, ml_norm_g, b_gate,
           w_fox_down, w_ml_down, w_out, final_g):
    assert TQ == ML_L
    B, S, _ = x.shape
    depth = norm_g.shape[0]
    assert depth == 1, "the final RMSNorm is fused into the (single) layer's output stage"
    out = _layer(x.reshape(B * S, D_MODEL), B, S, norm_g[0], w_in[0], b_fox_f[0], conv_w[0], conv_b[0],
                 b_ml_i[0], b_ml_f[0], ml_norm_g[0], b_gate[0], w_fox_down[0], w_ml_down[0],
                 w_out[0], final_g)
    return out.reshape(B, S, D_MODEL)
```

```python
import functools

import jax
import jax.numpy as jnp
from jax import lax
from jax.experimental import pallas as pl
from jax.experimental.pallas import tpu as pltpu

F32 = jnp.float32
BF16 = jnp.bfloat16

D_MODEL = 1024
FOX_HEADS = 8
FOX_HEAD_DIM = 128
FOX_WIDTH = FOX_HEADS * FOX_HEAD_DIM
ML_HEADS = 4
ML_QK_DIM = 128
ML_V_DIM = 256
ML_QK_WIDTH = ML_HEADS * ML_QK_DIM
ML_WIDTH = ML_HEADS * ML_V_DIM
CONV_WIDTH = 4
EPS = 1e-6

LANES = 128
Z_WIDTH = 8192
GATE_WIDTH = LANES
VMEM_LIMIT = 56 * 1024 * 1024
LOG2E = 1.4426950408889634
FOX_Q_SCALE = FOX_HEAD_DIM ** -0.5 * LOG2E

ZB_FK, ZB_FZ = 0, 8
ZB_MQ, ZB_MK, ZB_MV, ZB_MO, ZB_MZ = 16, 20, 24, 32, 40
ZB_GA, ZB_GB = 48, 56

GL_FOX, GL_IG, GL_LF = 0, 8, 16
CL_BETA, CL_G, CL_M = 0, 8, 16
RS_BETA = 0
AUG_LANES = 6 * FOX_HEADS

TM_IN, TN_IN = 1024, 2048
TQ = 256
FOX_LOOKAHEAD = 4
FOX_DEN_ROWS = 16
ML_L = 256
ML_LOOKAHEAD = 2
TM_OUT = 512

NT_DIMS = (((1,), (1,)), ((), ()))
TN_DIMS = (((0,), (0,)), ((), ()))


def _dot(a, b):
    return jnp.dot(a, b, preferred_element_type=F32)


def _inproj_kernel(x_ref, g_ref, w_ref, wfm_ref, z_ref, qt_ref, vt_ref, zg_ref, xh_ref):
    @pl.when(pl.program_id(1) == 0)
    def _():
        x = x_ref[...]
        ms = jnp.mean(x * x, axis=-1, keepdims=True)
        xn = x * lax.rsqrt(ms + EPS) * g_ref[...]
        xh = xn.astype(BF16)
        xh_ref[...] = xh
        fm = lax.dot_general(wfm_ref[...], xh, NT_DIMS, preferred_element_type=F32)
        qt_ref[...] = (fm[:FOX_WIDTH] * FOX_Q_SCALE).astype(BF16)
        vt_ref[...] = fm[FOX_WIDTH:2 * FOX_WIDTH].astype(BF16)
        zg_ref[...] = fm[2 * FOX_WIDTH:]

    z_ref[...] = _dot(xh_ref[...], w_ref[...]).astype(BF16)


def _inproj(x2, norm_g, w_big, w_fm):
    T = x2.shape[0]
    grid = (T // TM_IN, Z_WIDTH // TN_IN)
    fm_rows = 2 * FOX_WIDTH + GATE_WIDTH
    return pl.pallas_call(
        _inproj_kernel,
        grid=grid,
        in_specs=[
            pl.BlockSpec((TM_IN, D_MODEL), lambda i, j: (i, 0)),
            pl.BlockSpec((1, D_MODEL), lambda i, j: (0, 0)),
            pl.BlockSpec((D_MODEL, TN_IN), lambda i, j: (0, j)),
            pl.BlockSpec((fm_rows, D_MODEL), lambda i, j: (0, 0)),
        ],
        out_specs=[
            pl.BlockSpec((TM_IN, TN_IN), lambda i, j: (i, j)),
            pl.BlockSpec((FOX_WIDTH, TM_IN), lambda i, j: (0, i)),
            pl.BlockSpec((FOX_WIDTH, TM_IN), lambda i, j: (0, i)),
            pl.BlockSpec((GATE_WIDTH, TM_IN), lambda i, j: (0, i)),
        ],
        out_shape=[
            jax.ShapeDtypeStruct((T, Z_WIDTH), BF16),
            jax.ShapeDtypeStruct((FOX_WIDTH, T), BF16),
            jax.ShapeDtypeStruct((FOX_WIDTH, T), BF16),
            jax.ShapeDtypeStruct((GATE_WIDTH, T), F32),
        ],
        scratch_shapes=[pltpu.VMEM((TM_IN, D_MODEL), BF16)],
        compiler_params=pltpu.CompilerParams(
            dimension_semantics=("arbitrary", "arbitrary"), vmem_limit_bytes=VMEM_LIMIT),
        name="inproj",
    )(x2, norm_g, w_big, w_fm)


def _scan_lanes(x, op, ident):
    n = x.shape[1]
    lane = lax.broadcasted_iota(jnp.int32, x.shape, 1)
    k = 1
    while k < n:
        shifted = pltpu.roll(x, k, 1)
        x = op(x, jnp.where(lane >= k, shifted, ident))
        k *= 2
    return x


def _log_sigmoid(x):
    return jnp.minimum(x, 0.0) - jnp.log1p(jnp.exp(-jnp.abs(x)))


def _split3(x):
    hi = x.astype(BF16).astype(F32)
    r = x - hi
    mid = r.astype(BF16).astype(F32)
    lo = (r - mid).astype(BF16).astype(F32)
    return hi, mid, lo


def _gate_kernel(zg_ref, bias_ref, row_ref, col_ref, aq_ref, ak_ref):
    t = zg_ref[...] + bias_ref[...]
    S = t.shape[1]
    c = _scan_lanes(_log_sigmoid(t[GL_FOX:GL_FOX + 8]), jnp.add, 0.0)
    ig = t[GL_IG:GL_IG + 8]
    fsum = _scan_lanes(_log_sigmoid(t[GL_LF:GL_LF + 8]), jnp.add, 0.0)
    beta = (ig - fsum) * LOG2E
    gmax = jnp.maximum(_scan_lanes(beta, jnp.maximum, -jnp.inf), 0.0)
    m = fsum * LOG2E + gmax
    row_ref[...] = beta
    packed = jnp.concatenate([beta, gmax, m, jnp.zeros((LANES - 24, S), F32)], axis=0)
    col_ref[...] = jnp.transpose(packed)
    hi, mid, lo = _split3(c * LOG2E)
    ones = jnp.ones((8, S), F32)
    pad = jnp.zeros((LANES - AUG_LANES, S), F32)
    aq_ref[...] = jnp.concatenate([hi, mid, lo, ones, ones, ones, pad], axis=0).astype(BF16)
    ak_ref[...] = jnp.transpose(jnp.concatenate([ones, ones, ones, -hi, -mid, -lo, pad], axis=0)).astype(BF16)


def _gates(zg, bias_row, B, S):
    return pl.pallas_call(
        _gate_kernel,
        grid=(B,),
        in_specs=[
            pl.BlockSpec((GATE_WIDTH, S), lambda b: (0, b)),
            pl.BlockSpec((GATE_WIDTH, 1), lambda b: (0, 0)),
        ],
        out_specs=[
            pl.BlockSpec((None, 8, S), lambda b: (b, 0, 0)),
            pl.BlockSpec((S, LANES), lambda b: (b, 0)),
            pl.BlockSpec((None, LANES, S), lambda b: (b, 0, 0)),
            pl.BlockSpec((S, LANES), lambda b: (b, 0)),
        ],
        out_shape=[
            jax.ShapeDtypeStruct((B, 8, S), F32),
            jax.ShapeDtypeStruct((B * S, LANES), F32),
            jax.ShapeDtypeStruct((B, LANES, S), BF16),
            jax.ShapeDtypeStruct((B * S, LANES), BF16),
        ],
        compiler_params=pltpu.CompilerParams(
            dimension_semantics=("arbitrary",), vmem_limit_bytes=VMEM_LIMIT),
        name="gates",
    )(zg, bias_row)


def _fox_kernel(qt_ref, k_ref, vt_ref, fz_ref, aqt_ref, ak_ref, o_ref, qaug_ref, kaug_ref, vaug_ref):
    h = pl.program_id(1)
    S = k_ref.shape[0]
    d = FOX_HEAD_DIM
    lane = lax.broadcasted_iota(jnp.int32, (1, LANES), 1)
    keep = jnp.where(((lane & (FOX_HEADS - 1)) == h) & (lane < AUG_LANES), 1.0, 0.0).astype(BF16)
    qaug_ref[:d, :] = qt_ref[...]
    qaug_ref[d:, :] = aqt_ref[...]
    kaug_ref[:, :LANES] = k_ref[...]
    kaug_ref[:, LANES:] = ak_ref[...] * keep
    vaug_ref[:d, :] = vt_ref[...]
    vaug_ref[d:, :] = jnp.ones((FOX_DEN_ROWS, S), BF16)
    s_idx = lax.broadcasted_iota(jnp.int32, (TQ, TQ), 0)
    t_idx = lax.broadcasted_iota(jnp.int32, (TQ, TQ), 1)
    causal = s_idx <= t_idx

    ng = S // TQ
    m = [None] * ng
    acc = [None] * ng
    tiles = [(kj, g) for kj in range(ng) for g in range(kj, ng)]

    def scores(kj, g):
        ka = kaug_ref[kj * TQ:(kj + 1) * TQ, :]
        qa = qaug_ref[:, g * TQ:(g + 1) * TQ]
        return _dot(ka, qa)

    def finish(kj, g, st):
        va = vaug_ref[:, kj * TQ:(kj + 1) * TQ]
        if kj == g:
            st = jnp.where(causal, st, -jnp.inf)
        cm = jnp.max(st, axis=0, keepdims=True)
        if kj == 0:
            m[g] = cm
            acc[g] = _dot(va, jnp.exp2(st - cm).astype(BF16))
        else:
            m_new = jnp.maximum(m[g], cm)
            alpha = jnp.exp2(m[g] - m_new)
            acc[g] = alpha * acc[g] + _dot(va, jnp.exp2(st - m_new).astype(BF16))
            m[g] = m_new
        if kj == g:
            a = acc[g]
            o = jnp.transpose(a[:d] * (1.0 / a[d:d + 1]))
            fz = fz_ref[g * TQ:(g + 1) * TQ, :].astype(F32)
            o_ref[g * TQ:(g + 1) * TQ, :] = (o * (fz * jax.nn.sigmoid(fz))).astype(BF16)

    pending = [scores(*tiles[i]) for i in range(FOX_LOOKAHEAD)]
    for i, (kj, g) in enumerate(tiles):
        if i + FOX_LOOKAHEAD < len(tiles):
            pending.append(scores(*tiles[i + FOX_LOOKAHEAD]))
        finish(kj, g, pending.pop(0))


def _fox(qt, z, vt, aqt, ak, B, S):
    return pl.pallas_call(
        _fox_kernel,
        grid=(B, FOX_HEADS),
        in_specs=[
            pl.BlockSpec((FOX_HEAD_DIM, S), lambda b, h: (h, b)),
            pl.BlockSpec((S, LANES), lambda b, h: (b, ZB_FK + h)),
            pl.BlockSpec((FOX_HEAD_DIM, S), lambda b, h: (h, b)),
            pl.BlockSpec((S, LANES), lambda b, h: (b, ZB_FZ + h)),
            pl.BlockSpec((None, LANES, S), lambda b, h: (b, 0, 0)),
            pl.BlockSpec((S, LANES), lambda b, h: (b, 0)),
        ],
        out_specs=pl.BlockSpec((S, LANES), lambda b, h: (b, h)),
        out_shape=jax.ShapeDtypeStruct((B * S, FOX_WIDTH), BF16),
        scratch_shapes=[
            pltpu.VMEM((2 * LANES, S), BF16),
            pltpu.VMEM((S, 2 * LANES), BF16),
            pltpu.VMEM((FOX_HEAD_DIM + FOX_DEN_ROWS, S), BF16),
        ],
        compiler_params=pltpu.CompilerParams(
            dimension_semantics=("arbitrary", "arbitrary"), vmem_limit_bytes=VMEM_LIMIT),
        name="fox",
    )(qt, z, vt, z, aqt, ak)


def _conv_silu(x_ref, w_ref, b_ref):
    x = x_ref[...].astype(F32)
    row = lax.broadcasted_iota(jnp.int32, x.shape, 0)
    w = w_ref[...]
    y = x * w[CONV_WIDTH - 1:CONV_WIDTH, :] + b_ref[...]
    for r in range(1, CONV_WIDTH):
        xs = jnp.where(row >= r, pltpu.roll(x, r, 0), 0.0)
        y = y + xs * w[CONV_WIDTH - 1 - r:CONV_WIDTH - r, :]
    return y * jax.nn.sigmoid(y)


def _mlstm_kernel(q_ref, k_ref, v_ref, mo_ref, mz_ref, cwq_ref, cwk_ref, cbq_ref, cbk_ref,
                  col_ref, brow_ref, ng_ref, o_ref, qs_ref, kf_ref):
    h = pl.program_id(1)
    S = q_ref.shape[0]
    L = ML_L
    nc = S // L
    qs_ref[...] = (_conv_silu(q_ref, cwq_ref, cbq_ref) * (ML_QK_DIM ** -0.5)).astype(BF16)
    kf_ref[...] = _conv_silu(k_ref, cwk_ref, cbk_ref)
    ti = lax.broadcasted_iota(jnp.int32, (L, L), 0)
    si = lax.broadcasted_iota(jnp.int32, (L, L), 1)
    causal = si <= ti
    head_shift = (LANES - h) % LANES
    ng = ng_ref[...]

    def scores(c):
        q = qs_ref[c * L:(c + 1) * L, :]
        kb = kf_ref[c * L:(c + 1) * L, :].astype(BF16)
        return lax.dot_general(q, kb, NT_DIMS, preferred_element_type=F32)

    def finish(c, sqk, state):
        cmat, n_row, g_prev = state
        rows = slice(c * L, (c + 1) * L)
        q = qs_ref[rows, :]
        kf = kf_ref[rows, :]
        v = v_ref[rows, :]
        colp = pltpu.roll(col_ref[rows, :], head_shift, 1)
        b_col = colp[:, CL_BETA:CL_BETA + 1]
        g_col = colp[:, CL_G:CL_G + 1]
        m_col = colp[:, CL_M:CL_M + 1]
        b_row = brow_ref[c:c + 1, :]
        g_last = g_col[L - 1:L, :]
        kw = kf * jnp.exp2(b_col - g_last)
        upd = lax.dot_general(kw.astype(BF16), v, TN_DIMS, preferred_element_type=F32)
        sw = sqk * jnp.where(causal, jnp.exp2(b_row - g_col), 0.0)
        num = _dot(sw.astype(BF16), v)
        den = jnp.sum(sw, axis=1, keepdims=True)
        if c > 0:
            w_inter = jnp.exp2(g_prev - g_col)
            num = num + w_inter * _dot(q, cmat.astype(BF16))
            den = den + w_inter * jnp.sum(q.astype(F32) * n_row, axis=1, keepdims=True)
        inv = 1.0 / jnp.maximum(jnp.abs(den), jnp.exp2(-m_col))
        ssq = jnp.sum(num * num, axis=1, keepdims=True)
        row_scale = inv * lax.rsqrt(inv * inv * ssq * (1.0 / ML_V_DIM) + EPS)
        mo = mo_ref[rows, :].astype(F32)
        mz = mz_ref[rows, :].astype(F32)
        gate = mz * (1.0 / ((1.0 + jnp.exp2(mo * -LOG2E)) * (1.0 + jnp.exp2(mz * -LOG2E))))
        o_ref[rows, :] = (num * row_scale * (ng * gate)).astype(BF16)
        if c == 0:
            return upd, jnp.sum(kw, axis=0, keepdims=True), g_last
        decay = jnp.exp2(g_prev - g_last)
        return decay * cmat + upd, decay * n_row + jnp.sum(kw, axis=0, keepdims=True), g_last

    pending = [scores(c) for c in range(min(ML_LOOKAHEAD, nc))]
    state = (None, None, None)
    for c in range(nc):
        if c + ML_LOOKAHEAD < nc:
            pending.append(scores(c + ML_LOOKAHEAD))
        state = finish(c, pending.pop(0), state)


def _mlstm(z, conv_w, conv_b, colpack, rowpack4, ml_norm_g, B, S):
    nc = S // ML_L
    return pl.pallas_call(
        _mlstm_kernel,
        grid=(B, ML_HEADS),
        in_specs=[
            pl.BlockSpec((S, LANES), lambda b, h: (b, ZB_MQ + h)),
            pl.BlockSpec((S, LANES), lambda b, h: (b, ZB_MK + h)),
            pl.BlockSpec((S, ML_V_DIM), lambda b, h: (b, ZB_MV // 2 + h)),
            pl.BlockSpec((S, ML_V_DIM), lambda b, h: (b, ZB_MO // 2 + h)),
            pl.BlockSpec((S, ML_V_DIM), lambda b, h: (b, ZB_MZ // 2 + h)),
            pl.BlockSpec((CONV_WIDTH, LANES), lambda b, h: (0, h)),
            pl.BlockSpec((CONV_WIDTH, LANES), lambda b, h: (0, ML_HEADS + h)),
            pl.BlockSpec((1, LANES), lambda b, h: (0, h)),
            pl.BlockSpec((1, LANES), lambda b, h: (0, ML_HEADS + h)),
            pl.BlockSpec((S, LANES), lambda b, h: (b, 0)),
            pl.BlockSpec((None, None, nc, ML_L), lambda b, h: (b, RS_BETA + h, 0, 0)),
            pl.BlockSpec((1, ML_V_DIM), lambda b, h: (0, h)),
        ],
        out_specs=pl.BlockSpec((S, ML_V_DIM), lambda b, h: (b, h)),
        out_shape=jax.ShapeDtypeStruct((B * S, ML_WIDTH), BF16),
        scratch_shapes=[
            pltpu.VMEM((S, ML_QK_DIM), BF16),
            pltpu.VMEM((S, ML_QK_DIM), F32),
        ],
        compiler_params=pltpu.CompilerParams(
            dimension_semantics=("arbitrary", "arbitrary"), vmem_limit_bytes=VMEM_LIMIT),
        name="mlstm",
    )(z, z, z, z, z, conv_w, conv_w, conv_b, conv_b, colpack, rowpack4, ml_norm_g)


def _out_kernel(oa_ref, hb_ref, ga_ref, gb_ref, x_ref, wfd_ref, wmd_ref, wo_ref, bga_ref, bgb_ref,
                fg_ref, o_ref):
    ya = _dot(oa_ref[...], wfd_ref[...])
    yb = _dot(hb_ref[...], wmd_ref[...])
    gate_a = jax.nn.sigmoid(ga_ref[...].astype(F32) + bga_ref[...])
    gate_b = jax.nn.sigmoid(gb_ref[...].astype(F32) + bgb_ref[...])
    y = gate_a * ya + gate_b * yb
    r = x_ref[...] + _dot(y.astype(BF16), wo_ref[...])
    ms = jnp.mean(r * r, axis=-1, keepdims=True)
    o_ref[...] = r * lax.rsqrt(ms + EPS) * fg_ref[...]


def _out_stage(oa, hb, z, x2, wfd, wmd, wo, b_gate, final_g):
    T = x2.shape[0]
    row = lambda i: (i, 0)
    const = lambda i: (0, 0)
    return pl.pallas_call(
        _out_kernel,
        grid=(T // TM_OUT,),
        in_specs=[
            pl.BlockSpec((TM_OUT, FOX_WIDTH), row),
            pl.BlockSpec((TM_OUT, ML_WIDTH), row),
            pl.BlockSpec((TM_OUT, D_MODEL), lambda i: (i, ZB_GA // 8)),
            pl.BlockSpec((TM_OUT, D_MODEL), lambda i: (i, ZB_GB // 8)),
            pl.BlockSpec((TM_OUT, D_MODEL), row),
            pl.BlockSpec((FOX_WIDTH, D_MODEL), const),
            pl.BlockSpec((ML_WIDTH, D_MODEL), const),
            pl.BlockSpec((D_MODEL, D_MODEL), const),
            pl.BlockSpec((1, D_MODEL), lambda i: (0, 0)),
            pl.BlockSpec((1, D_MODEL), lambda i: (0, 1)),
            pl.BlockSpec((1, D_MODEL), const),
        ],
        out_specs=pl.BlockSpec((TM_OUT, D_MODEL), row),
        out_shape=jax.ShapeDtypeStruct((T, D_MODEL), F32),
        compiler_params=pltpu.CompilerParams(
            dimension_semantics=("arbitrary",), vmem_limit_bytes=VMEM_LIMIT),
        name="outstage",
    )(oa, hb, z, z, x2, wfd, wmd, wo, b_gate, b_gate, final_g)


def _layer(x2, B, S, norm_g, w_in, b_fox_f, conv_w, conv_b, b_ml_i, b_ml_f, ml_norm_g, b_gate,
           w_fox_down, w_ml_down, w_out, out_g):
    o_ff = 3 * FOX_WIDTH
    o_fz = o_ff + FOX_HEADS
    o_mi = o_fz + FOX_WIDTH + 2 * ML_QK_WIDTH + ML_WIDTH
    o_mf = o_mi + ML_HEADS
    o_mo = o_mf + ML_HEADS
    w_big = jnp.concatenate([w_in[:, FOX_WIDTH:2 * FOX_WIDTH], w_in[:, o_fz:o_mi], w_in[:, o_mo:]],
                            axis=1).astype(BF16)
    zpad = lambda n: jnp.zeros((D_MODEL, n), F32)
    w_gate = jnp.concatenate(
        [w_in[:, o_ff:o_fz], w_in[:, o_mi:o_mf], zpad(8 - ML_HEADS), w_in[:, o_mf:o_mo],
         zpad(GATE_WIDTH - GL_LF - ML_HEADS)], axis=1)
    w_fm = jnp.transpose(jnp.concatenate(
        [w_in[:, :FOX_WIDTH], w_in[:, 2 * FOX_WIDTH:o_ff], w_gate], axis=1)).astype(BF16)
    zrow = lambda n: jnp.zeros((n,), F32)
    bias_col = jnp.concatenate(
        [b_fox_f, b_ml_i, zrow(8 - ML_HEADS), b_ml_f, zrow(GATE_WIDTH - GL_LF - ML_HEADS)]).reshape(GATE_WIDTH, 1)

    z, qt, vt, zg = _inproj(x2, norm_g.reshape(1, D_MODEL), w_big, w_fm)
    rowpack, colpack, aqt, ak = _gates(zg, bias_col, B, S)
    rowpack4 = rowpack.reshape(B, 8, S // ML_L, ML_L)
    oa = _fox(qt, z, vt, aqt, ak, B, S)
    hb = _mlstm(z, conv_w, conv_b.reshape(1, 2 * ML_QK_WIDTH), colpack, rowpack4,
                ml_norm_g.reshape(1, ML_WIDTH), B, S)
    return _out_stage(oa, hb, z, x2, w_fox_down.astype(BF16), w_ml_down.astype(BF16),
                      w_out.astype(BF16), b_gate.reshape(1, 2 * D_MODEL), out_g.reshape(1, D_MODEL))


def kernel(x, norm_g, w_in, b_fox_f, conv_w, conv_b, b_ml_i, b_ml_f, ml_norm_g, b_gate,
           w_fox_down, w_ml_down, w_out, final_g):
    assert TQ == ML_L
    B, S, _ = x.shape
    depth = norm_g.shape[0]
    assert depth == 1, "the final RMSNorm is fused into the (single) layer's output stage"
    out = _layer(x.reshape(B * S, D_MODEL), B, S, norm_g[0], w_in[0], b_fox_f[0], conv_w[0], conv_b[0],
                 b_ml_i[0], b_ml_f[0], ml_norm_g[0], b_gate[0], w_fox_down[0], w_ml_down[0],
                 w_out[0], final_g)
    return out.reshape(B, S, D_MODEL)
```

```python
import jax
import jax.numpy as jnp
from jax import lax
from jax.experimental import pallas as pl
from jax.experimental.pallas import tpu as pltpu

F32 = jnp.float32
BF16 = jnp.bfloat16

D_MODEL = 1024
FOX_HEADS = 8
FOX_HEAD_DIM = 128
FOX_WIDTH = FOX_HEADS * FOX_HEAD_DIM
ML_HEADS = 4
ML_QK_DIM = 128
ML_V_DIM = 256
ML_QK_WIDTH = ML_HEADS * ML_QK_DIM
ML_WIDTH = ML_HEADS * ML_V_DIM
CONV_WIDTH = 4
EPS = 1e-6

LANES = 128
SUBLANES = 8
BF16_ROWS = 16
GATE_WIDTH = LANES
VMEM_LIMIT = 56 * 1024 * 1024
LOG2E = 1.4426950408889634
FOX_Q_SCALE = FOX_HEAD_DIM ** -0.5 * LOG2E

FR_FQ = 0
FR_FV = FR_FQ + FOX_WIDTH
FR_MQ = FR_FV + FOX_WIDTH
FR_MV = FR_MQ + ML_QK_WIDTH
FR_MO = FR_MV + ML_WIDTH
FR_MZ = FR_MO + ML_WIDTH
FR_GATE = FR_MZ + ML_WIDTH
FM_ROWS = FR_GATE + GATE_WIDTH
FM_TILE = 1920
ZB_FK, ZB_FZ, ZB_GA, ZB_GB, ZB_MK = 0, 8, 16, 24, 32
Z_WIDTH = 4608
TN_IN = 2304
TM_IN = 1024

GL_FOX, GL_IG, GL_LF = 0, 8, 16
RS_BETA, RS_G, RS_M = 0, 8, 16
CL_BETA = 0
AUG_LANES = 6 * FOX_HEADS

TQ = 256
FOX_LOOKAHEAD = 4
ML_L = 256
ML_LOOKAHEAD = 2
TM_OUT = 512

NT_DIMS = (((1,), (1,)), ((), ()))


def _dot(a, b):
    return jnp.dot(a, b, preferred_element_type=F32)


N_FM_STEPS = FM_ROWS // FM_TILE
N_TM_STEPS = Z_WIDTH // TN_IN


def _inproj_kernel(x_ref, g_ref, wfm_ref, rs_ref, wtm_ref, fm_ref, zg_ref, z_ref, xh_ref):
    j = pl.program_id(1)

    def feature_major(xh):
        res = lax.dot_general(wfm_ref[...], xh, NT_DIMS, preferred_element_type=F32)
        fm_ref[...] = (res * rs_ref[...]).astype(BF16)
        return res

    @pl.when(j == 0)
    def _():
        x = x_ref[...]
        ms = jnp.mean(x * x, axis=-1, keepdims=True)
        xh = (x * lax.rsqrt(ms + EPS) * g_ref[...]).astype(BF16)
        xh_ref[...] = xh
        feature_major(xh)

    @pl.when((j > 0) & (j < N_FM_STEPS - 1))
    def _():
        feature_major(xh_ref[...])

    @pl.when(j == N_FM_STEPS - 1)
    def _():
        zg_ref[...] = feature_major(xh_ref[...])[FM_TILE - GATE_WIDTH:, :]

    @pl.when(j >= N_FM_STEPS)
    def _():
        z_ref[...] = _dot(xh_ref[...], wtm_ref[...]).astype(BF16)


def _inproj(x2, norm_g, w_fm, row_scale, w_tm):
    T = x2.shape[0]
    last_fm = N_FM_STEPS - 1
    return pl.pallas_call(
        _inproj_kernel,
        grid=(T // TM_IN, N_FM_STEPS + N_TM_STEPS),
        in_specs=[
            pl.BlockSpec((TM_IN, D_MODEL), lambda i, j: (i, 0)),
            pl.BlockSpec((1, D_MODEL), lambda i, j: (0, 0)),
            pl.BlockSpec((FM_TILE, D_MODEL), lambda i, j: (jnp.minimum(j, last_fm), 0)),
            pl.BlockSpec((FM_TILE, 1), lambda i, j: (jnp.minimum(j, last_fm), 0)),
            pl.BlockSpec((D_MODEL, TN_IN), lambda i, j: (0, jnp.maximum(j - N_FM_STEPS, 0))),
        ],
        out_specs=[
            pl.BlockSpec((FM_TILE, TM_IN), lambda i, j: (jnp.minimum(j, last_fm), i)),
            pl.BlockSpec((GATE_WIDTH, TM_IN), lambda i, j: (0, i)),
            pl.BlockSpec((TM_IN, TN_IN), lambda i, j: (i, jnp.maximum(j - N_FM_STEPS, 0))),
        ],
        out_shape=[
            jax.ShapeDtypeStruct((FM_ROWS, T), BF16),
            jax.ShapeDtypeStruct((GATE_WIDTH, T), F32),
            jax.ShapeDtypeStruct((T, Z_WIDTH), BF16),
        ],
        scratch_shapes=[pltpu.VMEM((TM_IN, D_MODEL), BF16)],
        compiler_params=pltpu.CompilerParams(
            dimension_semantics=("arbitrary", "arbitrary"), vmem_limit_bytes=VMEM_LIMIT),
        name="inproj",
    )(x2, norm_g, w_fm, row_scale, w_tm)


def _scan_lanes(x, op, ident):
    n = x.shape[1]
    lane = lax.broadcasted_iota(jnp.int32, x.shape, 1)
    k = 1
    while k < n:
        shifted = pltpu.roll(x, k, 1)
        x = op(x, jnp.where(lane >= k, shifted, ident))
        k *= 2
    return x


def _log_sigmoid(x):
    return jnp.minimum(x, 0.0) - jnp.log1p(jnp.exp(-jnp.abs(x)))


def _split3(x):
    hi = x.astype(BF16).astype(F32)
    r = x - hi
    mid = r.astype(BF16).astype(F32)
    lo = (r - mid).astype(BF16).astype(F32)
    return hi, mid, lo


def _gate_kernel(zg_ref, bias_ref, row_ref, col_ref, aq_ref, ak_ref):
    t = zg_ref[...] + bias_ref[...]
    S = t.shape[1]
    c = _scan_lanes(_log_sigmoid(t[GL_FOX:GL_FOX + 8]), jnp.add, 0.0)
    ig = t[GL_IG:GL_IG + 8]
    fsum = _scan_lanes(_log_sigmoid(t[GL_LF:GL_LF + 8]), jnp.add, 0.0)
    beta = (ig - fsum) * LOG2E
    gmax = jnp.maximum(_scan_lanes(beta, jnp.maximum, -jnp.inf), 0.0)
    m = fsum * LOG2E + gmax
    row_ref[RS_BETA:RS_BETA + 8, :] = beta
    row_ref[RS_G:RS_G + 8, :] = gmax
    row_ref[RS_M:RS_M + 8, :] = m
    col_ref[...] = jnp.transpose(jnp.concatenate([beta, jnp.zeros((LANES - 8, S), F32)], axis=0))
    hi, mid, lo = _split3(c * LOG2E)
    ones = jnp.ones((8, S), F32)
    pad = jnp.zeros((LANES - AUG_LANES, S), F32)
    aq_ref[...] = jnp.concatenate([hi, mid, lo, ones, ones, ones, pad], axis=0).astype(BF16)
    ak_ref[...] = jnp.transpose(jnp.concatenate([ones, ones, ones, -hi, -mid, -lo, pad], axis=0)).astype(BF16)


def _gates(zg, bias_col, B, S):
    return pl.pallas_call(
        _gate_kernel,
        grid=(B,),
        in_specs=[
            pl.BlockSpec((GATE_WIDTH, S), lambda b: (0, b)),
            pl.BlockSpec((GATE_WIDTH, 1), lambda b: (0, 0)),
        ],
        out_specs=[
            pl.BlockSpec((None, 24, S), lambda b: (b, 0, 0)),
            pl.BlockSpec((S, LANES), lambda b: (b, 0)),
            pl.BlockSpec((None, LANES, S), lambda b: (b, 0, 0)),
            pl.BlockSpec((S, LANES), lambda b: (b, 0)),
        ],
        out_shape=[
            jax.ShapeDtypeStruct((B, 24, S), F32),
            jax.ShapeDtypeStruct((B * S, LANES), F32),
            jax.ShapeDtypeStruct((B, LANES, S), BF16),
            jax.ShapeDtypeStruct((B * S, LANES), BF16),
        ],
        compiler_params=pltpu.CompilerParams(
            dimension_semantics=("arbitrary",), vmem_limit_bytes=VMEM_LIMIT),
        name="gates",
    )(zg, bias_col)


def _silu(x):
    return x * (1.0 / (1.0 + jnp.exp2(x * -LOG2E)))


def _fox_kernel(qt_ref, k_ref, vt_ref, fz_ref, aqt_ref, ak_ref, o_ref, qaug_ref, kaug_ref, vaug_ref):
    h = pl.program_id(1)
    S = k_ref.shape[0]
    d = FOX_HEAD_DIM
    lane = lax.broadcasted_iota(jnp.int32, (1, LANES), 1)
    keep = jnp.where(((lane & (FOX_HEADS - 1)) == h) & (lane < AUG_LANES), 1.0, 0.0).astype(BF16)
    qaug_ref[:d, :] = qt_ref[...]
    qaug_ref[d:, :] = aqt_ref[...]
    kaug_ref[:, :LANES] = k_ref[...]
    kaug_ref[:, LANES:] = ak_ref[...] * keep
    vaug_ref[:d, :] = vt_ref[...]
    vaug_ref[d:, :] = jnp.ones((BF16_ROWS, S), BF16)
    s_idx = lax.broadcasted_iota(jnp.int32, (TQ, TQ), 0)
    t_idx = lax.broadcasted_iota(jnp.int32, (TQ, TQ), 1)
    causal = s_idx <= t_idx

    ng = S // TQ
    m = [None] * ng
    acc = [None] * ng
    tiles = [(kj, g) for kj in range(ng) for g in range(kj, ng)]

    def scores(kj, g):
        ka = kaug_ref[kj * TQ:(kj + 1) * TQ, :]
        qa = qaug_ref[:, g * TQ:(g + 1) * TQ]
        return _dot(ka, qa)

    def finish(kj, g, st):
        va = vaug_ref[:, kj * TQ:(kj + 1) * TQ]
        if kj == g:
            st = jnp.where(causal, st, -jnp.inf)
        cm = jnp.max(st, axis=0, keepdims=True)
        if kj == 0:
            m[g] = cm
            acc[g] = _dot(va, jnp.exp2(st - cm).astype(BF16))
        else:
            m_new = jnp.maximum(m[g], cm)
            alpha = jnp.exp2(m[g] - m_new)
            acc[g] = alpha * acc[g] + _dot(va, jnp.exp2(st - m_new).astype(BF16))
            m[g] = m_new
        if kj == g:
            a = acc[g]
            o = jnp.transpose(a[:d] * (1.0 / a[d:d + 1]))
            fz = fz_ref[g * TQ:(g + 1) * TQ, :].astype(F32)
            o_ref[g * TQ:(g + 1) * TQ, :] = (o * _silu(fz)).astype(BF16)

    pending = [scores(*tiles[i]) for i in range(FOX_LOOKAHEAD)]
    for i, (kj, g) in enumerate(tiles):
        if i + FOX_LOOKAHEAD < len(tiles):
            pending.append(scores(*tiles[i + FOX_LOOKAHEAD]))
        finish(kj, g, pending.pop(0))


def _fox(fm, z, aqt, ak, B, S):
    d = FOX_HEAD_DIM
    return pl.pallas_call(
        _fox_kernel,
        grid=(B, FOX_HEADS),
        in_specs=[
            pl.BlockSpec((d, S), lambda b, h: (FR_FQ // d + h, b)),
            pl.BlockSpec((S, LANES), lambda b, h: (b, ZB_FK + h)),
            pl.BlockSpec((d, S), lambda b, h: (FR_FV // d + h, b)),
            pl.BlockSpec((S, LANES), lambda b, h: (b, ZB_FZ + h)),
            pl.BlockSpec((None, LANES, S), lambda b, h: (b, 0, 0)),
            pl.BlockSpec((S, LANES), lambda b, h: (b, 0)),
        ],
        out_specs=pl.BlockSpec((S, LANES), lambda b, h: (b, h)),
        out_shape=jax.ShapeDtypeStruct((B * S, FOX_WIDTH), BF16),
        scratch_shapes=[
            pltpu.VMEM((2 * LANES, S), BF16),
            pltpu.VMEM((S, 2 * LANES), BF16),
            pltpu.VMEM((d + BF16_ROWS, S), BF16),
        ],
        compiler_params=pltpu.CompilerParams(
            dimension_semantics=("arbitrary", "arbitrary"), vmem_limit_bytes=VMEM_LIMIT),
        name="fox",
    )(fm, z, fm, z, aqt, ak)


def _conv_taps(x, taps, bias, axis, edge):
    def run(xx, mask_idx):
        y = xx * taps[CONV_WIDTH - 1] + bias
        for r in range(1, CONV_WIDTH):
            xs = pltpu.roll(xx, r, axis)
            if mask_idx is not None:
                xs = jnp.where(mask_idx >= r, xs, 0.0)
            y = y + xs * taps[CONV_WIDTH - 1 - r]
        return y

    if axis == 0:
        head = x[:edge, :]
        body = run(x, None)[edge:, :]
    else:
        head = x[:, :edge]
        body = run(x, None)[:, edge:]
    head = run(head, lax.broadcasted_iota(jnp.int32, head.shape, axis))
    return jnp.concatenate([head, body], axis=axis)


def _mlstm_kernel(qt_ref, k_ref, vt_ref, mot_ref, mzt_ref, cwq_ref, cwk_ref, cbk_ref,
                  col_ref, brow_ref, grow_ref, mrow_ref, ng_ref, o_ref, qs_ref, kf_ref, vaug_ref):
    h = pl.program_id(1)
    S = k_ref.shape[0]
    L = ML_L
    nc = S // L
    dv = ML_V_DIM
    head_shift = (LANES - h) % LANES
    cwq = cwq_ref[...]
    q_taps = [cwq[:, j:j + 1] for j in range(CONV_WIDTH)]
    yq = _conv_taps(qt_ref[...].astype(F32), q_taps, cwq[:, CONV_WIDTH:CONV_WIDTH + 1], 1, LANES)
    qs_ref[...] = (_silu(yq) * (ML_QK_DIM ** -0.5)).astype(BF16)
    cwk = cwk_ref[...]
    k_taps = [cwk[j:j + 1, :] for j in range(CONV_WIDTH)]
    kf_ref[...] = _silu(_conv_taps(k_ref[...].astype(F32), k_taps, cbk_ref[...], 0, SUBLANES))
    vaug_ref[:dv, :] = vt_ref[...]
    vaug_ref[dv:, :] = jnp.ones((BF16_ROWS, S), BF16)
    s_idx = lax.broadcasted_iota(jnp.int32, (L, L), 0)
    t_idx = lax.broadcasted_iota(jnp.int32, (L, L), 1)
    causal = s_idx <= t_idx
    ng = ng_ref[...]

    def scores(c):
        kb = kf_ref[c * L:(c + 1) * L, :].astype(BF16)
        return _dot(kb, qs_ref[:, c * L:(c + 1) * L])

    def finish(c, st, state):
        ct_aug, g_prev = state
        cols = slice(c * L, (c + 1) * L)
        b_col = pltpu.roll(col_ref[cols, :], head_shift, 1)[:, CL_BETA:CL_BETA + 1]
        g_row = grow_ref[c:c + 1, :]
        g_last = g_row[:, L - 1:L]
        va = vaug_ref[:, cols]
        kw = kf_ref[cols, :] * jnp.exp2(b_col - g_last)
        upd = _dot(va, kw.astype(BF16))
        sw = st * jnp.where(causal, jnp.exp2(b_col - g_row), 0.0)
        num = _dot(va, sw.astype(BF16))
        if c > 0:
            w_inter = jnp.exp2(g_prev - g_row)
            num = num + w_inter * _dot(ct_aug.astype(BF16), qs_ref[:, cols])
        den = num[dv:dv + 1, :]
        num = num[:dv, :]
        inv = 1.0 / jnp.maximum(jnp.abs(den), jnp.exp2(-mrow_ref[c:c + 1, :]))
        ssq = jnp.sum(num * num, axis=0, keepdims=True)
        t_scale = inv * lax.rsqrt(inv * inv * ssq * (1.0 / dv) + EPS)
        mo = mot_ref[:, cols].astype(F32)
        mz = mzt_ref[:, cols].astype(F32)
        gate = mz * (1.0 / ((1.0 + jnp.exp2(mo * -LOG2E)) * (1.0 + jnp.exp2(mz * -LOG2E))))
        o_ref[cols, :] = jnp.transpose(num * t_scale * (ng * gate)).astype(BF16)
        if c == 0:
            return upd, g_last
        return jnp.exp2(g_prev - g_last) * ct_aug + upd, g_last

    pending = [scores(c) for c in range(min(ML_LOOKAHEAD, nc))]
    state = (None, None)
    for c in range(nc):
        if c + ML_LOOKAHEAD < nc:
            pending.append(scores(c + ML_LOOKAHEAD))
        state = finish(c, pending.pop(0), state)


def _mlstm(fm, z, cwq_t, conv_w, conv_b, colpack, rowpack4, ng_col, B, S):
    nc = S // ML_L
    dk, dv = ML_QK_DIM, ML_V_DIM
    row_spec = lambda slab: pl.BlockSpec((None, None, nc, ML_L), lambda b, h: (b, slab + h, 0, 0))
    return pl.pallas_call(
        _mlstm_kernel,
        grid=(B, ML_HEADS),
        in_specs=[
            pl.BlockSpec((dk, S), lambda b, h: (FR_MQ // dk + h, b)),
            pl.BlockSpec((S, LANES), lambda b, h: (b, ZB_MK + h)),
            pl.BlockSpec((dv, S), lambda b, h: (FR_MV // dv + h, b)),
            pl.BlockSpec((dv, S), lambda b, h: (FR_MO // dv + h, b)),
            pl.BlockSpec((dv, S), lambda b, h: (FR_MZ // dv + h, b)),
            pl.BlockSpec((dk, LANES), lambda b, h: (h, 0)),
            pl.BlockSpec((CONV_WIDTH, LANES), lambda b, h: (0, ML_HEADS + h)),
            pl.BlockSpec((1, LANES), lambda b, h: (0, ML_HEADS + h)),
            pl.BlockSpec((S, LANES), lambda b, h: (b, 0)),
            row_spec(RS_BETA), row_spec(RS_G), row_spec(RS_M),
            pl.BlockSpec((dv, 1), lambda b, h: (h, 0)),
        ],
        out_specs=pl.BlockSpec((S, dv), lambda b, h: (b, h)),
        out_shape=jax.ShapeDtypeStruct((B * S, ML_WIDTH), BF16),
        scratch_shapes=[
            pltpu.VMEM((dk, S), BF16),
            pltpu.VMEM((S, dk), F32),
            pltpu.VMEM((dv + BF16_ROWS, S), BF16),
        ],
        compiler_params=pltpu.CompilerParams(
            dimension_semantics=("arbitrary", "arbitrary"), vmem_limit_bytes=VMEM_LIMIT),
        name="mlstm",
    )(fm, z, fm, fm, fm, cwq_t, conv_w, conv_b, colpack, rowpack4, rowpack4, rowpack4, ng_col)


def _out_kernel(oa_ref, hb_ref, ga_ref, gb_ref, x_ref, wfd_ref, wmd_ref, wo_ref, bga_ref, bgb_ref,
                fg_ref, o_ref):
    ya = _dot(oa_ref[...], wfd_ref[...])
    yb = _dot(hb_ref[...], wmd_ref[...])
    gate_a = jax.nn.sigmoid(ga_ref[...].astype(F32) + bga_ref[...])
    gate_b = jax.nn.sigmoid(gb_ref[...].astype(F32) + bgb_ref[...])
    y = gate_a * ya + gate_b * yb
    r = x_ref[...] + _dot(y.astype(BF16), wo_ref[...])
    ms = jnp.mean(r * r, axis=-1, keepdims=True)
    o_ref[...] = r * lax.rsqrt(ms + EPS) * fg_ref[...]


def _out_stage(oa, hb, z, x2, wfd, wmd, wo, b_gate, final_g):
    T = x2.shape[0]
    row = lambda i: (i, 0)
    const = lambda i: (0, 0)
    return pl.pallas_call(
        _out_kernel,
        grid=(T // TM_OUT,),
        in_specs=[
            pl.BlockSpec((TM_OUT, FOX_WIDTH), row),
            pl.BlockSpec((TM_OUT, ML_WIDTH), row),
            pl.BlockSpec((TM_OUT, D_MODEL), lambda i: (i, ZB_GA // 8)),
            pl.BlockSpec((TM_OUT, D_MODEL), lambda i: (i, ZB_GB // 8)),
            pl.BlockSpec((TM_OUT, D_MODEL), row),
            pl.BlockSpec((FOX_WIDTH, D_MODEL), const),
            pl.BlockSpec((ML_WIDTH, D_MODEL), const),
            pl.BlockSpec((D_MODEL, D_MODEL), const),
            pl.BlockSpec((1, D_MODEL), lambda i: (0, 0)),
            pl.BlockSpec((1, D_MODEL), lambda i: (0, 1)),
            pl.BlockSpec((1, D_MODEL), const),
        ],
        out_specs=pl.BlockSpec((TM_OUT, D_MODEL), row),
        out_shape=jax.ShapeDtypeStruct((T, D_MODEL), F32),
        compiler_params=pltpu.CompilerParams(
            dimension_semantics=("arbitrary",), vmem_limit_bytes=VMEM_LIMIT),
        name="outstage",
    )(oa, hb, z, z, x2, wfd, wmd, wo, b_gate, b_gate, final_g)


def _layer(x2, B, S, norm_g, w_in, b_fox_f, conv_w, conv_b, b_ml_i, b_ml_f, ml_norm_g, b_gate,
           w_fox_down, w_ml_down, w_out, out_g):
    o_fq, o_fk, o_fv = 0, FOX_WIDTH, 2 * FOX_WIDTH
    o_ff = 3 * FOX_WIDTH
    o_fz = o_ff + FOX_HEADS
    o_mq = o_fz + FOX_WIDTH
    o_mk = o_mq + ML_QK_WIDTH
    o_mv = o_mk + ML_QK_WIDTH
    o_mi = o_mv + ML_WIDTH
    o_mf = o_mi + ML_HEADS
    o_mo = o_mf + ML_HEADS
    o_mz = o_mo + ML_WIDTH
    o_ga = o_mz + ML_WIDTH
    o_gb = o_ga + D_MODEL
    cols = lambda a, n: w_in[:, a:a + n]
    zpad = lambda n: jnp.zeros((D_MODEL, n), F32)
    w_fm = jnp.transpose(jnp.concatenate(
        [cols(o_fq, FOX_WIDTH), cols(o_fv, FOX_WIDTH), cols(o_mq, ML_QK_WIDTH), cols(o_mv, ML_WIDTH),
         cols(o_mo, ML_WIDTH), cols(o_mz, ML_WIDTH),
         cols(o_ff, FOX_HEADS), cols(o_mi, ML_HEADS), zpad(8 - ML_HEADS), cols(o_mf, ML_HEADS),
         zpad(GATE_WIDTH - GL_LF - ML_HEADS)], axis=1)).astype(BF16)
    row_scale = jnp.concatenate([jnp.full((FOX_WIDTH, 1), FOX_Q_SCALE, F32),
                                 jnp.ones((FM_ROWS - FOX_WIDTH, 1), F32)], axis=0)
    w_tm = jnp.concatenate([cols(o_fk, FOX_WIDTH), cols(o_fz, FOX_WIDTH), cols(o_ga, D_MODEL),
                            cols(o_gb, D_MODEL), cols(o_mk, ML_QK_WIDTH)], axis=1).astype(BF16)
    zrow = lambda n: jnp.zeros((n,), F32)
    bias_col = jnp.concatenate(
        [b_fox_f, b_ml_i, zrow(8 - ML_HEADS), b_ml_f, zrow(GATE_WIDTH - GL_LF - ML_HEADS)]).reshape(GATE_WIDTH, 1)
    cwq_t = jnp.concatenate(
        [jnp.transpose(conv_w[:, :ML_QK_WIDTH]), conv_b[:ML_QK_WIDTH, None],
         jnp.zeros((ML_QK_WIDTH, LANES - CONV_WIDTH - 1), F32)], axis=1)

    fm, zg, z = _inproj(x2, norm_g.reshape(1, D_MODEL), w_fm, row_scale, w_tm)
    rowpack, colpack, aqt, ak = _gates(zg, bias_col, B, S)
    rowpack4 = rowpack.reshape(B, 24, S // ML_L, ML_L)
    oa = _fox(fm, z, aqt, ak, B, S)
    hb = _mlstm(fm, z, cwq_t, conv_w, conv_b.reshape(1, 2 * ML_QK_WIDTH), colpack, rowpack4,
                ml_norm_g.reshape(ML_WIDTH, 1), B, S)
    return _out_stage(oa, hb, z, x2, w_fox_down.astype(BF16), w_ml_down.astype(BF16),
                      w_out.astype(BF16), b_gate.reshape(1, 2 * D_MODEL), out_g.reshape(1, D_MODEL))


def kernel(x, norm_g, w_in, b_fox_f, conv_w, conv_b, b_ml_i, b_ml_f, ml_norm_g, b_gate,
           w_fox_down, w_ml_down, w_out, final_g):
    B, S, _ = x.shape
    depth = norm_g.shape[0]
    assert depth == 1, "the final RMSNorm is fused into the (single) layer's output stage"
    out = _layer(x.reshape(B * S, D_MODEL), B, S, norm_g[0], w_in[0], b_fox_f[0], conv_w[0], conv_b[0],
                 b_ml_i[0], b_ml_f[0], ml_norm_g[0], b_gate[0], w_fox_down[0], w_ml_down[0],
                 w_out[0], final_g)
    return out.reshape(B, S, D_MODEL)
```

```python
import jax
import jax.numpy as jnp
from jax import lax
from jax.experimental import pallas as pl
from jax.experimental.pallas import tpu as pltpu

F32 = jnp.float32
BF16 = jnp.bfloat16

D_MODEL = 1024
FOX_HEADS = 8
FOX_HEAD_DIM = 128
FOX_WIDTH = FOX_HEADS * FOX_HEAD_DIM
ML_HEADS = 4
ML_QK_DIM = 128
ML_V_DIM = 256
ML_QK_WIDTH = ML_HEADS * ML_QK_DIM
ML_WIDTH = ML_HEADS * ML_V_DIM
CONV_WIDTH = 4
EPS = 1e-6

LANES = 128
SUBLANES = 8
BF16_ROWS = 16
GATE_WIDTH = LANES
VMEM_LIMIT = 56 * 1024 * 1024
LOG2E = 1.4426950408889634
FOX_Q_SCALE = FOX_HEAD_DIM ** -0.5 * LOG2E

FR_FQ = 0
FR_FV = FR_FQ + FOX_WIDTH
FR_MQ = FR_FV + FOX_WIDTH
FR_MV = FR_MQ + ML_QK_WIDTH
FR_MO = FR_MV + ML_WIDTH
FR_MZ = FR_MO + ML_WIDTH
FR_GATE = FR_MZ + ML_WIDTH
FM_ROWS = FR_GATE + GATE_WIDTH
FM_TILE = 1152
ZB_FK, ZB_FZ, ZB_GA, ZB_GB, ZB_MK = 0, 8, 16, 24, 32
Z_WIDTH = 4608
TN_IN = 1536
TM_IN = 512

GL_FOX, GL_IG, GL_LF = 0, 8, 16
RS_BETA, RS_G, RS_M = 0, 8, 16
CL_BETA = 0
AUG_LANES = 6 * FOX_HEADS

TQ = 256
FOX_LOOKAHEAD = 6
ML_L = 256
ML_LOOKAHEAD = 2
TM_OUT = 512

NT_DIMS = (((1,), (1,)), ((), ()))


def _dot(a, b):
    return jnp.dot(a, b, preferred_element_type=F32)


def _inproj_kernel(x_ref, g_ref, wfm_ref, wtm_ref, fm_ref, zg_ref, z_ref):
    x = x_ref[...]
    ms = jnp.mean(x * x, axis=-1, keepdims=True)
    xh = (x * lax.rsqrt(ms + EPS) * g_ref[...]).astype(BF16)
    for r0 in range(0, FM_ROWS, FM_TILE):
        res = lax.dot_general(wfm_ref[r0:r0 + FM_TILE, :], xh, NT_DIMS, preferred_element_type=F32)
        r1 = min(r0 + FM_TILE, FR_GATE)
        q1 = max(r0, min(r1, FR_FV))
        if r0 < q1:
            fm_ref[r0:q1, :] = (res[:q1 - r0, :] * FOX_Q_SCALE).astype(BF16)
        if q1 < r1:
            fm_ref[q1:r1, :] = res[q1 - r0:r1 - r0, :].astype(BF16)
        if r0 + FM_TILE > FR_GATE:
            zg_ref[...] = res[FR_GATE - r0:, :]
    for c0 in range(0, Z_WIDTH, TN_IN):
        z_ref[:, c0:c0 + TN_IN] = _dot(xh, wtm_ref[:, c0:c0 + TN_IN]).astype(BF16)


def _inproj(x2, norm_g, w_fm, w_tm):
    T = x2.shape[0]
    resident = pl.Buffered(1)
    return pl.pallas_call(
        _inproj_kernel,
        grid=(T // TM_IN,),
        in_specs=[
            pl.BlockSpec((TM_IN, D_MODEL), lambda i: (i, 0)),
            pl.BlockSpec((1, D_MODEL), lambda i: (0, 0)),
            pl.BlockSpec((FM_ROWS, D_MODEL), lambda i: (0, 0), pipeline_mode=resident),
            pl.BlockSpec((D_MODEL, Z_WIDTH), lambda i: (0, 0), pipeline_mode=resident),
        ],
        out_specs=[
            pl.BlockSpec((FR_GATE, TM_IN), lambda i: (0, i)),
            pl.BlockSpec((GATE_WIDTH, TM_IN), lambda i: (0, i)),
            pl.BlockSpec((TM_IN, Z_WIDTH), lambda i: (i, 0)),
        ],
        out_shape=[
            jax.ShapeDtypeStruct((FR_GATE, T), BF16),
            jax.ShapeDtypeStruct((GATE_WIDTH, T), F32),
            jax.ShapeDtypeStruct((T, Z_WIDTH), BF16),
        ],
        compiler_params=pltpu.CompilerParams(
            dimension_semantics=("arbitrary",), vmem_limit_bytes=VMEM_LIMIT),
        name="inproj",
    )(x2, norm_g, w_fm, w_tm)


def _scan_lanes(x, op, ident):
    n = x.shape[1]
    lane = lax.broadcasted_iota(jnp.int32, x.shape, 1)
    k = 1
    while k < n:
        shifted = pltpu.roll(x, k, 1)
        x = op(x, jnp.where(lane >= k, shifted, ident))
        k *= 2
    return x


def _log_sigmoid(x):
    return jnp.minimum(x, 0.0) - jnp.log1p(jnp.exp(-jnp.abs(x)))


def _split3(x):
    hi = x.astype(BF16).astype(F32)
    r = x - hi
    mid = r.astype(BF16).astype(F32)
    lo = (r - mid).astype(BF16).astype(F32)
    return hi, mid, lo


def _gate_kernel(zg_ref, bias_ref, row_ref, col_ref, aq_ref, ak_ref):
    t = zg_ref[...] + bias_ref[...]
    S = t.shape[1]
    c = _scan_lanes(_log_sigmoid(t[GL_FOX:GL_FOX + 8]), jnp.add, 0.0)
    ig = t[GL_IG:GL_IG + 8]
    fsum = _scan_lanes(_log_sigmoid(t[GL_LF:GL_LF + 8]), jnp.add, 0.0)
    beta = (ig - fsum) * LOG2E
    gmax = jnp.maximum(_scan_lanes(beta, jnp.maximum, -jnp.inf), 0.0)
    m = fsum * LOG2E + gmax
    row_ref[RS_BETA:RS_BETA + 8, :] = beta
    row_ref[RS_G:RS_G + 8, :] = gmax
    row_ref[RS_M:RS_M + 8, :] = m
    col_ref[...] = jnp.transpose(jnp.concatenate([beta, jnp.zeros((LANES - 8, S), F32)], axis=0))
    hi, mid, lo = _split3(c * LOG2E)
    ones = jnp.ones((8, S), F32)
    pad = jnp.zeros((LANES - AUG_LANES, S), F32)
    aq_ref[...] = jnp.concatenate([hi, mid, lo, ones, ones, ones, pad], axis=0).astype(BF16)
    ak_ref[...] = jnp.transpose(jnp.concatenate([ones, ones, ones, -hi, -mid, -lo, pad], axis=0)).astype(BF16)


def _gates(zg, bias_col, B, S):
    return pl.pallas_call(
        _gate_kernel,
        grid=(B,),
        in_specs=[
            pl.BlockSpec((GATE_WIDTH, S), lambda b: (0, b)),
            pl.BlockSpec((GATE_WIDTH, 1), lambda b: (0, 0)),
        ],
        out_specs=[
            pl.BlockSpec((None, 24, S), lambda b: (b, 0, 0)),
            pl.BlockSpec((S, LANES), lambda b: (b, 0)),
            pl.BlockSpec((None, LANES, S), lambda b: (b, 0, 0)),
            pl.BlockSpec((S, LANES), lambda b: (b, 0)),
        ],
        out_shape=[
            jax.ShapeDtypeStruct((B, 24, S), F32),
            jax.ShapeDtypeStruct((B * S, LANES), F32),
            jax.ShapeDtypeStruct((B, LANES, S), BF16),
            jax.ShapeDtypeStruct((B * S, LANES), BF16),
        ],
        compiler_params=pltpu.CompilerParams(
            dimension_semantics=("arbitrary",), vmem_limit_bytes=VMEM_LIMIT),
        name="gates",
    )(zg, bias_col)


def _silu(x):
    return x * (1.0 / (1.0 + jnp.exp2(x * -LOG2E)))


def _fox_kernel(qt_ref, k_ref, vt_ref, fz_ref, aqt_ref, ak_ref, o_ref, qaug_ref, kaug_ref, vaug_ref):
    h = pl.program_id(1)
    S = k_ref.shape[0]
    d = FOX_HEAD_DIM
    lane = lax.broadcasted_iota(jnp.int32, (1, LANES), 1)
    keep = jnp.where(((lane & (FOX_HEADS - 1)) == h) & (lane < AUG_LANES), 1.0, 0.0).astype(BF16)
    qaug_ref[:d, :] = qt_ref[...]
    qaug_ref[d:, :] = aqt_ref[...]
    kaug_ref[:, :LANES] = k_ref[...]
    kaug_ref[:, LANES:] = ak_ref[...] * keep
    vaug_ref[:d, :] = vt_ref[...]
    vaug_ref[d:, :] = jnp.ones((BF16_ROWS, S), BF16)
    s_idx = lax.broadcasted_iota(jnp.int32, (TQ, TQ), 0)
    t_idx = lax.broadcasted_iota(jnp.int32, (TQ, TQ), 1)
    causal = s_idx <= t_idx

    ng = S // TQ
    m = [None] * ng
    acc = [None] * ng
    tiles = [(kj, g) for kj in range(ng) for g in range(kj, ng)]

    def scores(kj, g):
        ka = kaug_ref[kj * TQ:(kj + 1) * TQ, :]
        qa = qaug_ref[:, g * TQ:(g + 1) * TQ]
        return _dot(ka, qa)

    def finish(kj, g, st):
        va = vaug_ref[:, kj * TQ:(kj + 1) * TQ]
        if kj == g:
            st = jnp.where(causal, st, -jnp.inf)
        cm = jnp.max(st, axis=0, keepdims=True)
        if kj == 0:
            m[g] = cm
            acc[g] = _dot(va, jnp.exp2(st - cm).astype(BF16))
        else:
            m_new = jnp.maximum(m[g], cm)
            alpha = jnp.exp2(m[g] - m_new)
            acc[g] = alpha * acc[g] + _dot(va, jnp.exp2(st - m_new).astype(BF16))
            m[g] = m_new
        if kj == g:
            a = acc[g]
            o = jnp.transpose(a[:d] * (1.0 / a[d:d + 1]))
            fz = fz_ref[g * TQ:(g + 1) * TQ, :].astype(F32)
            o_ref[g * TQ:(g + 1) * TQ, :] = (o * _silu(fz)).astype(BF16)

    pending = [scores(*tiles[i]) for i in range(FOX_LOOKAHEAD)]
    for i, (kj, g) in enumerate(tiles):
        if i + FOX_LOOKAHEAD < len(tiles):
            pending.append(scores(*tiles[i + FOX_LOOKAHEAD]))
        finish(kj, g, pending.pop(0))


def _fox(fm, z, aqt, ak, B, S):
    d = FOX_HEAD_DIM
    return pl.pallas_call(
        _fox_kernel,
        grid=(B, FOX_HEADS),
        in_specs=[
            pl.BlockSpec((d, S), lambda b, h: (FR_FQ // d + h, b)),
            pl.BlockSpec((S, LANES), lambda b, h: (b, ZB_FK + h)),
            pl.BlockSpec((d, S), lambda b, h: (FR_FV // d + h, b)),
            pl.BlockSpec((S, LANES), lambda b, h: (b, ZB_FZ + h)),
            pl.BlockSpec((None, LANES, S), lambda b, h: (b, 0, 0)),
            pl.BlockSpec((S, LANES), lambda b, h: (b, 0)),
        ],
        out_specs=pl.BlockSpec((S, LANES), lambda b, h: (b, h)),
        out_shape=jax.ShapeDtypeStruct((B * S, FOX_WIDTH), BF16),
        scratch_shapes=[
            pltpu.VMEM((2 * LANES, S), BF16),
            pltpu.VMEM((S, 2 * LANES), BF16),
            pltpu.VMEM((d + BF16_ROWS, S), BF16),
        ],
        compiler_params=pltpu.CompilerParams(
            dimension_semantics=("arbitrary", "arbitrary"), vmem_limit_bytes=VMEM_LIMIT),
        name="fox",
    )(fm, z, fm, z, aqt, ak)


def _conv_taps(x, taps, bias, axis, edge):
    def run(xx, mask_idx):
        y = xx * taps[CONV_WIDTH - 1] + bias
        for r in range(1, CONV_WIDTH):
            xs = pltpu.roll(xx, r, axis)
            if mask_idx is not None:
                xs = jnp.where(mask_idx >= r, xs, 0.0)
            y = y + xs * taps[CONV_WIDTH - 1 - r]
        return y

    if axis == 0:
        head = x[:edge, :]
        body = run(x, None)[edge:, :]
    else:
        head = x[:, :edge]
        body = run(x, None)[:, edge:]
    head = run(head, lax.broadcasted_iota(jnp.int32, head.shape, axis))
    return jnp.concatenate([head, body], axis=axis)


def _mlstm_kernel(qt_ref, k_ref, vt_ref, mot_ref, mzt_ref, cwq_ref, cwk_ref, cbk_ref,
                  col_ref, brow_ref, grow_ref, mrow_ref, ng_ref, o_ref, qs_ref, kf_ref, vaug_ref):
    h = pl.program_id(1)
    S = k_ref.shape[0]
    L = ML_L
    nc = S // L
    dv = ML_V_DIM
    head_shift = (LANES - h) % LANES
    cwq = cwq_ref[...]
    q_taps = [cwq[:, j:j + 1] for j in range(CONV_WIDTH)]
    yq = _conv_taps(qt_ref[...].astype(F32), q_taps, cwq[:, CONV_WIDTH:CONV_WIDTH + 1], 1, LANES)
    qs_ref[...] = (_silu(yq) * (ML_QK_DIM ** -0.5)).astype(BF16)
    cwk = cwk_ref[...]
    k_taps = [cwk[j:j + 1, :] for j in range(CONV_WIDTH)]
    kf_ref[...] = _silu(_conv_taps(k_ref[...].astype(F32), k_taps, cbk_ref[...], 0, SUBLANES))
    vaug_ref[:dv, :] = vt_ref[...]
    vaug_ref[dv:, :] = jnp.ones((BF16_ROWS, S), BF16)
    s_idx = lax.broadcasted_iota(jnp.int32, (L, L), 0)
    t_idx = lax.broadcasted_iota(jnp.int32, (L, L), 1)
    causal = s_idx <= t_idx
    ng = ng_ref[...]

    def scores(c):
        kb = kf_ref[c * L:(c + 1) * L, :].astype(BF16)
        return _dot(kb, qs_ref[:, c * L:(c + 1) * L])

    def finish(c, st, state):
        ct_aug, g_prev = state
        cols = slice(c * L, (c + 1) * L)
        b_col = pltpu.roll(col_ref[cols, :], head_shift, 1)[:, CL_BETA:CL_BETA + 1]
        g_row = grow_ref[c:c + 1, :]
        g_last = g_row[:, L - 1:L]
        va = vaug_ref[:, cols]
        kw = kf_ref[cols, :] * jnp.exp2(b_col - g_last)
        upd = _dot(va, kw.astype(BF16))
        sw = st * jnp.where(causal, jnp.exp2(b_col - g_row), 0.0)
        num = _dot(va, sw.astype(BF16))
        if c > 0:
            w_inter = jnp.exp2(g_prev - g_row)
            num = num + w_inter * _dot(ct_aug.astype(BF16), qs_ref[:, cols])
        den = num[dv:dv + 1, :]
        num = num[:dv, :]
        inv = 1.0 / jnp.maximum(jnp.abs(den), jnp.exp2(-mrow_ref[c:c + 1, :]))
        ssq = jnp.sum(num * num, axis=0, keepdims=True)
        t_scale = inv * lax.rsqrt(inv * inv * ssq * (1.0 / dv) + EPS)
        mo = mot_ref[:, cols].astype(F32)
        mz = mzt_ref[:, cols].astype(F32)
        gate = mz * (1.0 / ((1.0 + jnp.exp2(mo * -LOG2E)) * (1.0 + jnp.exp2(mz * -LOG2E))))
        o_ref[cols, :] = jnp.transpose(num * t_scale * (ng * gate)).astype(BF16)
        if c == 0:
            return upd, g_last
        return jnp.exp2(g_prev - g_last) * ct_aug + upd, g_last

    pending = [scores(c) for c in range(min(ML_LOOKAHEAD, nc))]
    state = (None, None)
    for c in range(nc):
        if c + ML_LOOKAHEAD < nc:
            pending.append(scores(c + ML_LOOKAHEAD))
        state = finish(c, pending.pop(0), state)


def _mlstm(fm, z, cwq_t, conv_w, conv_b, colpack, rowpack4, ng_col, B, S):
    nc = S // ML_L
    dk, dv = ML_QK_DIM, ML_V_DIM
    row_spec = lambda slab: pl.BlockSpec((None, None, nc, ML_L), lambda b, h: (b, slab + h, 0, 0))
    return pl.pallas_call(
        _mlstm_kernel,
        grid=(B, ML_HEADS),
        in_specs=[
            pl.BlockSpec((dk, S), lambda b, h: (FR_MQ // dk + h, b)),
            pl.BlockSpec((S, LANES), lambda b, h: (b, ZB_MK + h)),
            pl.BlockSpec((dv, S), lambda b, h: (FR_MV // dv + h, b)),
            pl.BlockSpec((dv, S), lambda b, h: (FR_MO // dv + h, b)),
            pl.BlockSpec((dv, S), lambda b, h: (FR_MZ // dv + h, b)),
            pl.BlockSpec((dk, LANES), lambda b, h: (h, 0)),
            pl.BlockSpec((CONV_WIDTH, LANES), lambda b, h: (0, ML_HEADS + h)),
            pl.BlockSpec((1, LANES), lambda b, h: (0, ML_HEADS + h)),
            pl.BlockSpec((S, LANES), lambda b, h: (b, 0)),
            row_spec(RS_BETA), row_spec(RS_G), row_spec(RS_M),
            pl.BlockSpec((dv, 1), lambda b, h: (h, 0)),
        ],
        out_specs=pl.BlockSpec((S, dv), lambda b, h: (b, h)),
        out_shape=jax.ShapeDtypeStruct((B * S, ML_WIDTH), BF16),
        scratch_shapes=[
            pltpu.VMEM((dk, S), BF16),
            pltpu.VMEM((S, dk), F32),
            pltpu.VMEM((dv + BF16_ROWS, S), BF16),
        ],
        compiler_params=pltpu.CompilerParams(
            dimension_semantics=("arbitrary", "arbitrary"), vmem_limit_bytes=VMEM_LIMIT),
        name="mlstm",
    )(fm, z, fm, fm, fm, cwq_t, conv_w, conv_b, colpack, rowpack4, rowpack4, rowpack4, ng_col)


def _out_kernel(oa_ref, hb_ref, ga_ref, gb_ref, x_ref, wfd_ref, wmd_ref, wo_ref, bga_ref, bgb_ref,
                fg_ref, o_ref):
    ya = _dot(oa_ref[...], wfd_ref[...])
    yb = _dot(hb_ref[...], wmd_ref[...])
    gate_a = jax.nn.sigmoid(ga_ref[...].astype(F32) + bga_ref[...])
    gate_b = jax.nn.sigmoid(gb_ref[...].astype(F32) + bgb_ref[...])
    y = gate_a * ya + gate_b * yb
    r = x_ref[...] + _dot(y.astype(BF16), wo_ref[...])
    ms = jnp.mean(r * r, axis=-1, keepdims=True)
    o_ref[...] = r * lax.rsqrt(ms + EPS) * fg_ref[...]


def _out_stage(oa, hb, z, x2, wfd, wmd, wo, b_gate, final_g):
    T = x2.shape[0]
    row = lambda i: (i, 0)
    const = lambda i: (0, 0)
    return pl.pallas_call(
        _out_kernel,
        grid=(T // TM_OUT,),
        in_specs=[
            pl.BlockSpec((TM_OUT, FOX_WIDTH), row),
            pl.BlockSpec((TM_OUT, ML_WIDTH), row),
            pl.BlockSpec((TM_OUT, D_MODEL), lambda i: (i, ZB_GA // 8)),
            pl.BlockSpec((TM_OUT, D_MODEL), lambda i: (i, ZB_GB // 8)),
            pl.BlockSpec((TM_OUT, D_MODEL), row),
            pl.BlockSpec((FOX_WIDTH, D_MODEL), const),
            pl.BlockSpec((ML_WIDTH, D_MODEL), const),
            pl.BlockSpec((D_MODEL, D_MODEL), const),
            pl.BlockSpec((1, D_MODEL), lambda i: (0, 0)),
            pl.BlockSpec((1, D_MODEL), lambda i: (0, 1)),
            pl.BlockSpec((1, D_MODEL), const),
        ],
        out_specs=pl.BlockSpec((TM_OUT, D_MODEL), row),
        out_shape=jax.ShapeDtypeStruct((T, D_MODEL), F32),
        compiler_params=pltpu.CompilerParams(
            dimension_semantics=("arbitrary",), vmem_limit_bytes=VMEM_LIMIT),
        name="outstage",
    )(oa, hb, z, z, x2, wfd, wmd, wo, b_gate, b_gate, final_g)


def _layer(x2, B, S, norm_g, w_in, b_fox_f, conv_w, conv_b, b_ml_i, b_ml_f, ml_norm_g, b_gate,
           w_fox_down, w_ml_down, w_out, out_g):
    o_fq, o_fk, o_fv = 0, FOX_WIDTH, 2 * FOX_WIDTH
    o_ff = 3 * FOX_WIDTH
    o_fz = o_ff + FOX_HEADS
    o_mq = o_fz + FOX_WIDTH
    o_mk = o_mq + ML_QK_WIDTH
    o_mv = o_mk + ML_QK_WIDTH
    o_mi = o_mv + ML_WIDTH
    o_mf = o_mi + ML_HEADS
    o_mo = o_mf + ML_HEADS
    o_mz = o_mo + ML_WIDTH
    o_ga = o_mz + ML_WIDTH
    o_gb = o_ga + D_MODEL
    cols = lambda a, n: w_in[:, a:a + n]
    zpad = lambda n: jnp.zeros((D_MODEL, n), F32)
    w_fm = jnp.transpose(jnp.concatenate(
        [cols(o_fq, FOX_WIDTH), cols(o_fv, FOX_WIDTH), cols(o_mq, ML_QK_WIDTH), cols(o_mv, ML_WIDTH),
         cols(o_mo, ML_WIDTH), cols(o_mz, ML_WIDTH),
         cols(o_ff, FOX_HEADS), cols(o_mi, ML_HEADS), zpad(8 - ML_HEADS), cols(o_mf, ML_HEADS),
         zpad(GATE_WIDTH - GL_LF - ML_HEADS)], axis=1)).astype(BF16)
    w_tm = jnp.concatenate([cols(o_fk, FOX_WIDTH), cols(o_fz, FOX_WIDTH), cols(o_ga, D_MODEL),
                            cols(o_gb, D_MODEL), cols(o_mk, ML_QK_WIDTH)], axis=1).astype(BF16)
    zrow = lambda n: jnp.zeros((n,), F32)
    bias_col = jnp.concatenate(
        [b_fox_f, b_ml_i, zrow(8 - ML_HEADS), b_ml_f, zrow(GATE_WIDTH - GL_LF - ML_HEADS)]).reshape(GATE_WIDTH, 1)
    cwq_t = jnp.concatenate(
        [jnp.transpose(conv_w[:, :ML_QK_WIDTH]), conv_b[:ML_QK_WIDTH, None],
         jnp.zeros((ML_QK_WIDTH, LANES - CONV_WIDTH - 1), F32)], axis=1)

    fm, zg, z = _inproj(x2, norm_g.reshape(1, D_MODEL), w_fm, w_tm)
    rowpack, colpack, aqt, ak = _gates(zg, bias_col, B, S)
    rowpack4 = rowpack.reshape(B, 24, S // ML_L, ML_L)
    oa = _fox(fm, z, aqt, ak, B, S)
    hb = _mlstm(fm, z, cwq_t, conv_w, conv_b.reshape(1, 2 * ML_QK_WIDTH), colpack, rowpack4,
                ml_norm_g.reshape(ML_WIDTH, 1), B, S)
    return _out_stage(oa, hb, z, x2, w_fox_down.astype(BF16), w_ml_down.astype(BF16),
                      w_out.astype(BF16), b_gate.reshape(1, 2 * D_MODEL), out_g.reshape(1, D_MODEL))


def kernel(x, norm_g, w_in, b_fox_f, conv_w, conv_b, b_ml_i, b_ml_f, ml_norm_g, b_gate,
           w_fox_down, w_ml_down, w_out, final_g):
    B, S, _ = x.shape
    depth = norm_g.shape[0]
    assert depth == 1, "the final RMSNorm is fused into the (single) layer's output stage"
    out = _layer(x.reshape(B * S, D_MODEL), B, S, norm_g[0], w_in[0], b_fox_f[0], conv_w[0], conv_b[0],
                 b_ml_i[0], b_ml_f[0], ml_norm_g[0], b_gate[0], w_fox_down[0], w_ml_down[0],
                 w_out[0], final_g)
    return out.reshape(B, S, D_MODEL)
```

```python
import jax
import jax.numpy as jnp
from jax import lax
from jax.experimental import pallas as pl
from jax.experimental.pallas import tpu as pltpu

F32 = jnp.float32
BF16 = jnp.bfloat16

D_MODEL = 1024
FOX_HEADS = 8
FOX_HEAD_DIM = 128
FOX_WIDTH = FOX_HEADS * FOX_HEAD_DIM
ML_HEADS = 4
ML_QK_DIM = 128
ML_V_DIM = 256
ML_QK_WIDTH = ML_HEADS * ML_QK_DIM
ML_WIDTH = ML_HEADS * ML_V_DIM
CONV_WIDTH = 4
EPS = 1e-6

LANES = 128
SUBLANES = 8
BF16_ROWS = 16
GATE_WIDTH = LANES
VMEM_LIMIT = 56 * 1024 * 1024
LOG2E = 1.4426950408889634
FOX_Q_SCALE = FOX_HEAD_DIM ** -0.5 * LOG2E

FR_FQ = 0
FR_FV = FR_FQ + FOX_WIDTH
FR_MQ = FR_FV + FOX_WIDTH
FR_MV = FR_MQ + ML_QK_WIDTH
FR_MO = FR_MV + ML_WIDTH
FR_MZ = FR_MO + ML_WIDTH
FR_GATE = FR_MZ + ML_WIDTH
FM_ROWS = FR_GATE + GATE_WIDTH
FM_TILE = 1152
ZB_FK, ZB_FZ, ZB_GA, ZB_GB, ZB_MK = 0, 8, 16, 24, 32
Z_WIDTH = 4608
TN_IN = 1536
TM_IN = 512

GL_FOX, GL_IG, GL_LF = 0, 8, 16
RS_BETA, RS_G, RS_M = 0, 8, 16
CL_BETA = 0
AUG_LANES = 6 * FOX_HEADS

TQ = 256
FOX_LOOKAHEAD = 6
ML_L = 256
ML_LOOKAHEAD = 2
TM_OUT = 1024
OUT_SUB = 256

NT_DIMS = (((1,), (1,)), ((), ()))


def _dot(a, b):
    return jnp.dot(a, b, preferred_element_type=F32)


O_FQ, O_FK, O_FV = 0, FOX_WIDTH, 2 * FOX_WIDTH
O_FF = 3 * FOX_WIDTH
O_FZ = O_FF + FOX_HEADS
O_MQ = O_FZ + FOX_WIDTH
O_MK = O_MQ + ML_QK_WIDTH
O_MV = O_MK + ML_QK_WIDTH
O_MI = O_MV + ML_WIDTH
O_MF = O_MI + ML_HEADS
O_MO = O_MF + ML_HEADS
O_MZ = O_MO + ML_WIDTH
O_GA = O_MZ + ML_WIDTH
O_GB = O_GA + D_MODEL
IN_WIDTH = O_GB + D_MODEL

PREP_UNIT = 512
PREP_BLOCKS = PREP_UNIT // LANES + 1
PREP_SHIFTS = (0, 8, 16)
FM_GROUPS = ((O_FQ, FOX_WIDTH), (O_FV, FOX_WIDTH), (O_MQ, ML_QK_WIDTH), (O_MV, ML_WIDTH),
             (O_MO, ML_WIDTH), (O_MZ, ML_WIDTH))
TM_GROUPS = ((O_FK, FOX_WIDTH), (O_FZ, FOX_WIDTH), (O_GA, D_MODEL), (O_GB, D_MODEL),
             (O_MK, ML_QK_WIDTH))
KIND_GATE = len(PREP_SHIFTS)
KIND_TM = KIND_GATE + 1


def _prep_units():
    starts, kinds = [], []
    for base, groups in ((0, FM_GROUPS), (KIND_TM, TM_GROUPS)):
        if base == KIND_TM:
            starts.append(0)
            kinds.append(KIND_GATE)
        for off, width in groups:
            for s in range(off, off + width, PREP_UNIT):
                starts.append(s // LANES)
                kinds.append(base + PREP_SHIFTS.index(s % LANES))
    return starts, kinds


PREP_STARTS, PREP_KINDS = _prep_units()
N_FM_UNITS = PREP_KINDS.index(KIND_GATE) + 1
FM_W_ROWS = N_FM_UNITS * PREP_UNIT


def _prep_kernel(start_ref, kind_ref, b0, b1, b2, b3, b4, wg_ref, fm_ref, tm_ref):
    kind = kind_ref[pl.program_id(0)]

    def window(shift):
        blocks = [b0[...], b1[...], b2[...], b3[...]] + ([b4[...]] if shift else [])
        return jnp.concatenate(blocks, axis=1)[:, shift:shift + PREP_UNIT]

    for i, shift in enumerate(PREP_SHIFTS):
        @pl.when(kind == i)
        def _(shift=shift):
            fm_ref[...] = jnp.transpose(window(shift)).astype(BF16)

        @pl.when(kind == KIND_TM + i)
        def _(shift=shift):
            tm_ref[...] = window(shift).astype(BF16)

    @pl.when(kind == KIND_GATE)
    def _():
        gate_rows = jnp.transpose(wg_ref[...])
        zeros = jnp.zeros((PREP_UNIT - GATE_WIDTH, D_MODEL), F32)
        fm_ref[...] = jnp.concatenate([gate_rows, zeros], axis=0).astype(BF16)


def _prep_weights(w_in, w_gate):
    n_units = len(PREP_KINDS)
    src = lambda k: pl.BlockSpec((D_MODEL, LANES), lambda u, start, kind: (0, start[u] + k))
    grid_spec = pltpu.PrefetchScalarGridSpec(
        num_scalar_prefetch=2,
        grid=(n_units,),
        in_specs=[src(k) for k in range(PREP_BLOCKS)] + [
            pl.BlockSpec((D_MODEL, GATE_WIDTH), lambda u, start, kind: (0, 0))],
        out_specs=[
            pl.BlockSpec((PREP_UNIT, D_MODEL), lambda u, start, kind: (jnp.minimum(u, N_FM_UNITS - 1), 0)),
            pl.BlockSpec((D_MODEL, PREP_UNIT), lambda u, start, kind: (0, jnp.maximum(u - N_FM_UNITS, 0))),
        ],
    )
    return pl.pallas_call(
        _prep_kernel,
        grid_spec=grid_spec,
        out_shape=[
            jax.ShapeDtypeStruct((FM_W_ROWS, D_MODEL), BF16),
            jax.ShapeDtypeStruct((D_MODEL, Z_WIDTH), BF16),
        ],
        compiler_params=pltpu.CompilerParams(
            dimension_semantics=("arbitrary",), vmem_limit_bytes=VMEM_LIMIT),
        name="prep",
    )(jnp.asarray(PREP_STARTS, jnp.int32), jnp.asarray(PREP_KINDS, jnp.int32),
      *([w_in] * PREP_BLOCKS), w_gate)


def _inproj_kernel(x_ref, g_ref, wfm_ref, wtm_ref, fm_ref, zg_ref, z_ref):
    x = x_ref[...]
    ms = jnp.mean(x * x, axis=-1, keepdims=True)
    xh = (x * lax.rsqrt(ms + EPS) * g_ref[...]).astype(BF16)
    for r0 in range(0, FM_ROWS, FM_TILE):
        res = lax.dot_general(wfm_ref[r0:r0 + FM_TILE, :], xh, NT_DIMS, preferred_element_type=F32)
        r1 = min(r0 + FM_TILE, FR_GATE)
        q1 = max(r0, min(r1, FR_FV))
        if r0 < q1:
            fm_ref[r0:q1, :] = (res[:q1 - r0, :] * FOX_Q_SCALE).astype(BF16)
        if q1 < r1:
            fm_ref[q1:r1, :] = res[q1 - r0:r1 - r0, :].astype(BF16)
        if r0 + FM_TILE > FR_GATE:
            zg_ref[...] = res[FR_GATE - r0:, :]
    for c0 in range(0, Z_WIDTH, TN_IN):
        z_ref[:, c0:c0 + TN_IN] = _dot(xh, wtm_ref[:, c0:c0 + TN_IN]).astype(BF16)


def _inproj(x2, norm_g, w_fm, w_tm):
    T = x2.shape[0]
    resident = pl.Buffered(1)
    return pl.pallas_call(
        _inproj_kernel,
        grid=(T // TM_IN,),
        in_specs=[
            pl.BlockSpec((TM_IN, D_MODEL), lambda i: (i, 0)),
            pl.BlockSpec((1, D_MODEL), lambda i: (0, 0)),
            pl.BlockSpec((FM_W_ROWS, D_MODEL), lambda i: (0, 0), pipeline_mode=resident),
            pl.BlockSpec((D_MODEL, Z_WIDTH), lambda i: (0, 0), pipeline_mode=resident),
        ],
        out_specs=[
            pl.BlockSpec((FR_GATE, TM_IN), lambda i: (0, i)),
            pl.BlockSpec((GATE_WIDTH, TM_IN), lambda i: (0, i)),
            pl.BlockSpec((TM_IN, Z_WIDTH), lambda i: (i, 0)),
        ],
        out_shape=[
            jax.ShapeDtypeStruct((FR_GATE, T), BF16),
            jax.ShapeDtypeStruct((GATE_WIDTH, T), F32),
            jax.ShapeDtypeStruct((T, Z_WIDTH), BF16),
        ],
        compiler_params=pltpu.CompilerParams(
            dimension_semantics=("arbitrary",), vmem_limit_bytes=VMEM_LIMIT),
        name="inproj",
    )(x2, norm_g, w_fm, w_tm)


def _scan_lanes(x, op, ident):
    n = x.shape[1]
    lane = lax.broadcasted_iota(jnp.int32, x.shape, 1)
    k = 1
    while k < n:
        shifted = pltpu.roll(x, k, 1)
        x = op(x, jnp.where(lane >= k, shifted, ident))
        k *= 2
    return x


def _log_sigmoid(x):
    return jnp.minimum(x, 0.0) - jnp.log1p(jnp.exp(-jnp.abs(x)))


def _split3(x):
    hi = x.astype(BF16).astype(F32)
    r = x - hi
    mid = r.astype(BF16).astype(F32)
    lo = (r - mid).astype(BF16).astype(F32)
    return hi, mid, lo


def _gate_kernel(zg_ref, bias_ref, row_ref, col_ref, aq_ref, ak_ref):
    t = zg_ref[...] + bias_ref[...]
    S = t.shape[1]
    c = _scan_lanes(_log_sigmoid(t[GL_FOX:GL_FOX + 8]), jnp.add, 0.0)
    ig = t[GL_IG:GL_IG + 8]
    fsum = _scan_lanes(_log_sigmoid(t[GL_LF:GL_LF + 8]), jnp.add, 0.0)
    beta = (ig - fsum) * LOG2E
    gmax = jnp.maximum(_scan_lanes(beta, jnp.maximum, -jnp.inf), 0.0)
    m = fsum * LOG2E + gmax
    row_ref[RS_BETA:RS_BETA + 8, :] = beta
    row_ref[RS_G:RS_G + 8, :] = gmax
    row_ref[RS_M:RS_M + 8, :] = m
    col_ref[...] = jnp.transpose(jnp.concatenate([beta, jnp.zeros((LANES - 8, S), F32)], axis=0))
    hi, mid, lo = _split3(c * LOG2E)
    ones = jnp.ones((8, S), F32)
    pad = jnp.zeros((LANES - AUG_LANES, S), F32)
    aq_ref[...] = jnp.concatenate([hi, mid, lo, ones, ones, ones, pad], axis=0).astype(BF16)
    ak_ref[...] = jnp.transpose(jnp.concatenate([ones, ones, ones, -hi, -mid, -lo, pad], axis=0)).astype(BF16)


def _gates(zg, bias_col, B, S):
    return pl.pallas_call(
        _gate_kernel,
        grid=(B,),
        in_specs=[
            pl.BlockSpec((GATE_WIDTH, S), lambda b: (0, b)),
            pl.BlockSpec((GATE_WIDTH, 1), lambda b: (0, 0)),
        ],
        out_specs=[
            pl.BlockSpec((None, 24, S), lambda b: (b, 0, 0)),
            pl.BlockSpec((S, LANES), lambda b: (b, 0)),
            pl.BlockSpec((None, LANES, S), lambda b: (b, 0, 0)),
            pl.BlockSpec((S, LANES), lambda b: (b, 0)),
        ],
        out_shape=[
            jax.ShapeDtypeStruct((B, 24, S), F32),
            jax.ShapeDtypeStruct((B * S, LANES), F32),
            jax.ShapeDtypeStruct((B, LANES, S), BF16),
            jax.ShapeDtypeStruct((B * S, LANES), BF16),
        ],
        compiler_params=pltpu.CompilerParams(
            dimension_semantics=("arbitrary",), vmem_limit_bytes=VMEM_LIMIT),
        name="gates",
    )(zg, bias_col)


def _silu(x):
    return x * (1.0 / (1.0 + jnp.exp2(x * -LOG2E)))


def _fox_kernel(qt_ref, k_ref, vt_ref, fz_ref, aqt_ref, ak_ref, o_ref, qaug_ref, kaug_ref, vaug_ref):
    h = pl.program_id(1)
    S = k_ref.shape[0]
    d = FOX_HEAD_DIM
    lane = lax.broadcasted_iota(jnp.int32, (1, LANES), 1)
    keep = jnp.where(((lane & (FOX_HEADS - 1)) == h) & (lane < AUG_LANES), 1.0, 0.0).astype(BF16)
    qaug_ref[:d, :] = qt_ref[...]
    qaug_ref[d:, :] = aqt_ref[...]
    kaug_ref[:, :LANES] = k_ref[...]
    kaug_ref[:, LANES:] = ak_ref[...] * keep
    vaug_ref[:d, :] = vt_ref[...]
    vaug_ref[d:, :] = jnp.ones((BF16_ROWS, S), BF16)
    s_idx = lax.broadcasted_iota(jnp.int32, (TQ, TQ), 0)
    t_idx = lax.broadcasted_iota(jnp.int32, (TQ, TQ), 1)
    causal = s_idx <= t_idx

    ng = S // TQ
    m = [None] * ng
    acc = [None] * ng
    tiles = [(kj, g) for kj in range(ng) for g in range(kj, ng)]

    def scores(kj, g):
        ka = kaug_ref[kj * TQ:(kj + 1) * TQ, :]
        qa = qaug_ref[:, g * TQ:(g + 1) * TQ]
        return _dot(ka, qa)

    def finish(kj, g, st):
        va = vaug_ref[:, kj * TQ:(kj + 1) * TQ]
        if kj == g:
            st = jnp.where(causal, st, -jnp.inf)
        cm = jnp.max(st, axis=0, keepdims=True)
        if kj == 0:
            m[g] = cm
            acc[g] = _dot(va, jnp.exp2(st - cm).astype(BF16))
        else:
            m_new = jnp.maximum(m[g], cm)
            alpha = jnp.exp2(m[g] - m_new)
            acc[g] = alpha * acc[g] + _dot(va, jnp.exp2(st - m_new).astype(BF16))
            m[g] = m_new
        if kj == g:
            a = acc[g]
            o = jnp.transpose(a[:d] * (1.0 / a[d:d + 1]))
            fz = fz_ref[g * TQ:(g + 1) * TQ, :].astype(F32)
            o_ref[g * TQ:(g + 1) * TQ, :] = (o * _silu(fz)).astype(BF16)

    pending = [scores(*tiles[i]) for i in range(FOX_LOOKAHEAD)]
    for i, (kj, g) in enumerate(tiles):
        if i + FOX_LOOKAHEAD < len(tiles):
            pending.append(scores(*tiles[i + FOX_LOOKAHEAD]))
        finish(kj, g, pending.pop(0))


def _fox(fm, z, aqt, ak, B, S):
    d = FOX_HEAD_DIM
    return pl.pallas_call(
        _fox_kernel,
        grid=(B, FOX_HEADS),
        in_specs=[
            pl.BlockSpec((d, S), lambda b, h: (FR_FQ // d + h, b)),
            pl.BlockSpec((S, LANES), lambda b, h: (b, ZB_FK + h)),
            pl.BlockSpec((d, S), lambda b, h: (FR_FV // d + h, b)),
            pl.BlockSpec((S, LANES), lambda b, h: (b, ZB_FZ + h)),
            pl.BlockSpec((None, LANES, S), lambda b, h: (b, 0, 0)),
            pl.BlockSpec((S, LANES), lambda b, h: (b, 0)),
        ],
        out_specs=pl.BlockSpec((S, LANES), lambda b, h: (b, h)),
        out_shape=jax.ShapeDtypeStruct((B * S, FOX_WIDTH), BF16),
        scratch_shapes=[
            pltpu.VMEM((2 * LANES, S), BF16),
            pltpu.VMEM((S, 2 * LANES), BF16),
            pltpu.VMEM((d + BF16_ROWS, S), BF16),
        ],
        compiler_params=pltpu.CompilerParams(
            dimension_semantics=("arbitrary", "arbitrary"), vmem_limit_bytes=VMEM_LIMIT),
        name="fox",
    )(fm, z, fm, z, aqt, ak)


def _conv_taps(x, taps, bias, axis, edge):
    def run(xx, mask_idx):
        y = xx * taps[CONV_WIDTH - 1] + bias
        for r in range(1, CONV_WIDTH):
            xs = pltpu.roll(xx, r, axis)
            if mask_idx is not None:
                xs = jnp.where(mask_idx >= r, xs, 0.0)
            y = y + xs * taps[CONV_WIDTH - 1 - r]
        return y

    if axis == 0:
        head = x[:edge, :]
        body = run(x, None)[edge:, :]
    else:
        head = x[:, :edge]
        body = run(x, None)[:, edge:]
    head = run(head, lax.broadcasted_iota(jnp.int32, head.shape, axis))
    return jnp.concatenate([head, body], axis=axis)


def _mlstm_kernel(qt_ref, k_ref, vt_ref, mot_ref, mzt_ref, cwq_ref, cwk_ref, cbk_ref,
                  col_ref, brow_ref, grow_ref, mrow_ref, ng_ref, o_ref, qs_ref, kf_ref, vaug_ref):
    h = pl.program_id(1)
    S = k_ref.shape[0]
    L = ML_L
    nc = S // L
    dv = ML_V_DIM
    head_shift = (LANES - h) % LANES
    cwq = cwq_ref[...]
    q_taps = [cwq[:, j:j + 1] for j in range(CONV_WIDTH)]
    yq = _conv_taps(qt_ref[...].astype(F32), q_taps, cwq[:, CONV_WIDTH:CONV_WIDTH + 1], 1, LANES)
    qs_ref[...] = (_silu(yq) * (ML_QK_DIM ** -0.5)).astype(BF16)
    cwk = cwk_ref[...]
    k_taps = [cwk[j:j + 1, :] for j in range(CONV_WIDTH)]
    kf_ref[...] = _silu(_conv_taps(k_ref[...].astype(F32), k_taps, cbk_ref[...], 0, SUBLANES))
    vaug_ref[:dv, :] = vt_ref[...]
    vaug_ref[dv:, :] = jnp.ones((BF16_ROWS, S), BF16)
    s_idx = lax.broadcasted_iota(jnp.int32, (L, L), 0)
    t_idx = lax.broadcasted_iota(jnp.int32, (L, L), 1)
    causal = s_idx <= t_idx
    ng = ng_ref[...]

    def scores(c):
        kb = kf_ref[c * L:(c + 1) * L, :].astype(BF16)
        return _dot(kb, qs_ref[:, c * L:(c + 1) * L])

    def finish(c, st, state):
        ct_aug, g_prev = state
        cols = slice(c * L, (c + 1) * L)
        b_col = pltpu.roll(col_ref[cols, :], head_shift, 1)[:, CL_BETA:CL_BETA + 1]
        g_row = grow_ref[c:c + 1, :]
        g_last = g_row[:, L - 1:L]
        va = vaug_ref[:, cols]
        kw = kf_ref[cols, :] * jnp.exp2(b_col - g_last)
        upd = _dot(va, kw.astype(BF16))
        sw = st * jnp.where(causal, jnp.exp2(b_col - g_row), 0.0)
        num = _dot(va, sw.astype(BF16))
        if c > 0:
            w_inter = jnp.exp2(g_prev - g_row)
            num = num + w_inter * _dot(ct_aug.astype(BF16), qs_ref[:, cols])
        den = num[dv:dv + 1, :]
        num = num[:dv, :]
        inv = 1.0 / jnp.maximum(jnp.abs(den), jnp.exp2(-mrow_ref[c:c + 1, :]))
        ssq = jnp.sum(num * num, axis=0, keepdims=True)
        t_scale = inv * lax.rsqrt(inv * inv * ssq * (1.0 / dv) + EPS)
        mo = mot_ref[:, cols].astype(F32)
        mz = mzt_ref[:, cols].astype(F32)
        gate = mz * (1.0 / ((1.0 + jnp.exp2(mo * -LOG2E)) * (1.0 + jnp.exp2(mz * -LOG2E))))
        o_ref[cols, :] = jnp.transpose(num * t_scale * (ng * gate)).astype(BF16)
        if c == 0:
            return upd, g_last
        return jnp.exp2(g_prev - g_last) * ct_aug + upd, g_last

    pending = [scores(c) for c in range(min(ML_LOOKAHEAD, nc))]
    state = (None, None)
    for c in range(nc):
        if c + ML_LOOKAHEAD < nc:
            pending.append(scores(c + ML_LOOKAHEAD))
        state = finish(c, pending.pop(0), state)


def _mlstm(fm, z, cwq_t, conv_w, conv_b, colpack, rowpack4, ng_col, B, S):
    nc = S // ML_L
    dk, dv = ML_QK_DIM, ML_V_DIM
    row_spec = lambda slab: pl.BlockSpec((None, None, nc, ML_L), lambda b, h: (b, slab + h, 0, 0))
    return pl.pallas_call(
        _mlstm_kernel,
        grid=(B, ML_HEADS),
        in_specs=[
            pl.BlockSpec((dk, S), lambda b, h: (FR_MQ // dk + h, b)),
            pl.BlockSpec((S, LANES), lambda b, h: (b, ZB_MK + h)),
            pl.BlockSpec((dv, S), lambda b, h: (FR_MV // dv + h, b)),
            pl.BlockSpec((dv, S), lambda b, h: (FR_MO // dv + h, b)),
            pl.BlockSpec((dv, S), lambda b, h: (FR_MZ // dv + h, b)),
            pl.BlockSpec((dk, LANES), lambda b, h: (h, 0)),
            pl.BlockSpec((CONV_WIDTH, LANES), lambda b, h: (0, ML_HEADS + h)),
            pl.BlockSpec((1, LANES), lambda b, h: (0, ML_HEADS + h)),
            pl.BlockSpec((S, LANES), lambda b, h: (b, 0)),
            row_spec(RS_BETA), row_spec(RS_G), row_spec(RS_M),
            pl.BlockSpec((dv, 1), lambda b, h: (h, 0)),
        ],
        out_specs=pl.BlockSpec((S, dv), lambda b, h: (b, h)),
        out_shape=jax.ShapeDtypeStruct((B * S, ML_WIDTH), BF16),
        scratch_shapes=[
            pltpu.VMEM((dk, S), BF16),
            pltpu.VMEM((S, dk), F32),
            pltpu.VMEM((dv + BF16_ROWS, S), BF16),
        ],
        compiler_params=pltpu.CompilerParams(
            dimension_semantics=("arbitrary", "arbitrary"), vmem_limit_bytes=VMEM_LIMIT),
        name="mlstm",
    )(fm, z, fm, fm, fm, cwq_t, conv_w, conv_b, colpack, rowpack4, rowpack4, rowpack4, ng_col)


def _out_kernel(oa_ref, hb_ref, ga_ref, gb_ref, x_ref, wfd_ref, wmd_ref, wo_ref, bga_ref, bgb_ref,
                fg_ref, o_ref):
    def branches(s):
        rows = slice(s * OUT_SUB, (s + 1) * OUT_SUB)
        return _dot(oa_ref[rows, :], wfd_ref[...]), _dot(hb_ref[rows, :], wmd_ref[...])

    def finish(s, ya, yb):
        rows = slice(s * OUT_SUB, (s + 1) * OUT_SUB)
        gate_a = jax.nn.sigmoid(ga_ref[rows, :].astype(F32) + bga_ref[...])
        gate_b = jax.nn.sigmoid(gb_ref[rows, :].astype(F32) + bgb_ref[...])
        y = gate_a * ya + gate_b * yb
        r = x_ref[rows, :] + _dot(y.astype(BF16), wo_ref[...])
        ms = jnp.mean(r * r, axis=-1, keepdims=True)
        o_ref[rows, :] = r * lax.rsqrt(ms + EPS) * fg_ref[...]

    n_sub = TM_OUT // OUT_SUB
    pending = branches(0)
    for s in range(n_sub):
        nxt = branches(s + 1) if s + 1 < n_sub else None
        finish(s, *pending)
        pending = nxt


def _out_stage(oa, hb, z, x2, wfd, wmd, wo, b_gate, final_g):
    T = x2.shape[0]
    row = lambda i: (i, 0)
    const = lambda i: (0, 0)
    return pl.pallas_call(
        _out_kernel,
        grid=(T // TM_OUT,),
        in_specs=[
            pl.BlockSpec((TM_OUT, FOX_WIDTH), row),
            pl.BlockSpec((TM_OUT, ML_WIDTH), row),
            pl.BlockSpec((TM_OUT, D_MODEL), lambda i: (i, ZB_GA // 8)),
            pl.BlockSpec((TM_OUT, D_MODEL), lambda i: (i, ZB_GB // 8)),
            pl.BlockSpec((TM_OUT, D_MODEL), row),
            pl.BlockSpec((FOX_WIDTH, D_MODEL), const, pipeline_mode=pl.Buffered(1)),
            pl.BlockSpec((ML_WIDTH, D_MODEL), const, pipeline_mode=pl.Buffered(1)),
            pl.BlockSpec((D_MODEL, D_MODEL), const, pipeline_mode=pl.Buffered(1)),
            pl.BlockSpec((1, D_MODEL), lambda i: (0, 0)),
            pl.BlockSpec((1, D_MODEL), lambda i: (0, 1)),
            pl.BlockSpec((1, D_MODEL), const),
        ],
        out_specs=pl.BlockSpec((TM_OUT, D_MODEL), row),
        out_shape=jax.ShapeDtypeStruct((T, D_MODEL), F32),
        compiler_params=pltpu.CompilerParams(
            dimension_semantics=("arbitrary",), vmem_limit_bytes=VMEM_LIMIT),
        name="outstage",
    )(oa, hb, z, z, x2, wfd, wmd, wo, b_gate, b_gate, final_g)


def _layer(x2, B, S, norm_g, w_in, b_fox_f, conv_w, conv_b, b_ml_i, b_ml_f, ml_norm_g, b_gate,
           w_fox_down, w_ml_down, w_out, out_g):
    assert w_in.shape == (D_MODEL, IN_WIDTH)
    cols = lambda a, n: w_in[:, a:a + n]
    zpad = lambda n: jnp.zeros((D_MODEL, n), F32)
    w_gate = jnp.concatenate([cols(O_FF, FOX_HEADS), cols(O_MI, ML_HEADS), zpad(8 - ML_HEADS),
                              cols(O_MF, ML_HEADS), zpad(GATE_WIDTH - GL_LF - ML_HEADS)], axis=1)
    w_fm, w_tm = _prep_weights(w_in, w_gate)
    zrow = lambda n: jnp.zeros((n,), F32)
    bias_col = jnp.concatenate(
        [b_fox_f, b_ml_i, zrow(8 - ML_HEADS), b_ml_f, zrow(GATE_WIDTH - GL_LF - ML_HEADS)]).reshape(GATE_WIDTH, 1)
    cwq_t = jnp.concatenate(
        [jnp.transpose(conv_w[:, :ML_QK_WIDTH]), conv_b[:ML_QK_WIDTH, None],
         jnp.zeros((ML_QK_WIDTH, LANES - CONV_WIDTH - 1), F32)], axis=1)

    fm, zg, z = _inproj(x2, norm_g.reshape(1, D_MODEL), w_fm, w_tm)
    rowpack, colpack, aqt, ak = _gates(zg, bias_col, B, S)
    rowpack4 = rowpack.reshape(B, 24, S // ML_L, ML_L)
    oa = _fox(fm, z, aqt, ak, B, S)
    hb = _mlstm(fm, z, cwq_t, conv_w, conv_b.reshape(1, 2 * ML_QK_WIDTH), colpack, rowpack4,
                ml_norm_g.reshape(ML_WIDTH, 1), B, S)
    return _out_stage(oa, hb, z, x2, w_fox_down.astype(BF16), w_ml_down.astype(BF16),
                      w_out.astype(BF16), b_gate.reshape(1, 2 * D_MODEL), out_g.reshape(1, D_MODEL))


def kernel(x, norm_g, w_in, b_fox_f, conv_w, conv_b, b_ml_i, b_ml_f, ml_norm_g, b_gate,
           w_fox_down, w_ml_down, w_out, final_g):
    B, S, _ = x.shape
    depth = norm_g.shape[0]
    assert depth == 1, "the final RMSNorm is fused into the (single) layer's output stage"
    out = _layer(x.reshape(B * S, D_MODEL), B, S, norm_g[0], w_in[0], b_fox_f[0], conv_w[0], conv_b[0],
                 b_ml_i[0], b_ml_f[0], ml_norm_g[0], b_gate[0], w_fox_down[0], w_ml_down[0],
                 w_out[0], final_g)
    return out.reshape(B, S, D_MODEL)
```

```python
import jax
import jax.numpy as jnp
from jax import lax
from jax.experimental import pallas as pl
from jax.experimental.pallas import tpu as pltpu

F32 = jnp.float32
BF16 = jnp.bfloat16

D_MODEL = 1024
FOX_HEADS = 8
FOX_HEAD_DIM = 128
FOX_WIDTH = FOX_HEADS * FOX_HEAD_DIM
ML_HEADS = 4
ML_QK_DIM = 128
ML_V_DIM = 256
ML_QK_WIDTH = ML_HEADS * ML_QK_DIM
ML_WIDTH = ML_HEADS * ML_V_DIM
CONV_WIDTH = 4
EPS = 1e-6

LANES = 128
SUBLANES = 8
BF16_ROWS = 16
GATE_WIDTH = LANES
VMEM_LIMIT = 56 * 1024 * 1024
LOG2E = 1.4426950408889634
FOX_Q_SCALE = FOX_HEAD_DIM ** -0.5 * LOG2E

FR_FQ = 0
FR_FV = FR_FQ + FOX_WIDTH
FR_MQ = FR_FV + FOX_WIDTH
FR_MV = FR_MQ + ML_QK_WIDTH
FR_MO = FR_MV + ML_WIDTH
FR_MZ = FR_MO + ML_WIDTH
FR_GATE = FR_MZ + ML_WIDTH
FM_ROWS = FR_GATE + GATE_WIDTH
FM_TILE = 1152
ZB_FK, ZB_FZ, ZB_GA, ZB_GB, ZB_MK = 0, 8, 16, 24, 32
Z_WIDTH = 4608
TN_IN = 1536
TM_IN = 512

GL_FOX, GL_IG, GL_LF = 0, 8, 16
RS_BETA, RS_G, RS_M = 0, 8, 16
CL_BETA = 0
AUG_LANES = 6 * FOX_HEADS

TQ = 256
FOX_LOOKAHEAD = 6
ML_L = 256
ML_LOOKAHEAD = 2
TM_OUT = 1024
OUT_SUB = 256

NT_DIMS = (((1,), (1,)), ((), ()))


def _dot(a, b):
    return jnp.dot(a, b, preferred_element_type=F32)


O_FQ, O_FK, O_FV = 0, FOX_WIDTH, 2 * FOX_WIDTH
O_FF = 3 * FOX_WIDTH
O_FZ = O_FF + FOX_HEADS
O_MQ = O_FZ + FOX_WIDTH
O_MK = O_MQ + ML_QK_WIDTH
O_MV = O_MK + ML_QK_WIDTH
O_MI = O_MV + ML_WIDTH
O_MF = O_MI + ML_HEADS
O_MO = O_MF + ML_HEADS
O_MZ = O_MO + ML_WIDTH
O_GA = O_MZ + ML_WIDTH
O_GB = O_GA + D_MODEL
IN_WIDTH = O_GB + D_MODEL

PREP_UNIT = 512
FM_GROUPS = ((O_FQ, FOX_WIDTH), (O_FV, FOX_WIDTH), (O_MQ, ML_QK_WIDTH), (O_MV, ML_WIDTH),
             (O_MO, ML_WIDTH), (O_MZ, ML_WIDTH))
TM_GROUPS = ((O_FK, FOX_WIDTH), (O_FZ, FOX_WIDTH), (O_GA, D_MODEL), (O_GB, D_MODEL),
             (O_MK, ML_QK_WIDTH))
KIND_FM, KIND_GATE, KIND_TM = 0, 1, 2


def _prep_units():
    units = [(s, KIND_FM) for off, width in FM_GROUPS for s in range(off, off + width, PREP_UNIT)]
    units.append((0, KIND_GATE))
    units += [(s, KIND_TM) for off, width in TM_GROUPS for s in range(off, off + width, PREP_UNIT)]
    assert all(s % SUBLANES == 0 for s, _ in units)
    return [s // SUBLANES for s, _ in units], [k for _, k in units]


PREP_STARTS, PREP_KINDS = _prep_units()
N_FM_UNITS = PREP_KINDS.index(KIND_GATE) + 1
FM_W_ROWS = N_FM_UNITS * PREP_UNIT


def _prep_kernel(start_ref, kind_ref, w_ref, ff_ref, mimf_ref, fm_ref, tm_ref):
    kind = kind_ref[pl.program_id(0)]

    @pl.when(kind == KIND_FM)
    def _():
        fm_ref[...] = w_ref[...].astype(BF16)

    @pl.when(kind == KIND_TM)
    def _():
        tm_ref[...] = jnp.transpose(w_ref[...]).astype(BF16)

    @pl.when(kind == KIND_GATE)
    def _():
        mimf = mimf_ref[...]
        pad = lambda n: jnp.zeros((n, D_MODEL), F32)
        rows = jnp.concatenate(
            [ff_ref[...], mimf[:ML_HEADS], pad(GL_LF - GL_IG - ML_HEADS), mimf[ML_HEADS:],
             pad(PREP_UNIT - GL_LF - ML_HEADS)], axis=0)
        fm_ref[...] = rows.astype(BF16)


def _prep_weights(w_t):
    assert O_MF == O_MI + ML_HEADS and FOX_HEADS == SUBLANES and 2 * ML_HEADS == SUBLANES
    gate_rows = lambda off: pl.BlockSpec((pl.Element(SUBLANES), pl.Element(D_MODEL)),
                                         lambda u, start, kind: (off, 0))
    grid_spec = pltpu.PrefetchScalarGridSpec(
        num_scalar_prefetch=2,
        grid=(len(PREP_KINDS),),
        in_specs=[
            pl.BlockSpec((pl.Element(PREP_UNIT), pl.Element(D_MODEL)),
                         lambda u, start, kind: (start[u] * SUBLANES, 0)),
            gate_rows(O_FF), gate_rows(O_MI),
        ],
        out_specs=[
            pl.BlockSpec((PREP_UNIT, D_MODEL), lambda u, start, kind: (jnp.minimum(u, N_FM_UNITS - 1), 0)),
            pl.BlockSpec((D_MODEL, PREP_UNIT), lambda u, start, kind: (0, jnp.maximum(u - N_FM_UNITS, 0))),
        ],
    )
    return pl.pallas_call(
        _prep_kernel,
        grid_spec=grid_spec,
        out_shape=[
            jax.ShapeDtypeStruct((FM_W_ROWS, D_MODEL), BF16),
            jax.ShapeDtypeStruct((D_MODEL, Z_WIDTH), BF16),
        ],
        compiler_params=pltpu.CompilerParams(
            dimension_semantics=("arbitrary",), vmem_limit_bytes=VMEM_LIMIT),
        name="prep",
    )(jnp.asarray(PREP_STARTS, jnp.int32), jnp.asarray(PREP_KINDS, jnp.int32), w_t, w_t, w_t)


def _inproj_kernel(x_ref, g_ref, wfm_ref, wtm_ref, fm_ref, zg_ref, z_ref):
    x = x_ref[...]
    ms = jnp.mean(x * x, axis=-1, keepdims=True)
    xh = (x * lax.rsqrt(ms + EPS) * g_ref[...]).astype(BF16)
    for r0 in range(0, FM_ROWS, FM_TILE):
        res = lax.dot_general(wfm_ref[r0:r0 + FM_TILE, :], xh, NT_DIMS, preferred_element_type=F32)
        r1 = min(r0 + FM_TILE, FR_GATE)
        q1 = max(r0, min(r1, FR_FV))
        if r0 < q1:
            fm_ref[r0:q1, :] = (res[:q1 - r0, :] * FOX_Q_SCALE).astype(BF16)
        if q1 < r1:
            fm_ref[q1:r1, :] = res[q1 - r0:r1 - r0, :].astype(BF16)
        if r0 + FM_TILE > FR_GATE:
            zg_ref[...] = res[FR_GATE - r0:, :]
    for c0 in range(0, Z_WIDTH, TN_IN):
        z_ref[:, c0:c0 + TN_IN] = _dot(xh, wtm_ref[:, c0:c0 + TN_IN]).astype(BF16)


def _inproj(x2, norm_g, w_fm, w_tm):
    T = x2.shape[0]
    resident = pl.Buffered(1)
    return pl.pallas_call(
        _inproj_kernel,
        grid=(T // TM_IN,),
        in_specs=[
            pl.BlockSpec((TM_IN, D_MODEL), lambda i: (i, 0)),
            pl.BlockSpec((1, D_MODEL), lambda i: (0, 0)),
            pl.BlockSpec((FM_W_ROWS, D_MODEL), lambda i: (0, 0), pipeline_mode=resident),
            pl.BlockSpec((D_MODEL, Z_WIDTH), lambda i: (0, 0), pipeline_mode=resident),
        ],
        out_specs=[
            pl.BlockSpec((FR_GATE, TM_IN), lambda i: (0, i)),
            pl.BlockSpec((GATE_WIDTH, TM_IN), lambda i: (0, i)),
            pl.BlockSpec((TM_IN, Z_WIDTH), lambda i: (i, 0)),
        ],
        out_shape=[
            jax.ShapeDtypeStruct((FR_GATE, T), BF16),
            jax.ShapeDtypeStruct((GATE_WIDTH, T), F32),
            jax.ShapeDtypeStruct((T, Z_WIDTH), BF16),
        ],
        compiler_params=pltpu.CompilerParams(
            dimension_semantics=("arbitrary",), vmem_limit_bytes=VMEM_LIMIT),
        name="inproj",
    )(x2, norm_g, w_fm, w_tm)


def _scan_lanes(x, op, ident):
    n = x.shape[1]
    lane = lax.broadcasted_iota(jnp.int32, x.shape, 1)
    k = 1
    while k < n:
        shifted = pltpu.roll(x, k, 1)
        x = op(x, jnp.where(lane >= k, shifted, ident))
        k *= 2
    return x


def _log_sigmoid(x):
    return jnp.minimum(x, 0.0) - jnp.log1p(jnp.exp(-jnp.abs(x)))


def _split3(x):
    hi = x.astype(BF16).astype(F32)
    r = x - hi
    mid = r.astype(BF16).astype(F32)
    lo = (r - mid).astype(BF16).astype(F32)
    return hi, mid, lo


def _gate_kernel(zg_ref, bias_ref, row_ref, col_ref, aq_ref, ak_ref):
    t = zg_ref[...] + bias_ref[...]
    S = t.shape[1]
    c = _scan_lanes(_log_sigmoid(t[GL_FOX:GL_FOX + 8]), jnp.add, 0.0)
    ig = t[GL_IG:GL_IG + 8]
    fsum = _scan_lanes(_log_sigmoid(t[GL_LF:GL_LF + 8]), jnp.add, 0.0)
    beta = (ig - fsum) * LOG2E
    gmax = jnp.maximum(_scan_lanes(beta, jnp.maximum, -jnp.inf), 0.0)
    m = fsum * LOG2E + gmax
    row_ref[RS_BETA:RS_BETA + 8, :] = beta
    row_ref[RS_G:RS_G + 8, :] = gmax
    row_ref[RS_M:RS_M + 8, :] = m
    col_ref[...] = jnp.transpose(jnp.concatenate([beta, jnp.zeros((LANES - 8, S), F32)], axis=0))
    hi, mid, lo = _split3(c * LOG2E)
    ones = jnp.ones((8, S), F32)
    pad = jnp.zeros((LANES - AUG_LANES, S), F32)
    aq_ref[...] = jnp.concatenate([hi, mid, lo, ones, ones, ones, pad], axis=0).astype(BF16)
    ak_ref[...] = jnp.transpose(jnp.concatenate([ones, ones, ones, -hi, -mid, -lo, pad], axis=0)).astype(BF16)


def _gates(zg, bias_col, B, S):
    return pl.pallas_call(
        _gate_kernel,
        grid=(B,),
        in_specs=[
            pl.BlockSpec((GATE_WIDTH, S), lambda b: (0, b)),
            pl.BlockSpec((GATE_WIDTH, 1), lambda b: (0, 0)),
        ],
        out_specs=[
            pl.BlockSpec((None, 24, S), lambda b: (b, 0, 0)),
            pl.BlockSpec((S, LANES), lambda b: (b, 0)),
            pl.BlockSpec((None, LANES, S), lambda b: (b, 0, 0)),
            pl.BlockSpec((S, LANES), lambda b: (b, 0)),
        ],
        out_shape=[
            jax.ShapeDtypeStruct((B, 24, S), F32),
            jax.ShapeDtypeStruct((B * S, LANES), F32),
            jax.ShapeDtypeStruct((B, LANES, S), BF16),
            jax.ShapeDtypeStruct((B * S, LANES), BF16),
        ],
        compiler_params=pltpu.CompilerParams(
            dimension_semantics=("arbitrary",), vmem_limit_bytes=VMEM_LIMIT),
        name="gates",
    )(zg, bias_col)


def _silu(x):
    return x * (1.0 / (1.0 + jnp.exp2(x * -LOG2E)))


def _fox_kernel(qt_ref, k_ref, vt_ref, fz_ref, aqt_ref, ak_ref, o_ref, qaug_ref, kaug_ref, vaug_ref):
    h = pl.program_id(1)
    S = k_ref.shape[0]
    d = FOX_HEAD_DIM
    lane = lax.broadcasted_iota(jnp.int32, (1, LANES), 1)
    keep = jnp.where(((lane & (FOX_HEADS - 1)) == h) & (lane < AUG_LANES), 1.0, 0.0).astype(BF16)
    qaug_ref[:d, :] = qt_ref[...]
    qaug_ref[d:, :] = aqt_ref[...]
    kaug_ref[:, :LANES] = k_ref[...]
    kaug_ref[:, LANES:] = ak_ref[...] * keep
    vaug_ref[:d, :] = vt_ref[...]
    vaug_ref[d:, :] = jnp.ones((BF16_ROWS, S), BF16)
    s_idx = lax.broadcasted_iota(jnp.int32, (TQ, TQ), 0)
    t_idx = lax.broadcasted_iota(jnp.int32, (TQ, TQ), 1)
    causal = s_idx <= t_idx

    ng = S // TQ
    m = [None] * ng
    acc = [None] * ng
    tiles = [(kj, g) for kj in range(ng) for g in range(kj, ng)]

    def scores(kj, g):
        ka = kaug_ref[kj * TQ:(kj + 1) * TQ, :]
        qa = qaug_ref[:, g * TQ:(g + 1) * TQ]
        return _dot(ka, qa)

    def finish(kj, g, st):
        va = vaug_ref[:, kj * TQ:(kj + 1) * TQ]
        if kj == g:
            st = jnp.where(causal, st, -jnp.inf)
        cm = jnp.max(st, axis=0, keepdims=True)
        if kj == 0:
            m[g] = cm
            acc[g] = _dot(va, jnp.exp2(st - cm).astype(BF16))
        else:
            m_new = jnp.maximum(m[g], cm)
            alpha = jnp.exp2(m[g] - m_new)
            acc[g] = alpha * acc[g] + _dot(va, jnp.exp2(st - m_new).astype(BF16))
            m[g] = m_new
        if kj == g:
            a = acc[g]
            o = jnp.transpose(a[:d] * (1.0 / a[d:d + 1]))
            fz = fz_ref[g * TQ:(g + 1) * TQ, :].astype(F32)
            o_ref[g * TQ:(g + 1) * TQ, :] = (o * _silu(fz)).astype(BF16)

    pending = [scores(*tiles[i]) for i in range(FOX_LOOKAHEAD)]
    for i, (kj, g) in enumerate(tiles):
        if i + FOX_LOOKAHEAD < len(tiles):
            pending.append(scores(*tiles[i + FOX_LOOKAHEAD]))
        finish(kj, g, pending.pop(0))


def _fox(fm, z, aqt, ak, B, S):
    d = FOX_HEAD_DIM
    return pl.pallas_call(
        _fox_kernel,
        grid=(B, FOX_HEADS),
        in_specs=[
            pl.BlockSpec((d, S), lambda b, h: (FR_FQ // d + h, b)),
            pl.BlockSpec((S, LANES), lambda b, h: (b, ZB_FK + h)),
            pl.BlockSpec((d, S), lambda b, h: (FR_FV // d + h, b)),
            pl.BlockSpec((S, LANES), lambda b, h: (b, ZB_FZ + h)),
            pl.BlockSpec((None, LANES, S), lambda b, h: (b, 0, 0)),
            pl.BlockSpec((S, LANES), lambda b, h: (b, 0)),
        ],
        out_specs=pl.BlockSpec((S, LANES), lambda b, h: (b, h)),
        out_shape=jax.ShapeDtypeStruct((B * S, FOX_WIDTH), BF16),
        scratch_shapes=[
            pltpu.VMEM((2 * LANES, S), BF16),
            pltpu.VMEM((S, 2 * LANES), BF16),
            pltpu.VMEM((d + BF16_ROWS, S), BF16),
        ],
        compiler_params=pltpu.CompilerParams(
            dimension_semantics=("arbitrary", "arbitrary"), vmem_limit_bytes=VMEM_LIMIT),
        name="fox",
    )(fm, z, fm, z, aqt, ak)


def _conv_taps(x, taps, bias, axis, edge):
    def run(xx, mask_idx):
        y = xx * taps[CONV_WIDTH - 1] + bias
        for r in range(1, CONV_WIDTH):
            xs = pltpu.roll(xx, r, axis)
            if mask_idx is not None:
                xs = jnp.where(mask_idx >= r, xs, 0.0)
            y = y + xs * taps[CONV_WIDTH - 1 - r]
        return y

    if axis == 0:
        head = x[:edge, :]
        body = run(x, None)[edge:, :]
    else:
        head = x[:, :edge]
        body = run(x, None)[:, edge:]
    head = run(head, lax.broadcasted_iota(jnp.int32, head.shape, axis))
    return jnp.concatenate([head, body], axis=axis)


def _mlstm_kernel(qt_ref, k_ref, vt_ref, mot_ref, mzt_ref, cwq_ref, cwk_ref, cbk_ref,
                  col_ref, brow_ref, grow_ref, mrow_ref, ng_ref, o_ref, qs_ref, kf_ref, vaug_ref):
    h = pl.program_id(1)
    S = k_ref.shape[0]
    L = ML_L
    nc = S // L
    dv = ML_V_DIM
    head_shift = (LANES - h) % LANES
    cwq = cwq_ref[...]
    q_taps = [cwq[:, j:j + 1] for j in range(CONV_WIDTH)]
    yq = _conv_taps(qt_ref[...].astype(F32), q_taps, cwq[:, CONV_WIDTH:CONV_WIDTH + 1], 1, LANES)
    qs_ref[...] = (_silu(yq) * (ML_QK_DIM ** -0.5)).astype(BF16)
    cwk = cwk_ref[...]
    k_taps = [cwk[j:j + 1, :] for j in range(CONV_WIDTH)]
    kf_ref[...] = _silu(_conv_taps(k_ref[...].astype(F32), k_taps, cbk_ref[...], 0, SUBLANES))
    vaug_ref[:dv, :] = vt_ref[...]
    vaug_ref[dv:, :] = jnp.ones((BF16_ROWS, S), BF16)
    s_idx = lax.broadcasted_iota(jnp.int32, (L, L), 0)
    t_idx = lax.broadcasted_iota(jnp.int32, (L, L), 1)
    causal = s_idx <= t_idx
    ng = ng_ref[...]

    def scores(c):
        kb = kf_ref[c * L:(c + 1) * L, :].astype(BF16)
        return _dot(kb, qs_ref[:, c * L:(c + 1) * L])

    def finish(c, st, state):
        ct_aug, g_prev = state
        cols = slice(c * L, (c + 1) * L)
        b_col = pltpu.roll(col_ref[cols, :], head_shift, 1)[:, CL_BETA:CL_BETA + 1]
        g_row = grow_ref[c:c + 1, :]
        g_last = g_row[:, L - 1:L]
        va = vaug_ref[:, cols]
        kw = kf_ref[cols, :] * jnp.exp2(b_col - g_last)
        upd = _dot(va, kw.astype(BF16))
        sw = st * jnp.where(causal, jnp.exp2(b_col - g_row), 0.0)
        num = _dot(va, sw.astype(BF16))
        if c > 0:
            w_inter = jnp.exp2(g_prev - g_row)
            num = num + w_inter * _dot(ct_aug.astype(BF16), qs_ref[:, cols])
        den = num[dv:dv + 1, :]
        num = num[:dv, :]
        inv = 1.0 / jnp.maximum(jnp.abs(den), jnp.exp2(-mrow_ref[c:c + 1, :]))
        ssq = jnp.sum(num * num, axis=0, keepdims=True)
        t_scale = inv * lax.rsqrt(inv * inv * ssq * (1.0 / dv) + EPS)
        mo = mot_ref[:, cols].astype(F32)
        mz = mzt_ref[:, cols].astype(F32)
        gate = mz * (1.0 / ((1.0 + jnp.exp2(mo * -LOG2E)) * (1.0 + jnp.exp2(mz * -LOG2E))))
        o_ref[cols, :] = jnp.transpose(num * t_scale * (ng * gate)).astype(BF16)
        if c == 0:
            return upd, g_last
        return jnp.exp2(g_prev - g_last) * ct_aug + upd, g_last

    pending = [scores(c) for c in range(min(ML_LOOKAHEAD, nc))]
    state = (None, None)
    for c in range(nc):
        if c + ML_LOOKAHEAD < nc:
            pending.append(scores(c + ML_LOOKAHEAD))
        state = finish(c, pending.pop(0), state)


def _mlstm(fm, z, cwq_t, conv_w, conv_b, colpack, rowpack4, ng_col, B, S):
    nc = S // ML_L
    dk, dv = ML_QK_DIM, ML_V_DIM
    row_spec = lambda slab: pl.BlockSpec((None, None, nc, ML_L), lambda b, h: (b, slab + h, 0, 0))
    return pl.pallas_call(
        _mlstm_kernel,
        grid=(B, ML_HEADS),
        in_specs=[
            pl.BlockSpec((dk, S), lambda b, h: (FR_MQ // dk + h, b)),
            pl.BlockSpec((S, LANES), lambda b, h: (b, ZB_MK + h)),
            pl.BlockSpec((dv, S), lambda b, h: (FR_MV // dv + h, b)),
            pl.BlockSpec((dv, S), lambda b, h: (FR_MO // dv + h, b)),
            pl.BlockSpec((dv, S), lambda b, h: (FR_MZ // dv + h, b)),
            pl.BlockSpec((dk, LANES), lambda b, h: (h, 0)),
            pl.BlockSpec((CONV_WIDTH, LANES), lambda b, h: (0, ML_HEADS + h)),
            pl.BlockSpec((1, LANES), lambda b, h: (0, ML_HEADS + h)),
            pl.BlockSpec((S, LANES), lambda b, h: (b, 0)),
            row_spec(RS_BETA), row_spec(RS_G), row_spec(RS_M),
            pl.BlockSpec((dv, 1), lambda b, h: (h, 0)),
        ],
        out_specs=pl.BlockSpec((S, dv), lambda b, h: (b, h)),
        out_shape=jax.ShapeDtypeStruct((B * S, ML_WIDTH), BF16),
        scratch_shapes=[
            pltpu.VMEM((dk, S), BF16),
            pltpu.VMEM((S, dk), F32),
            pltpu.VMEM((dv + BF16_ROWS, S), BF16),
        ],
        compiler_params=pltpu.CompilerParams(
            dimension_semantics=("arbitrary", "arbitrary"), vmem_limit_bytes=VMEM_LIMIT),
        name="mlstm",
    )(fm, z, fm, fm, fm, cwq_t, conv_w, conv_b, colpack, rowpack4, rowpack4, rowpack4, ng_col)


def _out_kernel(oa_ref, hb_ref, ga_ref, gb_ref, x_ref, wfd_ref, wmd_ref, wo_ref, bga_ref, bgb_ref,
                fg_ref, o_ref):
    def branches(s):
        rows = slice(s * OUT_SUB, (s + 1) * OUT_SUB)
        return _dot(oa_ref[rows, :], wfd_ref[...]), _dot(hb_ref[rows, :], wmd_ref[...])

    def finish(s, ya, yb):
        rows = slice(s * OUT_SUB, (s + 1) * OUT_SUB)
        gate_a = jax.nn.sigmoid(ga_ref[rows, :].astype(F32) + bga_ref[...])
        gate_b = jax.nn.sigmoid(gb_ref[rows, :].astype(F32) + bgb_ref[...])
        y = gate_a * ya + gate_b * yb
        r = x_ref[rows, :] + _dot(y.astype(BF16), wo_ref[...])
        ms = jnp.mean(r * r, axis=-1, keepdims=True)
        o_ref[rows, :] = r * lax.rsqrt(ms + EPS) * fg_ref[...]

    n_sub = TM_OUT // OUT_SUB
    pending = branches(0)
    for s in range(n_sub):
        nxt = branches(s + 1) if s + 1 < n_sub else None
        finish(s, *pending)
        pending = nxt


def _out_stage(oa, hb, z, x2, wfd, wmd, wo, b_gate, final_g):
    T = x2.shape[0]
    row = lambda i: (i, 0)
    const = lambda i: (0, 0)
    return pl.pallas_call(
        _out_kernel,
        grid=(T // TM_OUT,),
        in_specs=[
            pl.BlockSpec((TM_OUT, FOX_WIDTH), row),
            pl.BlockSpec((TM_OUT, ML_WIDTH), row),
            pl.BlockSpec((TM_OUT, D_MODEL), lambda i: (i, ZB_GA // 8)),
            pl.BlockSpec((TM_OUT, D_MODEL), lambda i: (i, ZB_GB // 8)),
            pl.BlockSpec((TM_OUT, D_MODEL), row),
            pl.BlockSpec((FOX_WIDTH, D_MODEL), const, pipeline_mode=pl.Buffered(1)),
            pl.BlockSpec((ML_WIDTH, D_MODEL), const, pipeline_mode=pl.Buffered(1)),
            pl.BlockSpec((D_MODEL, D_MODEL), const, pipeline_mode=pl.Buffered(1)),
            pl.BlockSpec((1, D_MODEL), lambda i: (0, 0)),
            pl.BlockSpec((1, D_MODEL), lambda i: (0, 1)),
            pl.BlockSpec((1, D_MODEL), const),
        ],
        out_specs=pl.BlockSpec((TM_OUT, D_MODEL), row),
        out_shape=jax.ShapeDtypeStruct((T, D_MODEL), F32),
        compiler_params=pltpu.CompilerParams(
            dimension_semantics=("arbitrary",), vmem_limit_bytes=VMEM_LIMIT),
        name="outstage",
    )(oa, hb, z, z, x2, wfd, wmd, wo, b_gate, b_gate, final_g)


def _layer(x2, B, S, norm_g, w_in, b_fox_f, conv_w, conv_b, b_ml_i, b_ml_f, ml_norm_g, b_gate,
           w_fox_down, w_ml_down, w_out, out_g):
    assert w_in.shape == (D_MODEL, IN_WIDTH)
    w_fm, w_tm = _prep_weights(jnp.transpose(w_in))
    zrow = lambda n: jnp.zeros((n,), F32)
    bias_col = jnp.concatenate(
        [b_fox_f, b_ml_i, zrow(8 - ML_HEADS), b_ml_f, zrow(GATE_WIDTH - GL_LF - ML_HEADS)]).reshape(GATE_WIDTH, 1)
    cwq_t = jnp.concatenate(
        [jnp.transpose(conv_w[:, :ML_QK_WIDTH]), conv_b[:ML_QK_WIDTH, None],
         jnp.zeros((ML_QK_WIDTH, LANES - CONV_WIDTH - 1), F32)], axis=1)

    fm, zg, z = _inproj(x2, norm_g.reshape(1, D_MODEL), w_fm, w_tm)
    rowpack, colpack, aqt, ak = _gates(zg, bias_col, B, S)
    rowpack4 = rowpack.reshape(B, 24, S // ML_L, ML_L)
    oa = _fox(fm, z, aqt, ak, B, S)
    hb = _mlstm(fm, z, cwq_t, conv_w, conv_b.reshape(1, 2 * ML_QK_WIDTH), colpack, rowpack4,
                ml_norm_g.reshape(ML_WIDTH, 1), B, S)
    return _out_stage(oa, hb, z, x2, w_fox_down.astype(BF16), w_ml_down.astype(BF16),
                      w_out.astype(BF16), b_gate.reshape(1, 2 * D_MODEL), out_g.reshape(1, D_MODEL))


def kernel(x, norm_g, w_in, b_fox_f, conv_w, conv_b, b_ml_i, b_ml_f, ml_norm_g, b_gate,
           w_fox_down, w_ml_down, w_out, final_g):
    B, S, _ = x.shape
    depth = norm_g.shape[0]
    assert depth == 1, "the final RMSNorm is fused into the (single) layer's output stage"
    out = _layer(x.reshape(B * S, D_MODEL), B, S, norm_g[0], w_in[0], b_fox_f[0], conv_w[0], conv_b[0],
                 b_ml_i[0], b_ml_f[0], ml_norm_g[0], b_gate[0], w_fox_down[0], w_ml_down[0],
                 w_out[0], final_g)
    return out.reshape(B, S, D_MODEL)
```

```python
import functools

import jax
import jax.numpy as jnp
from jax import lax
from jax.experimental import pallas as pl
from jax.experimental.pallas import tpu as pltpu

F32 = jnp.float32
BF16 = jnp.bfloat16

D_MODEL = 1024
FOX_HEADS = 8
FOX_HEAD_DIM = 128
FOX_WIDTH = FOX_HEADS * FOX_HEAD_DIM
ML_HEADS = 4
ML_QK_DIM = 128
ML_V_DIM = 256
ML_QK_WIDTH = ML_HEADS * ML_QK_DIM
ML_WIDTH = ML_HEADS * ML_V_DIM
CONV_WIDTH = 4
EPS = 1e-6

LANES = 128
SUBLANES = 8
BF16_ROWS = 16
GATE_WIDTH = LANES
VMEM_LIMIT = 56 * 1024 * 1024
LOG2E = 1.4426950408889634
FOX_Q_SCALE = FOX_HEAD_DIM ** -0.5 * LOG2E

FR_MQ = 0
FR_FQ = FR_MQ + ML_QK_WIDTH
FR_FV = FR_FQ + FOX_WIDTH
FR_MV = FR_FV + FOX_WIDTH
FR_MO = FR_MV + ML_WIDTH
FR_MZ = FR_MO + ML_WIDTH
FR_GATE = FR_MZ + ML_WIDTH
FM_ROWS = FR_GATE + GATE_WIDTH
FM_CHUNKS = ((FR_MQ, FR_FV), (FR_FV, FR_MV), (FR_MV, FR_MO), (FR_MO, FR_MZ), (FR_MZ, FM_ROWS))
ZB_FK, ZB_FZ, ZB_GA, ZB_GB, ZB_MK = 0, 8, 16, 24, 32
Z_WIDTH = 4608
TN_IN = 1536
TM_IN = 512

GL_FOX, GL_IG, GL_LF = 0, 8, 16
RS_G, RS_M = 0, 8
CL_BETA = 0
AUG_LANES = 6 * FOX_HEADS

TQ = 256
FOX_LOOKAHEAD = 6
ML_L = 256
ML_LOOKAHEAD = 2
TM_OUT = 1024
OUT_SUB = 256

NT_DIMS = (((1,), (1,)), ((), ()))


def _dot(a, b):
    return jnp.dot(a, b, preferred_element_type=F32)


O_FQ, O_FK, O_FV = 0, FOX_WIDTH, 2 * FOX_WIDTH
O_FF = 3 * FOX_WIDTH
O_FZ = O_FF + FOX_HEADS
O_MQ = O_FZ + FOX_WIDTH
O_MK = O_MQ + ML_QK_WIDTH
O_MV = O_MK + ML_QK_WIDTH
O_MI = O_MV + ML_WIDTH
O_MF = O_MI + ML_HEADS
O_MO = O_MF + ML_HEADS
O_MZ = O_MO + ML_WIDTH
O_GA = O_MZ + ML_WIDTH
O_GB = O_GA + D_MODEL
IN_WIDTH = O_GB + D_MODEL

PREP_UNIT = 512
FM_GROUPS = ((O_MQ, ML_QK_WIDTH), (O_FQ, FOX_WIDTH), (O_FV, FOX_WIDTH), (O_MV, ML_WIDTH),
             (O_MO, ML_WIDTH), (O_MZ, ML_WIDTH))
TM_GROUPS = ((O_FK, FOX_WIDTH), (O_FZ, FOX_WIDTH), (O_GA, D_MODEL), (O_GB, D_MODEL),
             (O_MK, ML_QK_WIDTH))
KIND_FM, KIND_GATE, KIND_TM = 0, 1, 2


def _prep_units():
    units = [(s, KIND_FM) for off, width in FM_GROUPS for s in range(off, off + width, PREP_UNIT)]
    units.append((0, KIND_GATE))
    units += [(s, KIND_TM) for off, width in TM_GROUPS for s in range(off, off + width, PREP_UNIT)]
    assert all(s % SUBLANES == 0 for s, _ in units)
    return [s // SUBLANES for s, _ in units], [k for _, k in units]


PREP_STARTS, PREP_KINDS = _prep_units()
N_FM_UNITS = PREP_KINDS.index(KIND_GATE) + 1
FM_W_ROWS = N_FM_UNITS * PREP_UNIT


def _prep_kernel(start_ref, kind_ref, w_ref, ff_ref, mimf_ref, fm_ref, tm_ref):
    kind = kind_ref[pl.program_id(0)]

    @pl.when(kind == KIND_FM)
    def _():
        fm_ref[...] = w_ref[...].astype(BF16)

    @pl.when(kind == KIND_TM)
    def _():
        tm_ref[...] = jnp.transpose(w_ref[...]).astype(BF16)

    @pl.when(kind == KIND_GATE)
    def _():
        mimf = mimf_ref[...]
        pad = lambda n: jnp.zeros((n, D_MODEL), F32)
        rows = jnp.concatenate(
            [ff_ref[...], mimf[:ML_HEADS], pad(GL_LF - GL_IG - ML_HEADS), mimf[ML_HEADS:],
             pad(PREP_UNIT - GL_LF - ML_HEADS)], axis=0)
        fm_ref[...] = rows.astype(BF16)


def _prep_weights(w_t):
    assert O_MF == O_MI + ML_HEADS and FOX_HEADS == SUBLANES and 2 * ML_HEADS == SUBLANES
    gate_rows = lambda off: pl.BlockSpec((pl.Element(SUBLANES), pl.Element(D_MODEL)),
                                         lambda u, start, kind: (off, 0))
    grid_spec = pltpu.PrefetchScalarGridSpec(
        num_scalar_prefetch=2,
        grid=(len(PREP_KINDS),),
        in_specs=[
            pl.BlockSpec((pl.Element(PREP_UNIT), pl.Element(D_MODEL)),
                         lambda u, start, kind: (start[u] * SUBLANES, 0)),
            gate_rows(O_FF), gate_rows(O_MI),
        ],
        out_specs=[
            pl.BlockSpec((PREP_UNIT, D_MODEL), lambda u, start, kind: (jnp.minimum(u, N_FM_UNITS - 1), 0)),
            pl.BlockSpec((D_MODEL, PREP_UNIT), lambda u, start, kind: (0, jnp.maximum(u - N_FM_UNITS, 0))),
        ],
    )
    return pl.pallas_call(
        _prep_kernel,
        grid_spec=grid_spec,
        out_shape=[
            jax.ShapeDtypeStruct((FM_W_ROWS, D_MODEL), BF16),
            jax.ShapeDtypeStruct((D_MODEL, Z_WIDTH), BF16),
        ],
        compiler_params=pltpu.CompilerParams(
            dimension_semantics=("arbitrary",), vmem_limit_bytes=VMEM_LIMIT),
        name="prep",
    )(jnp.asarray(PREP_STARTS, jnp.int32), jnp.asarray(PREP_KINDS, jnp.int32), w_t, w_t, w_t)


def _silu(x):
    return x * (1.0 / (1.0 + jnp.exp2(x * -LOG2E)))


def _causal_conv(x, taps, bias, axis, halo):
    def run(xx):
        y = xx * taps[CONV_WIDTH - 1] + bias
        for r in range(1, CONV_WIDTH):
            y = y + pltpu.roll(xx, r, axis) * taps[CONV_WIDTH - 1 - r]
        return y

    edge = halo.shape[axis]
    if axis == 0:
        head = run(jnp.concatenate([halo, x[:edge, :]], axis=0))[edge:, :]
        return jnp.concatenate([head, run(x)[edge:, :]], axis=0)
    head = run(jnp.concatenate([halo, x[:, :edge]], axis=1))[:, edge:]
    return jnp.concatenate([head, run(x)[:, edge:]], axis=1)


def _inproj_kernel(x_ref, g_ref, wfm_ref, wtm_ref, cwq_ref, cwk_ref, cbk_ref,
                   fm_ref, zg_ref, z_ref, qhalo_ref, khalo_ref, *, steps_per_seq):
    @pl.when(pl.program_id(0) % steps_per_seq == 0)
    def _():
        qhalo_ref[...] = jnp.zeros_like(qhalo_ref)
        khalo_ref[...] = jnp.zeros_like(khalo_ref)

    x = x_ref[...]
    ms = jnp.mean(x * x, axis=-1, keepdims=True)
    xh = (x * lax.rsqrt(ms + EPS) * g_ref[...]).astype(BF16)

    def token_major(c0):
        res = _dot(xh, wtm_ref[:, c0:c0 + TN_IN])

        def epilogue():
            if c0 + TN_IN == Z_WIDTH:
                k0 = TN_IN - ML_QK_WIDTH
                xk = res[:, k0:]
                cwk = cwk_ref[...]
                taps = [cwk[j:j + 1, :] for j in range(CONV_WIDTH)]
                yk = _causal_conv(xk, taps, cbk_ref[...], 0, khalo_ref[...])
                khalo_ref[...] = xk[TM_IN - SUBLANES:, :]
                z_ref[:, c0:c0 + k0] = res[:, :k0].astype(BF16)
                z_ref[:, c0 + k0:] = _silu(yk).astype(BF16)
            else:
                z_ref[:, c0:c0 + TN_IN] = res.astype(BF16)
        return epilogue

    def feature_major(r0, r1):
        res = lax.dot_general(wfm_ref[r0:r1, :], xh, NT_DIMS, preferred_element_type=F32)

        def epilogue():
            if r0 == FR_MQ:
                xq = res[:ML_QK_WIDTH, :]
                cwq = cwq_ref[...]
                taps = [cwq[:, j:j + 1] for j in range(CONV_WIDTH)]
                yq = _causal_conv(xq, taps, cwq[:, CONV_WIDTH:CONV_WIDTH + 1], 1, qhalo_ref[...])
                fm_ref[FR_MQ:FR_FQ, :] = (_silu(yq) * (ML_QK_DIM ** -0.5)).astype(BF16)
                qhalo_ref[...] = xq[:, TM_IN - LANES:]
                fm_ref[FR_FQ:FR_FV, :] = (res[ML_QK_WIDTH:, :] * FOX_Q_SCALE).astype(BF16)
            elif r1 == FM_ROWS:
                fm_ref[r0:FR_GATE, :] = res[:FR_GATE - r0, :].astype(BF16)
                zg_ref[...] = res[FR_GATE - r0:, :]
            else:
                fm_ref[r0:r1, :] = res.astype(BF16)
        return epilogue

    chunks = ([functools.partial(token_major, Z_WIDTH - TN_IN)]
              + [functools.partial(feature_major, r0, r1) for r0, r1 in FM_CHUNKS]
              + [functools.partial(token_major, c0) for c0 in range(0, Z_WIDTH - TN_IN, TN_IN)])
    pending = None
    for start in chunks:
        epilogue = start()
        if pending is not None:
            pending()
        pending = epilogue
    pending()


def _inproj(x2, norm_g, w_fm, w_tm, cwq_t, conv_w, conv_b, S):
    T = x2.shape[0]
    assert S % TM_IN == 0 and ZB_MK * LANES + ML_QK_WIDTH == Z_WIDTH
    resident = pl.Buffered(1)
    return pl.pallas_call(
        functools.partial(_inproj_kernel, steps_per_seq=S // TM_IN),
        grid=(T // TM_IN,),
        in_specs=[
            pl.BlockSpec((TM_IN, D_MODEL), lambda i: (i, 0)),
            pl.BlockSpec((1, D_MODEL), lambda i: (0, 0)),
            pl.BlockSpec((FM_W_ROWS, D_MODEL), lambda i: (0, 0), pipeline_mode=resident),
            pl.BlockSpec((D_MODEL, Z_WIDTH), lambda i: (0, 0), pipeline_mode=resident),
            pl.BlockSpec((ML_QK_WIDTH, LANES), lambda i: (0, 0)),
            pl.BlockSpec((CONV_WIDTH, ML_QK_WIDTH), lambda i: (0, 1)),
            pl.BlockSpec((1, ML_QK_WIDTH), lambda i: (0, 1)),
        ],
        out_specs=[
            pl.BlockSpec((FR_GATE, TM_IN), lambda i: (0, i)),
            pl.BlockSpec((GATE_WIDTH, TM_IN), lambda i: (0, i)),
            pl.BlockSpec((TM_IN, Z_WIDTH), lambda i: (i, 0)),
        ],
        out_shape=[
            jax.ShapeDtypeStruct((FR_GATE, T), BF16),
            jax.ShapeDtypeStruct((GATE_WIDTH, T), F32),
            jax.ShapeDtypeStruct((T, Z_WIDTH), BF16),
        ],
        scratch_shapes=[
            pltpu.VMEM((ML_QK_WIDTH, LANES), F32),
            pltpu.VMEM((SUBLANES, ML_QK_WIDTH), F32),
        ],
        compiler_params=pltpu.CompilerParams(
            dimension_semantics=("arbitrary",), vmem_limit_bytes=VMEM_LIMIT),
        name="inproj",
    )(x2, norm_g, w_fm, w_tm, cwq_t, conv_w, conv_b)


def _scan_lanes(x, op, ident):
    n = x.shape[1]
    lane = lax.broadcasted_iota(jnp.int32, x.shape, 1)
    k = 1
    while k < n:
        shifted = pltpu.roll(x, k, 1)
        x = op(x, jnp.where(lane >= k, shifted, ident))
        k *= 2
    return x


def _log_sigmoid(x):
    return jnp.minimum(x, 0.0) - jnp.log1p(jnp.exp(-jnp.abs(x)))


def _split3(x):
    hi = x.astype(BF16).astype(F32)
    r = x - hi
    mid = r.astype(BF16).astype(F32)
    lo = (r - mid).astype(BF16).astype(F32)
    return hi, mid, lo


def _gate_kernel(zg_ref, bias_ref, row_ref, col_ref, aq_ref, ak_ref):
    t = zg_ref[...] + bias_ref[...]
    S = t.shape[1]
    c = _scan_lanes(_log_sigmoid(t[GL_FOX:GL_FOX + 8]), jnp.add, 0.0)
    ig = t[GL_IG:GL_IG + 8]
    fsum = _scan_lanes(_log_sigmoid(t[GL_LF:GL_LF + 8]), jnp.add, 0.0)
    beta = (ig - fsum) * LOG2E
    gmax = jnp.maximum(_scan_lanes(beta, jnp.maximum, -jnp.inf), 0.0)
    m = fsum * LOG2E + gmax
    row_ref[RS_G:RS_G + 8, :] = gmax
    row_ref[RS_M:RS_M + 8, :] = m
    col_ref[...] = jnp.transpose(jnp.concatenate([beta, jnp.zeros((LANES - 8, S), F32)], axis=0))
    hi, mid, lo = _split3(c * LOG2E)
    ones = jnp.ones((8, S), F32)
    pad = jnp.zeros((LANES - AUG_LANES, S), F32)
    aq_ref[...] = jnp.concatenate([hi, mid, lo, ones, ones, ones, pad], axis=0).astype(BF16)
    ak_ref[...] = jnp.transpose(jnp.concatenate([ones, ones, ones, -hi, -mid, -lo, pad], axis=0)).astype(BF16)


def _gates(zg, bias_col, B, S):
    return pl.pallas_call(
        _gate_kernel,
        grid=(B,),
        in_specs=[
            pl.BlockSpec((GATE_WIDTH, S), lambda b: (0, b)),
            pl.BlockSpec((GATE_WIDTH, 1), lambda b: (0, 0)),
        ],
        out_specs=[
            pl.BlockSpec((None, 16, S), lambda b: (b, 0, 0)),
            pl.BlockSpec((S, LANES), lambda b: (b, 0)),
            pl.BlockSpec((None, LANES, S), lambda b: (b, 0, 0)),
            pl.BlockSpec((S, LANES), lambda b: (b, 0)),
        ],
        out_shape=[
            jax.ShapeDtypeStruct((B, 16, S), F32),
            jax.ShapeDtypeStruct((B * S, LANES), F32),
            jax.ShapeDtypeStruct((B, LANES, S), BF16),
            jax.ShapeDtypeStruct((B * S, LANES), BF16),
        ],
        compiler_params=pltpu.CompilerParams(
            dimension_semantics=("arbitrary",), vmem_limit_bytes=VMEM_LIMIT),
        name="gates",
    )(zg, bias_col)


def _fox_kernel(qt_ref, k_ref, vt_ref, fz_ref, aqt_ref, ak_ref, o_ref, qaug_ref, kaug_ref, vaug_ref):
    h = pl.program_id(1)
    S = k_ref.shape[0]
    d = FOX_HEAD_DIM
    lane = lax.broadcasted_iota(jnp.int32, (1, LANES), 1)
    keep = jnp.where(((lane & (FOX_HEADS - 1)) == h) & (lane < AUG_LANES), 1.0, 0.0).astype(BF16)
    qaug_ref[:d, :] = qt_ref[...]
    qaug_ref[d:, :] = aqt_ref[...]
    kaug_ref[:, :LANES] = k_ref[...]
    kaug_ref[:, LANES:] = ak_ref[...] * keep
    vaug_ref[:d, :] = vt_ref[...]
    vaug_ref[d:, :] = jnp.ones((BF16_ROWS, S), BF16)
    s_idx = lax.broadcasted_iota(jnp.int32, (TQ, TQ), 0)
    t_idx = lax.broadcasted_iota(jnp.int32, (TQ, TQ), 1)
    causal = s_idx <= t_idx

    ng = S // TQ
    m = [None] * ng
    acc = [None] * ng
    tiles = [(kj, g) for kj in range(ng) for g in range(kj, ng)]

    def scores(kj, g):
        ka = kaug_ref[kj * TQ:(kj + 1) * TQ, :]
        qa = qaug_ref[:, g * TQ:(g + 1) * TQ]
        return _dot(ka, qa)

    def finish(kj, g, st):
        va = vaug_ref[:, kj * TQ:(kj + 1) * TQ]
        if kj == g:
            st = jnp.where(causal, st, -jnp.inf)
        cm = jnp.max(st, axis=0, keepdims=True)
        if kj == 0:
            m[g] = cm
            acc[g] = _dot(va, jnp.exp2(st - cm).astype(BF16))
        else:
            m_new = jnp.maximum(m[g], cm)
            alpha = jnp.exp2(m[g] - m_new)
            acc[g] = alpha * acc[g] + _dot(va, jnp.exp2(st - m_new).astype(BF16))
            m[g] = m_new
        if kj == g:
            a = acc[g]
            o = jnp.transpose(a[:d] * (1.0 / a[d:d + 1]))
            fz = fz_ref[g * TQ:(g + 1) * TQ, :].astype(F32)
            o_ref[g * TQ:(g + 1) * TQ, :] = (o * _silu(fz)).astype(BF16)

    pending = [scores(*tiles[i]) for i in range(FOX_LOOKAHEAD)]
    for i, (kj, g) in enumerate(tiles):
        if i + FOX_LOOKAHEAD < len(tiles):
            pending.append(scores(*tiles[i + FOX_LOOKAHEAD]))
        finish(kj, g, pending.pop(0))


def _fox(fm, z, aqt, ak, B, S):
    d = FOX_HEAD_DIM
    return pl.pallas_call(
        _fox_kernel,
        grid=(B, FOX_HEADS),
        in_specs=[
            pl.BlockSpec((d, S), lambda b, h: (FR_FQ // d + h, b)),
            pl.BlockSpec((S, LANES), lambda b, h: (b, ZB_FK + h)),
            pl.BlockSpec((d, S), lambda b, h: (FR_FV // d + h, b)),
            pl.BlockSpec((S, LANES), lambda b, h: (b, ZB_FZ + h)),
            pl.BlockSpec((None, LANES, S), lambda b, h: (b, 0, 0)),
            pl.BlockSpec((S, LANES), lambda b, h: (b, 0)),
        ],
        out_specs=pl.BlockSpec((S, LANES), lambda b, h: (b, h)),
        out_shape=jax.ShapeDtypeStruct((B * S, FOX_WIDTH), BF16),
        scratch_shapes=[
            pltpu.VMEM((2 * LANES, S), BF16),
            pltpu.VMEM((S, 2 * LANES), BF16),
            pltpu.VMEM((d + BF16_ROWS, S), BF16),
        ],
        compiler_params=pltpu.CompilerParams(
            dimension_semantics=("arbitrary", "arbitrary"), vmem_limit_bytes=VMEM_LIMIT),
        name="fox",
    )(fm, z, fm, z, aqt, ak)


def _mlstm_kernel(qt_ref, k_ref, vt_ref, mot_ref, mzt_ref, col_ref, grow_ref, mrow_ref, ng_ref,
                  o_ref, vaug_ref):
    h = pl.program_id(1)
    S = k_ref.shape[0]
    L = ML_L
    nc = S // L
    dv = ML_V_DIM
    head_shift = (LANES - h) % LANES
    vaug_ref[:dv, :] = vt_ref[...]
    vaug_ref[dv:, :] = jnp.ones((BF16_ROWS, S), BF16)
    s_idx = lax.broadcasted_iota(jnp.int32, (L, L), 0)
    t_idx = lax.broadcasted_iota(jnp.int32, (L, L), 1)
    causal = s_idx <= t_idx
    ng = ng_ref[...]

    def scores(c):
        return _dot(k_ref[c * L:(c + 1) * L, :], qt_ref[:, c * L:(c + 1) * L])

    def finish(c, st, state):
        ct_aug, g_prev = state
        cols = slice(c * L, (c + 1) * L)
        b_col = pltpu.roll(col_ref[cols, :], head_shift, 1)[:, CL_BETA:CL_BETA + 1]
        g_row = grow_ref[c:c + 1, :]
        g_last = g_row[:, L - 1:L]
        va = vaug_ref[:, cols]
        kw = k_ref[cols, :].astype(F32) * jnp.exp2(b_col - g_last)
        upd = _dot(va, kw.astype(BF16))
        sw = st * jnp.where(causal, jnp.exp2(b_col - g_row), 0.0)
        num = _dot(va, sw.astype(BF16))
        if c > 0:
            w_inter = jnp.exp2(g_prev - g_row)
            num = num + w_inter * _dot(ct_aug.astype(BF16), qt_ref[:, cols])
        den = num[dv:dv + 1, :]
        num = num[:dv, :]
        inv = 1.0 / jnp.maximum(jnp.abs(den), jnp.exp2(-mrow_ref[c:c + 1, :]))
        ssq = jnp.sum(num * num, axis=0, keepdims=True)
        t_scale = inv * lax.rsqrt(inv * inv * ssq * (1.0 / dv) + EPS)
        mo = mot_ref[:, cols].astype(F32)
        mz = mzt_ref[:, cols].astype(F32)
        gate = mz * (1.0 / ((1.0 + jnp.exp2(mo * -LOG2E)) * (1.0 + jnp.exp2(mz * -LOG2E))))
        o_ref[cols, :] = jnp.transpose(num * t_scale * (ng * gate)).astype(BF16)
        if c == 0:
            return upd, g_last
        return jnp.exp2(g_prev - g_last) * ct_aug + upd, g_last

    pending = [scores(c) for c in range(min(ML_LOOKAHEAD, nc))]
    state = (None, None)
    for c in range(nc):
        if c + ML_LOOKAHEAD < nc:
            pending.append(scores(c + ML_LOOKAHEAD))
        state = finish(c, pending.pop(0), state)


def _mlstm(fm, z, colpack, rowpack4, ng_col, B, S):
    nc = S // ML_L
    dk, dv = ML_QK_DIM, ML_V_DIM
    row_spec = lambda slab: pl.BlockSpec((None, None, nc, ML_L), lambda b, h: (b, slab + h, 0, 0))
    return pl.pallas_call(
        _mlstm_kernel,
        grid=(B, ML_HEADS),
        in_specs=[
            pl.BlockSpec((dk, S), lambda b, h: (FR_MQ // dk + h, b)),
            pl.BlockSpec((S, LANES), lambda b, h: (b, ZB_MK + h)),
            pl.BlockSpec((dv, S), lambda b, h: (FR_MV // dv + h, b)),
            pl.BlockSpec((dv, S), lambda b, h: (FR_MO // dv + h, b)),
            pl.BlockSpec((dv, S), lambda b, h: (FR_MZ // dv + h, b)),
            pl.BlockSpec((S, LANES), lambda b, h: (b, 0)),
            row_spec(RS_G), row_spec(RS_M),
            pl.BlockSpec((dv, 1), lambda b, h: (h, 0)),
        ],
        out_specs=pl.BlockSpec((S, dv), lambda b, h: (b, h)),
        out_shape=jax.ShapeDtypeStruct((B * S, ML_WIDTH), BF16),
        scratch_shapes=[pltpu.VMEM((dv + BF16_ROWS, S), BF16)],
        compiler_params=pltpu.CompilerParams(
            dimension_semantics=("arbitrary", "arbitrary"), vmem_limit_bytes=VMEM_LIMIT),
        name="mlstm",
    )(fm, z, fm, fm, fm, colpack, rowpack4, rowpack4, ng_col)


def _out_kernel(oa_ref, hb_ref, ga_ref, gb_ref, x_ref, wfd_ref, wmd_ref, wo_ref, bga_ref, bgb_ref,
                fg_ref, o_ref):
    def branches(s):
        rows = slice(s * OUT_SUB, (s + 1) * OUT_SUB)
        return _dot(oa_ref[rows, :], wfd_ref[...]), _dot(hb_ref[rows, :], wmd_ref[...])

    def finish(s, ya, yb):
        rows = slice(s * OUT_SUB, (s + 1) * OUT_SUB)
        gate_a = jax.nn.sigmoid(ga_ref[rows, :].astype(F32) + bga_ref[...])
        gate_b = jax.nn.sigmoid(gb_ref[rows, :].astype(F32) + bgb_ref[...])
        y = gate_a * ya + gate_b * yb
        r = x_ref[rows, :] + _dot(y.astype(BF16), wo_ref[...])
        ms = jnp.mean(r * r, axis=-1, keepdims=True)
        o_ref[rows, :] = r * lax.rsqrt(ms + EPS) * fg_ref[...]

    n_sub = TM_OUT // OUT_SUB
    pending = branches(0)
    for s in range(n_sub):
        nxt = branches(s + 1) if s + 1 < n_sub else None
        finish(s, *pending)
        pending = nxt


def _out_stage(oa, hb, z, x2, wfd, wmd, wo, b_gate, final_g):
    T = x2.shape[0]
    row = lambda i: (i, 0)
    const = lambda i: (0, 0)
    return pl.pallas_call(
        _out_kernel,
        grid=(T // TM_OUT,),
        in_specs=[
            pl.BlockSpec((TM_OUT, FOX_WIDTH), row),
            pl.BlockSpec((TM_OUT, ML_WIDTH), row),
            pl.BlockSpec((TM_OUT, D_MODEL), lambda i: (i, ZB_GA // 8)),
            pl.BlockSpec((TM_OUT, D_MODEL), lambda i: (i, ZB_GB // 8)),
            pl.BlockSpec((TM_OUT, D_MODEL), row),
            pl.BlockSpec((FOX_WIDTH, D_MODEL), const, pipeline_mode=pl.Buffered(1)),
            pl.BlockSpec((ML_WIDTH, D_MODEL), const, pipeline_mode=pl.Buffered(1)),
            pl.BlockSpec((D_MODEL, D_MODEL), const, pipeline_mode=pl.Buffered(1)),
            pl.BlockSpec((1, D_MODEL), lambda i: (0, 0)),
            pl.BlockSpec((1, D_MODEL), lambda i: (0, 1)),
            pl.BlockSpec((1, D_MODEL), const),
        ],
        out_specs=pl.BlockSpec((TM_OUT, D_MODEL), row),
        out_shape=jax.ShapeDtypeStruct((T, D_MODEL), F32),
        compiler_params=pltpu.CompilerParams(
            dimension_semantics=("arbitrary",), vmem_limit_bytes=VMEM_LIMIT),
        name="outstage",
    )(oa, hb, z, z, x2, wfd, wmd, wo, b_gate, b_gate, final_g)


def _layer(x2, B, S, norm_g, w_in, b_fox_f, conv_w, conv_b, b_ml_i, b_ml_f, ml_norm_g, b_gate,
           w_fox_down, w_ml_down, w_out, out_g):
    assert w_in.shape == (D_MODEL, IN_WIDTH)
    w_fm, w_tm = _prep_weights(jnp.transpose(w_in))
    zrow = lambda n: jnp.zeros((n,), F32)
    bias_col = jnp.concatenate(
        [b_fox_f, b_ml_i, zrow(8 - ML_HEADS), b_ml_f, zrow(GATE_WIDTH - GL_LF - ML_HEADS)]).reshape(GATE_WIDTH, 1)
    cwq_t = jnp.concatenate(
        [jnp.transpose(conv_w[:, :ML_QK_WIDTH]), conv_b[:ML_QK_WIDTH, None],
         jnp.zeros((ML_QK_WIDTH, LANES - CONV_WIDTH - 1), F32)], axis=1)

    fm, zg, z = _inproj(x2, norm_g.reshape(1, D_MODEL), w_fm, w_tm, cwq_t, conv_w,
                        conv_b.reshape(1, 2 * ML_QK_WIDTH), S)
    rowpack, colpack, aqt, ak = _gates(zg, bias_col, B, S)
    rowpack4 = rowpack.reshape(B, 16, S // ML_L, ML_L)
    oa = _fox(fm, z, aqt, ak, B, S)
    hb = _mlstm(fm, z, colpack, rowpack4, ml_norm_g.reshape(ML_WIDTH, 1), B, S)
    return _out_stage(oa, hb, z, x2, w_fox_down.astype(BF16), w_ml_down.astype(BF16),
                      w_out.astype(BF16), b_gate.reshape(1, 2 * D_MODEL), out_g.reshape(1, D_MODEL))


def kernel(x, norm_g, w_in, b_fox_f, conv_w, conv_b, b_ml_i, b_ml_f, ml_norm_g, b_gate,
           w_fox_down, w_ml_down, w_out, final_g):
    B, S, _ = x.shape
    depth = norm_g.shape[0]
    assert depth == 1, "the final RMSNorm is fused into the (single) layer's output stage"
    out = _layer(x.reshape(B * S, D_MODEL), B, S, norm_g[0], w_in[0], b_fox_f[0], conv_w[0], conv_b[0],
                 b_ml_i[0], b_ml_f[0], ml_norm_g[0], b_gate[0], w_fox_down[0], w_ml_down[0],
                 w_out[0], final_g)
    return out.reshape(B, S, D_MODEL)
```

```python
import functools

import jax
import jax.numpy as jnp
from jax import lax
from jax.experimental import pallas as pl
from jax.experimental.pallas import tpu as pltpu

F32 = jnp.float32
BF16 = jnp.bfloat16

D_MODEL = 1024
FOX_HEADS = 8
FOX_HEAD_DIM = 128
FOX_WIDTH = FOX_HEADS * FOX_HEAD_DIM
ML_HEADS = 4
ML_QK_DIM = 128
ML_V_DIM = 256
ML_QK_WIDTH = ML_HEADS * ML_QK_DIM
ML_WIDTH = ML_HEADS * ML_V_DIM
CONV_WIDTH = 4
EPS = 1e-6

LANES = 128
SUBLANES = 8
BF16_ROWS = 16
GATE_WIDTH = LANES
VMEM_LIMIT = 56 * 1024 * 1024
LOG2E = 1.4426950408889634
FOX_Q_SCALE = FOX_HEAD_DIM ** -0.5 * LOG2E

FR_MQ = 0
FR_FQ = FR_MQ + ML_QK_WIDTH
FR_FV = FR_FQ + FOX_WIDTH
FR_MV = FR_FV + FOX_WIDTH
FR_MO = FR_MV + ML_WIDTH
FR_MZ = FR_MO + ML_WIDTH
FR_GATE = FR_MZ + ML_WIDTH
FM_ROWS = FR_GATE + GATE_WIDTH
FM_CHUNKS = ((FR_MQ, FR_FV), (FR_FV, FR_MV), (FR_MV, FR_MO), (FR_MO, FR_MZ), (FR_MZ, FM_ROWS))
ZB_FK, ZB_FZ, ZB_GA, ZB_GB, ZB_MK = 0, 8, 16, 24, 32
Z_WIDTH = 4608
TN_IN = 1536
TM_IN = 512

GL_FOX, GL_IG, GL_LF = 0, 8, 16
RS_G, RS_M = 0, 8
CL_BETA = 0
AUG_LANES = 6 * FOX_HEADS

GATE_SEQS_PER_STEP = 4
TQ = 256
FOX_HEADS_PER_STEP = 2
FOX_LOOKAHEAD = 8
ML_L = 256
ML_LOOKAHEAD = 2
TM_OUT = 1024
OUT_SUB = 256

NT_DIMS = (((1,), (1,)), ((), ()))


def _dot(a, b):
    return jnp.dot(a, b, preferred_element_type=F32)


O_FQ, O_FK, O_FV = 0, FOX_WIDTH, 2 * FOX_WIDTH
O_FF = 3 * FOX_WIDTH
O_FZ = O_FF + FOX_HEADS
O_MQ = O_FZ + FOX_WIDTH
O_MK = O_MQ + ML_QK_WIDTH
O_MV = O_MK + ML_QK_WIDTH
O_MI = O_MV + ML_WIDTH
O_MF = O_MI + ML_HEADS
O_MO = O_MF + ML_HEADS
O_MZ = O_MO + ML_WIDTH
O_GA = O_MZ + ML_WIDTH
O_GB = O_GA + D_MODEL
IN_WIDTH = O_GB + D_MODEL

PREP_UNIT = 512
FM_GROUPS = ((O_MQ, ML_QK_WIDTH), (O_FQ, FOX_WIDTH), (O_FV, FOX_WIDTH), (O_MV, ML_WIDTH),
             (O_MO, ML_WIDTH), (O_MZ, ML_WIDTH))
TM_GROUPS = ((O_FK, FOX_WIDTH), (O_FZ, FOX_WIDTH), (O_GA, D_MODEL), (O_GB, D_MODEL),
             (O_MK, ML_QK_WIDTH))
KIND_FM, KIND_GATE, KIND_TM = 0, 1, 2


def _prep_units():
    units = [(s, KIND_FM) for off, width in FM_GROUPS for s in range(off, off + width, PREP_UNIT)]
    units.append((0, KIND_GATE))
    units += [(s, KIND_TM) for off, width in TM_GROUPS for s in range(off, off + width, PREP_UNIT)]
    assert all(s % SUBLANES == 0 for s, _ in units)
    return [s // SUBLANES for s, _ in units], [k for _, k in units]


PREP_STARTS, PREP_KINDS = _prep_units()
N_FM_UNITS = PREP_KINDS.index(KIND_GATE) + 1
FM_W_ROWS = N_FM_UNITS * PREP_UNIT


def _prep_kernel(start_ref, kind_ref, w_ref, ff_ref, mimf_ref, fm_ref, tm_ref):
    kind = kind_ref[pl.program_id(0)]

    @pl.when(kind == KIND_FM)
    def _():
        fm_ref[...] = w_ref[...].astype(BF16)

    @pl.when(kind == KIND_TM)
    def _():
        tm_ref[...] = jnp.transpose(w_ref[...]).astype(BF16)

    @pl.when(kind == KIND_GATE)
    def _():
        mimf = mimf_ref[...]
        pad = lambda n: jnp.zeros((n, D_MODEL), F32)
        rows = jnp.concatenate(
            [ff_ref[...], mimf[:ML_HEADS], pad(GL_LF - GL_IG - ML_HEADS), mimf[ML_HEADS:],
             pad(PREP_UNIT - GL_LF - ML_HEADS)], axis=0)
        fm_ref[...] = rows.astype(BF16)


def _prep_weights(w_t):
    assert O_MF == O_MI + ML_HEADS and FOX_HEADS == SUBLANES and 2 * ML_HEADS == SUBLANES
    gate_rows = lambda off: pl.BlockSpec((pl.Element(SUBLANES), pl.Element(D_MODEL)),
                                         lambda u, start, kind: (off, 0))
    grid_spec = pltpu.PrefetchScalarGridSpec(
        num_scalar_prefetch=2,
        grid=(len(PREP_KINDS),),
        in_specs=[
            pl.BlockSpec((pl.Element(PREP_UNIT), pl.Element(D_MODEL)),
                         lambda u, start, kind: (start[u] * SUBLANES, 0)),
            gate_rows(O_FF), gate_rows(O_MI),
        ],
        out_specs=[
            pl.BlockSpec((PREP_UNIT, D_MODEL), lambda u, start, kind: (jnp.minimum(u, N_FM_UNITS - 1), 0)),
            pl.BlockSpec((D_MODEL, PREP_UNIT), lambda u, start, kind: (0, jnp.maximum(u - N_FM_UNITS, 0))),
        ],
    )
    return pl.pallas_call(
        _prep_kernel,
        grid_spec=grid_spec,
        out_shape=[
            jax.ShapeDtypeStruct((FM_W_ROWS, D_MODEL), BF16),
            jax.ShapeDtypeStruct((D_MODEL, Z_WIDTH), BF16),
        ],
        compiler_params=pltpu.CompilerParams(
            dimension_semantics=("arbitrary",), vmem_limit_bytes=VMEM_LIMIT),
        name="prep",
    )(jnp.asarray(PREP_STARTS, jnp.int32), jnp.asarray(PREP_KINDS, jnp.int32), w_t, w_t, w_t)


def _silu(x):
    return x * (1.0 / (1.0 + jnp.exp2(x * -LOG2E)))


def _causal_conv(x, taps, bias, axis, halo):
    def run(xx):
        y = xx * taps[CONV_WIDTH - 1] + bias
        for r in range(1, CONV_WIDTH):
            y = y + pltpu.roll(xx, r, axis) * taps[CONV_WIDTH - 1 - r]
        return y

    edge = halo.shape[axis]
    if axis == 0:
        head = run(jnp.concatenate([halo, x[:edge, :]], axis=0))[edge:, :]
        return jnp.concatenate([head, run(x)[edge:, :]], axis=0)
    head = run(jnp.concatenate([halo, x[:, :edge]], axis=1))[:, edge:]
    return jnp.concatenate([head, run(x)[:, edge:]], axis=1)


def _inproj_kernel(x_ref, g_ref, wfm_ref, wtm_ref, cwq_ref, cwk_ref, cbk_ref,
                   fm_ref, zg_ref, z_ref, qhalo_ref, khalo_ref, *, steps_per_seq):
    @pl.when(pl.program_id(0) % steps_per_seq == 0)
    def _():
        qhalo_ref[...] = jnp.zeros_like(qhalo_ref)
        khalo_ref[...] = jnp.zeros_like(khalo_ref)

    x = x_ref[...]
    ms = jnp.mean(x * x, axis=-1, keepdims=True)
    xh = (x * lax.rsqrt(ms + EPS) * g_ref[...]).astype(BF16)

    def token_major(c0):
        res = _dot(xh, wtm_ref[:, c0:c0 + TN_IN])

        def epilogue():
            if c0 + TN_IN == Z_WIDTH:
                k0 = TN_IN - ML_QK_WIDTH
                xk = res[:, k0:]
                cwk = cwk_ref[...]
                taps = [cwk[j:j + 1, :] for j in range(CONV_WIDTH)]
                yk = _causal_conv(xk, taps, cbk_ref[...], 0, khalo_ref[...])
                khalo_ref[...] = xk[TM_IN - SUBLANES:, :]
                z_ref[:, c0:c0 + k0] = res[:, :k0].astype(BF16)
                z_ref[:, c0 + k0:] = _silu(yk).astype(BF16)
            else:
                z_ref[:, c0:c0 + TN_IN] = res.astype(BF16)
        return epilogue

    def feature_major(r0, r1):
        res = lax.dot_general(wfm_ref[r0:r1, :], xh, NT_DIMS, preferred_element_type=F32)

        def epilogue():
            if r0 == FR_MQ:
                xq = res[:ML_QK_WIDTH, :]
                cwq = cwq_ref[...]
                taps = [cwq[:, j:j + 1] for j in range(CONV_WIDTH)]
                yq = _causal_conv(xq, taps, cwq[:, CONV_WIDTH:CONV_WIDTH + 1], 1, qhalo_ref[...])
                fm_ref[FR_MQ:FR_FQ, :] = (_silu(yq) * (ML_QK_DIM ** -0.5)).astype(BF16)
                qhalo_ref[...] = xq[:, TM_IN - LANES:]
                fm_ref[FR_FQ:FR_FV, :] = (res[ML_QK_WIDTH:, :] * FOX_Q_SCALE).astype(BF16)
            elif r1 == FM_ROWS:
                fm_ref[r0:FR_GATE, :] = res[:FR_GATE - r0, :].astype(BF16)
                zg_ref[...] = res[FR_GATE - r0:, :]
            else:
                fm_ref[r0:r1, :] = res.astype(BF16)
        return epilogue

    chunks = ([functools.partial(token_major, Z_WIDTH - TN_IN)]
              + [functools.partial(feature_major, r0, r1) for r0, r1 in FM_CHUNKS]
              + [functools.partial(token_major, c0) for c0 in range(0, Z_WIDTH - TN_IN, TN_IN)])
    pending = None
    for start in chunks:
        epilogue = start()
        if pending is not None:
            pending()
        pending = epilogue
    pending()


def _inproj(x2, norm_g, w_fm, w_tm, cwq_t, conv_w, conv_b, S):
    T = x2.shape[0]
    assert S % TM_IN == 0 and ZB_MK * LANES + ML_QK_WIDTH == Z_WIDTH
    resident = pl.Buffered(1)
    return pl.pallas_call(
        functools.partial(_inproj_kernel, steps_per_seq=S // TM_IN),
        grid=(T // TM_IN,),
        in_specs=[
            pl.BlockSpec((TM_IN, D_MODEL), lambda i: (i, 0)),
            pl.BlockSpec((1, D_MODEL), lambda i: (0, 0)),
            pl.BlockSpec((FM_W_ROWS, D_MODEL), lambda i: (0, 0), pipeline_mode=resident),
            pl.BlockSpec((D_MODEL, Z_WIDTH), lambda i: (0, 0), pipeline_mode=resident),
            pl.BlockSpec((ML_QK_WIDTH, LANES), lambda i: (0, 0)),
            pl.BlockSpec((CONV_WIDTH, ML_QK_WIDTH), lambda i: (0, 1)),
            pl.BlockSpec((1, ML_QK_WIDTH), lambda i: (0, 1)),
        ],
        out_specs=[
            pl.BlockSpec((FR_GATE, TM_IN), lambda i: (0, i)),
            pl.BlockSpec((GATE_WIDTH, TM_IN), lambda i: (0, i)),
            pl.BlockSpec((TM_IN, Z_WIDTH), lambda i: (i, 0)),
        ],
        out_shape=[
            jax.ShapeDtypeStruct((FR_GATE, T), BF16),
            jax.ShapeDtypeStruct((GATE_WIDTH, T), F32),
            jax.ShapeDtypeStruct((T, Z_WIDTH), BF16),
        ],
        scratch_shapes=[
            pltpu.VMEM((ML_QK_WIDTH, LANES), F32),
            pltpu.VMEM((SUBLANES, ML_QK_WIDTH), F32),
        ],
        compiler_params=pltpu.CompilerParams(
            dimension_semantics=("arbitrary",), vmem_limit_bytes=VMEM_LIMIT),
        name="inproj",
    )(x2, norm_g, w_fm, w_tm, cwq_t, conv_w, conv_b)


def _scan_lanes(x, op, ident):
    n = x.shape[1]
    lane = lax.broadcasted_iota(jnp.int32, x.shape, 1)
    k = 1
    while k < n:
        shifted = pltpu.roll(x, k, 1)
        x = op(x, jnp.where(lane >= k, shifted, ident))
        k *= 2
    return x


def _log_sigmoid(x):
    return jnp.minimum(x, 0.0) - jnp.log1p(jnp.exp(-jnp.abs(x)))


def _split3(x):
    hi = x.astype(BF16).astype(F32)
    r = x - hi
    mid = r.astype(BF16).astype(F32)
    lo = (r - mid).astype(BF16).astype(F32)
    return hi, mid, lo


def _gate_kernel(zg_ref, bias_ref, row_ref, col_ref, aq_ref, ak_ref):
    S = row_ref.shape[2]
    for j in range(row_ref.shape[0]):
        t = zg_ref[:, j * S:(j + 1) * S] + bias_ref[...]
        c = _scan_lanes(_log_sigmoid(t[GL_FOX:GL_FOX + 8]), jnp.add, 0.0)
        ig = t[GL_IG:GL_IG + 8]
        fsum = _scan_lanes(_log_sigmoid(t[GL_LF:GL_LF + 8]), jnp.add, 0.0)
        beta = (ig - fsum) * LOG2E
        gmax = jnp.maximum(_scan_lanes(beta, jnp.maximum, -jnp.inf), 0.0)
        m = fsum * LOG2E + gmax
        row_ref[j, RS_G:RS_G + 8, :] = gmax
        row_ref[j, RS_M:RS_M + 8, :] = m
        col_ref[j * S:(j + 1) * S, :] = jnp.transpose(
            jnp.concatenate([beta, jnp.zeros((LANES - 8, S), F32)], axis=0))
        hi, mid, lo = _split3(c * LOG2E)
        ones = jnp.ones((8, S), F32)
        pad = jnp.zeros((LANES - AUG_LANES, S), F32)
        aq_ref[j] = jnp.concatenate([hi, mid, lo, ones, ones, ones, pad], axis=0).astype(BF16)
        ak_ref[j * S:(j + 1) * S, :] = jnp.transpose(
            jnp.concatenate([ones, ones, ones, -hi, -mid, -lo, pad], axis=0)).astype(BF16)


def _gates(zg, bias_col, B, S):
    nb = GATE_SEQS_PER_STEP
    return pl.pallas_call(
        _gate_kernel,
        grid=(B // nb,),
        in_specs=[
            pl.BlockSpec((GATE_WIDTH, nb * S), lambda b: (0, b)),
            pl.BlockSpec((GATE_WIDTH, 1), lambda b: (0, 0)),
        ],
        out_specs=[
            pl.BlockSpec((nb, 16, S), lambda b: (b, 0, 0)),
            pl.BlockSpec((nb * S, LANES), lambda b: (b, 0)),
            pl.BlockSpec((nb, LANES, S), lambda b: (b, 0, 0)),
            pl.BlockSpec((nb * S, LANES), lambda b: (b, 0)),
        ],
        out_shape=[
            jax.ShapeDtypeStruct((B, 16, S), F32),
            jax.ShapeDtypeStruct((B * S, LANES), F32),
            jax.ShapeDtypeStruct((B, LANES, S), BF16),
            jax.ShapeDtypeStruct((B * S, LANES), BF16),
        ],
        compiler_params=pltpu.CompilerParams(
            dimension_semantics=("arbitrary",), vmem_limit_bytes=VMEM_LIMIT),
        name="gates",
    )(zg, bias_col)


def _fox_kernel(qt_ref, k_ref, vt_ref, fz_ref, aqt_ref, ak_ref, o_ref, qaug_ref, kaug_ref, vaug_ref):
    S = k_ref.shape[0]
    d = FOX_HEAD_DIM
    hs = range(FOX_HEADS_PER_STEP)
    lane = lax.broadcasted_iota(jnp.int32, (1, LANES), 1)
    for j in hs:
        h = pl.program_id(1) * FOX_HEADS_PER_STEP + j
        keep = jnp.where(((lane & (FOX_HEADS - 1)) == h) & (lane < AUG_LANES), 1.0, 0.0).astype(BF16)
        qaug_ref[j, :d, :] = qt_ref[j * d:(j + 1) * d, :]
        qaug_ref[j, d:, :] = aqt_ref[...]
        kaug_ref[j, :, :LANES] = k_ref[:, j * LANES:(j + 1) * LANES]
        kaug_ref[j, :, LANES:] = ak_ref[...] * keep
        vaug_ref[j, :d, :] = vt_ref[j * d:(j + 1) * d, :]
        vaug_ref[j, d:, :] = jnp.ones((BF16_ROWS, S), BF16)
    s_idx = lax.broadcasted_iota(jnp.int32, (TQ, TQ), 0)
    t_idx = lax.broadcasted_iota(jnp.int32, (TQ, TQ), 1)
    causal = s_idx <= t_idx

    ng = S // TQ
    m = [[None] * ng for _ in hs]
    acc = [[None] * ng for _ in hs]
    tiles = [(j, kj, g) for kj in range(ng) for g in range(kj, ng) for j in hs]

    def scores(j, kj, g):
        ka = kaug_ref[j, kj * TQ:(kj + 1) * TQ, :]
        qa = qaug_ref[j, :, g * TQ:(g + 1) * TQ]
        return _dot(ka, qa)

    def finish(j, kj, g, st):
        va = vaug_ref[j, :, kj * TQ:(kj + 1) * TQ]
        if kj == g:
            st = jnp.where(causal, st, -jnp.inf)
        cm = jnp.max(st, axis=0, keepdims=True)
        if kj == 0:
            m[j][g] = cm
            acc[j][g] = _dot(va, jnp.exp2(st - cm).astype(BF16))
        else:
            m_new = jnp.maximum(m[j][g], cm)
            alpha = jnp.exp2(m[j][g] - m_new)
            acc[j][g] = alpha * acc[j][g] + _dot(va, jnp.exp2(st - m_new).astype(BF16))
            m[j][g] = m_new
        if kj == g:
            a = acc[j][g]
            o = jnp.transpose(a[:d] * (1.0 / a[d:d + 1]))
            fz = fz_ref[g * TQ:(g + 1) * TQ, j * LANES:(j + 1) * LANES].astype(F32)
            o_ref[g * TQ:(g + 1) * TQ, j * LANES:(j + 1) * LANES] = (o * _silu(fz)).astype(BF16)

    pending = [scores(*tiles[i]) for i in range(FOX_LOOKAHEAD)]
    for i, tile in enumerate(tiles):
        if i + FOX_LOOKAHEAD < len(tiles):
            pending.append(scores(*tiles[i + FOX_LOOKAHEAD]))
        finish(*tile, pending.pop(0))


def _fox(fm, z, aqt, ak, B, S):
    n = FOX_HEADS_PER_STEP
    dn = n * FOX_HEAD_DIM
    return pl.pallas_call(
        _fox_kernel,
        grid=(B, FOX_HEADS // n),
        in_specs=[
            pl.BlockSpec((dn, S), lambda b, h: (FR_FQ // dn + h, b)),
            pl.BlockSpec((S, dn), lambda b, h: (b, ZB_FK // n + h)),
            pl.BlockSpec((dn, S), lambda b, h: (FR_FV // dn + h, b)),
            pl.BlockSpec((S, dn), lambda b, h: (b, ZB_FZ // n + h)),
            pl.BlockSpec((None, LANES, S), lambda b, h: (b, 0, 0)),
            pl.BlockSpec((S, LANES), lambda b, h: (b, 0)),
        ],
        out_specs=pl.BlockSpec((S, dn), lambda b, h: (b, h)),
        out_shape=jax.ShapeDtypeStruct((B * S, FOX_WIDTH), BF16),
        scratch_shapes=[
            pltpu.VMEM((n, 2 * LANES, S), BF16),
            pltpu.VMEM((n, S, 2 * LANES), BF16),
            pltpu.VMEM((n, FOX_HEAD_DIM + BF16_ROWS, S), BF16),
        ],
        compiler_params=pltpu.CompilerParams(
            dimension_semantics=("arbitrary", "arbitrary"), vmem_limit_bytes=VMEM_LIMIT),
        name="fox",
    )(fm, z, fm, z, aqt, ak)


def _mlstm_kernel(qt_ref, k_ref, vt_ref, mot_ref, mzt_ref, col_ref, grow_ref, mrow_ref, ng_ref,
                  o_ref, vaug_ref):
    h = pl.program_id(1)
    S = k_ref.shape[0]
    L = ML_L
    nc = S // L
    dv = ML_V_DIM
    head_shift = (LANES - h) % LANES
    vaug_ref[:dv, :] = vt_ref[...]
    vaug_ref[dv:, :] = jnp.ones((BF16_ROWS, S), BF16)
    s_idx = lax.broadcasted_iota(jnp.int32, (L, L), 0)
    t_idx = lax.broadcasted_iota(jnp.int32, (L, L), 1)
    causal = s_idx <= t_idx
    ng = ng_ref[...]

    def scores(c):
        return _dot(k_ref[c * L:(c + 1) * L, :], qt_ref[:, c * L:(c + 1) * L])

    def finish(c, st, state):
        ct_aug, g_prev = state
        cols = slice(c * L, (c + 1) * L)
        b_col = pltpu.roll(col_ref[cols, :], head_shift, 1)[:, CL_BETA:CL_BETA + 1]
        g_row = grow_ref[c:c + 1, :]
        g_last = g_row[:, L - 1:L]
        va = vaug_ref[:, cols]
        kw = k_ref[cols, :].astype(F32) * jnp.exp2(b_col - g_last)
        upd = _dot(va, kw.astype(BF16))
        sw = st * jnp.where(causal, jnp.exp2(b_col - g_row), 0.0)
        num = _dot(va, sw.astype(BF16))
        if c > 0:
            w_inter = jnp.exp2(g_prev - g_row)
            num = num + w_inter * _dot(ct_aug.astype(BF16), qt_ref[:, cols])
        den = num[dv:dv + 1, :]
        num = num[:dv, :]
        inv = 1.0 / jnp.maximum(jnp.abs(den), jnp.exp2(-mrow_ref[c:c + 1, :]))
        ssq = jnp.sum(num * num, axis=0, keepdims=True)
        t_scale = inv * lax.rsqrt(inv * inv * ssq * (1.0 / dv) + EPS)
        mo = mot_ref[:, cols].astype(F32)
        mz = mzt_ref[:, cols].astype(F32)
        gate = mz * (1.0 / ((1.0 + jnp.exp2(mo * -LOG2E)) * (1.0 + jnp.exp2(mz * -LOG2E))))
        o_ref[cols, :] = jnp.transpose(num * t_scale * (ng * gate)).astype(BF16)
        if c == 0:
            return upd, g_last
        return jnp.exp2(g_prev - g_last) * ct_aug + upd, g_last

    pending = [scores(c) for c in range(min(ML_LOOKAHEAD, nc))]
    state = (None, None)
    for c in range(nc):
        if c + ML_LOOKAHEAD < nc:
            pending.append(scores(c + ML_LOOKAHEAD))
        state = finish(c, pending.pop(0), state)


def _mlstm(fm, z, colpack, rowpack4, ng_col, B, S):
    nc = S // ML_L
    dk, dv = ML_QK_DIM, ML_V_DIM
    row_spec = lambda slab: pl.BlockSpec((None, None, nc, ML_L), lambda b, h: (b, slab + h, 0, 0))
    return pl.pallas_call(
        _mlstm_kernel,
        grid=(B, ML_HEADS),
        in_specs=[
            pl.BlockSpec((dk, S), lambda b, h: (FR_MQ // dk + h, b)),
            pl.BlockSpec((S, LANES), lambda b, h: (b, ZB_MK + h)),
            pl.BlockSpec((dv, S), lambda b, h: (FR_MV // dv + h, b)),
            pl.BlockSpec((dv, S), lambda b, h: (FR_MO // dv + h, b)),
            pl.BlockSpec((dv, S), lambda b, h: (FR_MZ // dv + h, b)),
            pl.BlockSpec((S, LANES), lambda b, h: (b, 0)),
            row_spec(RS_G), row_spec(RS_M),
            pl.BlockSpec((dv, 1), lambda b, h: (h, 0)),
        ],
        out_specs=pl.BlockSpec((S, dv), lambda b, h: (b, h)),
        out_shape=jax.ShapeDtypeStruct((B * S, ML_WIDTH), BF16),
        scratch_shapes=[pltpu.VMEM((dv + BF16_ROWS, S), BF16)],
        compiler_params=pltpu.CompilerParams(
            dimension_semantics=("arbitrary", "arbitrary"), vmem_limit_bytes=VMEM_LIMIT),
        name="mlstm",
    )(fm, z, fm, fm, fm, colpack, rowpack4, rowpack4, ng_col)


def _out_kernel(oa_ref, hb_ref, ga_ref, gb_ref, x_ref, wfd_ref, wmd_ref, wo_ref, bga_ref, bgb_ref,
                fg_ref, o_ref):
    def branches(s):
        rows = slice(s * OUT_SUB, (s + 1) * OUT_SUB)
        return _dot(oa_ref[rows, :], wfd_ref[...]), _dot(hb_ref[rows, :], wmd_ref[...])

    def finish(s, ya, yb):
        rows = slice(s * OUT_SUB, (s + 1) * OUT_SUB)
        gate_a = jax.nn.sigmoid(ga_ref[rows, :].astype(F32) + bga_ref[...])
        gate_b = jax.nn.sigmoid(gb_ref[rows, :].astype(F32) + bgb_ref[...])
        y = gate_a * ya + gate_b * yb
        r = x_ref[rows, :] + _dot(y.astype(BF16), wo_ref[...])
        ms = jnp.mean(r * r, axis=-1, keepdims=True)
        o_ref[rows, :] = r * lax.rsqrt(ms + EPS) * fg_ref[...]

    n_sub = TM_OUT // OUT_SUB
    pending = branches(0)
    for s in range(n_sub):
        nxt = branches(s + 1) if s + 1 < n_sub else None
        finish(s, *pending)
        pending = nxt


def _out_stage(oa, hb, z, x2, wfd, wmd, wo, b_gate, final_g):
    T = x2.shape[0]
    row = lambda i: (i, 0)
    const = lambda i: (0, 0)
    return pl.pallas_call(
        _out_kernel,
        grid=(T // TM_OUT,),
        in_specs=[
            pl.BlockSpec((TM_OUT, FOX_WIDTH), row),
            pl.BlockSpec((TM_OUT, ML_WIDTH), row),
            pl.BlockSpec((TM_OUT, D_MODEL), lambda i: (i, ZB_GA // 8)),
            pl.BlockSpec((TM_OUT, D_MODEL), lambda i: (i, ZB_GB // 8)),
            pl.BlockSpec((TM_OUT, D_MODEL), row),
            pl.BlockSpec((FOX_WIDTH, D_MODEL), const, pipeline_mode=pl.Buffered(1)),
            pl.BlockSpec((ML_WIDTH, D_MODEL), const, pipeline_mode=pl.Buffered(1)),
            pl.BlockSpec((D_MODEL, D_MODEL), const, pipeline_mode=pl.Buffered(1)),
            pl.BlockSpec((1, D_MODEL), lambda i: (0, 0)),
            pl.BlockSpec((1, D_MODEL), lambda i: (0, 1)),
            pl.BlockSpec((1, D_MODEL), const),
        ],
        out_specs=pl.BlockSpec((TM_OUT, D_MODEL), row),
        out_shape=jax.ShapeDtypeStruct((T, D_MODEL), F32),
        compiler_params=pltpu.CompilerParams(
            dimension_semantics=("arbitrary",), vmem_limit_bytes=VMEM_LIMIT),
        name="outstage",
    )(oa, hb, z, z, x2, wfd, wmd, wo, b_gate, b_gate, final_g)


def _layer(x2, B, S, norm_g, w_in, b_fox_f, conv_w, conv_b, b_ml_i, b_ml_f, ml_norm_g, b_gate,
           w_fox_down, w_ml_down, w_out, out_g):
    assert w_in.shape == (D_MODEL, IN_WIDTH)
    w_fm, w_tm = _prep_weights(jnp.transpose(w_in))
    zrow = lambda n: jnp.zeros((n,), F32)
    bias_col = jnp.concatenate(
        [b_fox_f, b_ml_i, zrow(8 - ML_HEADS), b_ml_f, zrow(GATE_WIDTH - GL_LF - ML_HEADS)]).reshape(GATE_WIDTH, 1)
    cwq_t = jnp.concatenate(
        [jnp.transpose(conv_w[:, :ML_QK_WIDTH]), conv_b[:ML_QK_WIDTH, None],
         jnp.zeros((ML_QK_WIDTH, LANES - CONV_WIDTH - 1), F32)], axis=1)

    fm, zg, z = _inproj(x2, norm_g.reshape(1, D_MODEL), w_fm, w_tm, cwq_t, conv_w,
                        conv_b.reshape(1, 2 * ML_QK_WIDTH), S)
    rowpack, colpack, aqt, ak = _gates(zg, bias_col, B, S)
    rowpack4 = rowpack.reshape(B, 16, S // ML_L, ML_L)
    oa = _fox(fm, z, aqt, ak, B, S)
    hb = _mlstm(fm, z, colpack, rowpack4, ml_norm_g.reshape(ML_WIDTH, 1), B, S)
    return _out_stage(oa, hb, z, x2, w_fox_down.astype(BF16), w_ml_down.astype(BF16),
                      w_out.astype(BF16), b_gate.reshape(1, 2 * D_MODEL), out_g.reshape(1, D_MODEL))


def kernel(x, norm_g, w_in, b_fox_f, conv_w, conv_b, b_ml_i, b_ml_f, ml_norm_g, b_gate,
           w_fox_down, w_ml_down, w_out, final_g):
    B, S, _ = x.shape
    depth = norm_g.shape[0]
    assert depth == 1, "the final RMSNorm is fused into the (single) layer's output stage"
    out = _layer(x.reshape(B * S, D_MODEL), B, S, norm_g[0], w_in[0], b_fox_f[0], conv_w[0], conv_b[0],
                 b_ml_i[0], b_ml_f[0], ml_norm_g[0], b_gate[0], w_fox_down[0], w_ml_down[0],
                 w_out[0], final_g)
    return out.reshape(B, S, D_MODEL)
```

```python
import functools

import jax
import jax.numpy as jnp
from jax import lax
from jax.experimental import pallas as pl
from jax.experimental.pallas import tpu as pltpu

F32 = jnp.float32
BF16 = jnp.bfloat16

D_MODEL = 1024
FOX_HEADS = 8
FOX_HEAD_DIM = 128
FOX_WIDTH = FOX_HEADS * FOX_HEAD_DIM
ML_HEADS = 4
ML_QK_DIM = 128
ML_V_DIM = 256
ML_QK_WIDTH = ML_HEADS * ML_QK_DIM
ML_WIDTH = ML_HEADS * ML_V_DIM
CONV_WIDTH = 4
EPS = 1e-6

LANES = 128
SUBLANES = 8
BF16_ROWS = 16
GATE_WIDTH = LANES
VMEM_LIMIT = 56 * 1024 * 1024
LOG2E = 1.4426950408889634
FOX_Q_SCALE = FOX_HEAD_DIM ** -0.5 * LOG2E

FR_MQ = 0
FR_FQ = FR_MQ + ML_QK_WIDTH
FR_FV = FR_FQ + FOX_WIDTH
FR_MV = FR_FV + FOX_WIDTH
FR_MO = FR_MV + ML_WIDTH
FR_MZ = FR_MO + ML_WIDTH
FR_GATE = FR_MZ + ML_WIDTH
FM_ROWS = FR_GATE + GATE_WIDTH
FM_CHUNKS = ((FR_MQ, FR_FV), (FR_FV, FR_MV), (FR_MV, FR_MO), (FR_MO, FR_MZ), (FR_MZ, FM_ROWS))
ZB_FK, ZB_FZ, ZB_GA, ZB_GB, ZB_MK = 0, 8, 16, 24, 32
Z_WIDTH = 4608
TN_IN = 1536
TM_IN = 512

GL_FOX, GL_IG, GL_LF = 0, 8, 16
RS_G, RS_M = 0, 8
CL_BETA = 0
AUG_LANES = 6 * FOX_HEADS

GATE_SEQS_PER_STEP = 4
TQ = 256
FOX_HEADS_PER_STEP = 2
FOX_LOOKAHEAD = 4
ML_L = 256
ML_HEADS_PER_STEP = 2
ML_LOOKAHEAD = 2
TM_OUT = 1024
OUT_SUB = 256

NT_DIMS = (((1,), (1,)), ((), ()))


def _dot(a, b):
    return jnp.dot(a, b, preferred_element_type=F32)


O_FQ, O_FK, O_FV = 0, FOX_WIDTH, 2 * FOX_WIDTH
O_FF = 3 * FOX_WIDTH
O_FZ = O_FF + FOX_HEADS
O_MQ = O_FZ + FOX_WIDTH
O_MK = O_MQ + ML_QK_WIDTH
O_MV = O_MK + ML_QK_WIDTH
O_MI = O_MV + ML_WIDTH
O_MF = O_MI + ML_HEADS
O_MO = O_MF + ML_HEADS
O_MZ = O_MO + ML_WIDTH
O_GA = O_MZ + ML_WIDTH
O_GB = O_GA + D_MODEL
IN_WIDTH = O_GB + D_MODEL

PREP_UNIT = 512
FM_GROUPS = ((O_MQ, ML_QK_WIDTH), (O_FQ, FOX_WIDTH), (O_FV, FOX_WIDTH), (O_MV, ML_WIDTH),
             (O_MO, ML_WIDTH), (O_MZ, ML_WIDTH))
TM_GROUPS = ((O_FK, FOX_WIDTH), (O_FZ, FOX_WIDTH), (O_GA, D_MODEL), (O_GB, D_MODEL),
             (O_MK, ML_QK_WIDTH))
KIND_FM, KIND_GATE, KIND_TM = 0, 1, 2


def _prep_units():
    units = [(s, KIND_FM) for off, width in FM_GROUPS for s in range(off, off + width, PREP_UNIT)]
    units.append((0, KIND_GATE))
    units += [(s, KIND_TM) for off, width in TM_GROUPS for s in range(off, off + width, PREP_UNIT)]
    assert all(s % SUBLANES == 0 for s, _ in units)
    return [s // SUBLANES for s, _ in units], [k for _, k in units]


PREP_STARTS, PREP_KINDS = _prep_units()
N_FM_UNITS = PREP_KINDS.index(KIND_GATE) + 1
FM_W_ROWS = N_FM_UNITS * PREP_UNIT


def _prep_kernel(start_ref, kind_ref, w_ref, ff_ref, mimf_ref, fm_ref, tm_ref):
    kind = kind_ref[pl.program_id(0)]

    @pl.when(kind == KIND_FM)
    def _():
        fm_ref[...] = w_ref[...].astype(BF16)

    @pl.when(kind == KIND_TM)
    def _():
        tm_ref[...] = jnp.transpose(w_ref[...]).astype(BF16)

    @pl.when(kind == KIND_GATE)
    def _():
        mimf = mimf_ref[...]
        pad = lambda n: jnp.zeros((n, D_MODEL), F32)
        rows = jnp.concatenate(
            [ff_ref[...], mimf[:ML_HEADS], pad(GL_LF - GL_IG - ML_HEADS), mimf[ML_HEADS:],
             pad(PREP_UNIT - GL_LF - ML_HEADS)], axis=0)
        fm_ref[...] = rows.astype(BF16)


def _prep_weights(w_t):
    assert O_MF == O_MI + ML_HEADS and FOX_HEADS == SUBLANES and 2 * ML_HEADS == SUBLANES
    gate_rows = lambda off: pl.BlockSpec((pl.Element(SUBLANES), pl.Element(D_MODEL)),
                                         lambda u, start, kind: (off, 0))
    grid_spec = pltpu.PrefetchScalarGridSpec(
        num_scalar_prefetch=2,
        grid=(len(PREP_KINDS),),
        in_specs=[
            pl.BlockSpec((pl.Element(PREP_UNIT), pl.Element(D_MODEL)),
                         lambda u, start, kind: (start[u] * SUBLANES, 0)),
            gate_rows(O_FF), gate_rows(O_MI),
        ],
        out_specs=[
            pl.BlockSpec((PREP_UNIT, D_MODEL), lambda u, start, kind: (jnp.minimum(u, N_FM_UNITS - 1), 0)),
            pl.BlockSpec((D_MODEL, PREP_UNIT), lambda u, start, kind: (0, jnp.maximum(u - N_FM_UNITS, 0))),
        ],
    )
    return pl.pallas_call(
        _prep_kernel,
        grid_spec=grid_spec,
        out_shape=[
            jax.ShapeDtypeStruct((FM_W_ROWS, D_MODEL), BF16),
            jax.ShapeDtypeStruct((D_MODEL, Z_WIDTH), BF16),
        ],
        compiler_params=pltpu.CompilerParams(
            dimension_semantics=("arbitrary",), vmem_limit_bytes=VMEM_LIMIT),
        name="prep",
    )(jnp.asarray(PREP_STARTS, jnp.int32), jnp.asarray(PREP_KINDS, jnp.int32), w_t, w_t, w_t)


def _silu(x):
    return x * (1.0 / (1.0 + jnp.exp2(x * -LOG2E)))


def _causal_conv(x, taps, bias, axis, halo):
    def run(xx):
        y = xx * taps[CONV_WIDTH - 1] + bias
        for r in range(1, CONV_WIDTH):
            y = y + pltpu.roll(xx, r, axis) * taps[CONV_WIDTH - 1 - r]
        return y

    edge = halo.shape[axis]
    if axis == 0:
        head = run(jnp.concatenate([halo, x[:edge, :]], axis=0))[edge:, :]
        return jnp.concatenate([head, run(x)[edge:, :]], axis=0)
    head = run(jnp.concatenate([halo, x[:, :edge]], axis=1))[:, edge:]
    return jnp.concatenate([head, run(x)[:, edge:]], axis=1)


def _inproj_kernel(x_ref, g_ref, wfm_ref, wtm_ref, cwq_ref, cwk_ref, cbk_ref,
                   fm_ref, zg_ref, z_ref, qhalo_ref, khalo_ref, *, steps_per_seq):
    @pl.when(pl.program_id(0) % steps_per_seq == 0)
    def _():
        qhalo_ref[...] = jnp.zeros_like(qhalo_ref)
        khalo_ref[...] = jnp.zeros_like(khalo_ref)

    x = x_ref[...]
    ms = jnp.mean(x * x, axis=-1, keepdims=True)
    xh = (x * lax.rsqrt(ms + EPS) * g_ref[...]).astype(BF16)

    def token_major(c0):
        res = _dot(xh, wtm_ref[:, c0:c0 + TN_IN])

        def epilogue():
            if c0 + TN_IN == Z_WIDTH:
                k0 = TN_IN - ML_QK_WIDTH
                xk = res[:, k0:]
                cwk = cwk_ref[...]
                taps = [cwk[j:j + 1, :] for j in range(CONV_WIDTH)]
                yk = _causal_conv(xk, taps, cbk_ref[...], 0, khalo_ref[...])
                khalo_ref[...] = xk[TM_IN - SUBLANES:, :]
                z_ref[:, c0:c0 + k0] = res[:, :k0].astype(BF16)
                z_ref[:, c0 + k0:] = _silu(yk).astype(BF16)
            else:
                z_ref[:, c0:c0 + TN_IN] = res.astype(BF16)
        return epilogue

    def feature_major(r0, r1):
        res = lax.dot_general(wfm_ref[r0:r1, :], xh, NT_DIMS, preferred_element_type=F32)

        def epilogue():
            if r0 == FR_MQ:
                xq = res[:ML_QK_WIDTH, :]
                cwq = cwq_ref[...]
                taps = [cwq[:, j:j + 1] for j in range(CONV_WIDTH)]
                yq = _causal_conv(xq, taps, cwq[:, CONV_WIDTH:CONV_WIDTH + 1], 1, qhalo_ref[...])
                fm_ref[FR_MQ:FR_FQ, :] = (_silu(yq) * (ML_QK_DIM ** -0.5)).astype(BF16)
                qhalo_ref[...] = xq[:, TM_IN - LANES:]
                fm_ref[FR_FQ:FR_FV, :] = (res[ML_QK_WIDTH:, :] * FOX_Q_SCALE).astype(BF16)
            elif r1 == FM_ROWS:
                fm_ref[r0:FR_GATE, :] = res[:FR_GATE - r0, :].astype(BF16)
                zg_ref[...] = res[FR_GATE - r0:, :]
            else:
                fm_ref[r0:r1, :] = res.astype(BF16)
        return epilogue

    chunks = ([functools.partial(token_major, Z_WIDTH - TN_IN)]
              + [functools.partial(feature_major, r0, r1) for r0, r1 in FM_CHUNKS]
              + [functools.partial(token_major, c0) for c0 in range(0, Z_WIDTH - TN_IN, TN_IN)])
    pending = None
    for start in chunks:
        epilogue = start()
        if pending is not None:
            pending()
        pending = epilogue
    pending()


def _inproj(x2, norm_g, w_fm, w_tm, cwq_t, conv_w, conv_b, S):
    T = x2.shape[0]
    assert S % TM_IN == 0 and ZB_MK * LANES + ML_QK_WIDTH == Z_WIDTH
    resident = pl.Buffered(1)
    return pl.pallas_call(
        functools.partial(_inproj_kernel, steps_per_seq=S // TM_IN),
        grid=(T // TM_IN,),
        in_specs=[
            pl.BlockSpec((TM_IN, D_MODEL), lambda i: (i, 0)),
            pl.BlockSpec((1, D_MODEL), lambda i: (0, 0)),
            pl.BlockSpec((FM_W_ROWS, D_MODEL), lambda i: (0, 0), pipeline_mode=resident),
            pl.BlockSpec((D_MODEL, Z_WIDTH), lambda i: (0, 0), pipeline_mode=resident),
            pl.BlockSpec((ML_QK_WIDTH, LANES), lambda i: (0, 0)),
            pl.BlockSpec((CONV_WIDTH, ML_QK_WIDTH), lambda i: (0, 1)),
            pl.BlockSpec((1, ML_QK_WIDTH), lambda i: (0, 1)),
        ],
        out_specs=[
            pl.BlockSpec((FR_GATE, TM_IN), lambda i: (0, i)),
            pl.BlockSpec((GATE_WIDTH, TM_IN), lambda i: (0, i)),
            pl.BlockSpec((TM_IN, Z_WIDTH), lambda i: (i, 0)),
        ],
        out_shape=[
            jax.ShapeDtypeStruct((FR_GATE, T), BF16),
            jax.ShapeDtypeStruct((GATE_WIDTH, T), F32),
            jax.ShapeDtypeStruct((T, Z_WIDTH), BF16),
        ],
        scratch_shapes=[
            pltpu.VMEM((ML_QK_WIDTH, LANES), F32),
            pltpu.VMEM((SUBLANES, ML_QK_WIDTH), F32),
        ],
        compiler_params=pltpu.CompilerParams(
            dimension_semantics=("arbitrary",), vmem_limit_bytes=VMEM_LIMIT),
        name="inproj",
    )(x2, norm_g, w_fm, w_tm, cwq_t, conv_w, conv_b)


def _scan_lanes(x, op, ident):
    n = x.shape[1]
    lane = lax.broadcasted_iota(jnp.int32, x.shape, 1)
    k = 1
    while k < n:
        shifted = pltpu.roll(x, k, 1)
        x = op(x, jnp.where(lane >= k, shifted, ident))
        k *= 2
    return x


def _log_sigmoid(x):
    return jnp.minimum(x, 0.0) - jnp.log1p(jnp.exp(-jnp.abs(x)))


def _split3(x):
    hi = x.astype(BF16).astype(F32)
    r = x - hi
    mid = r.astype(BF16).astype(F32)
    lo = (r - mid).astype(BF16).astype(F32)
    return hi, mid, lo


def _gate_kernel(zg_ref, bias_ref, row_ref, col_ref, aq_ref, ak_ref):
    S = row_ref.shape[2]
    for j in range(row_ref.shape[0]):
        t = zg_ref[:, j * S:(j + 1) * S] + bias_ref[...]
        c = _scan_lanes(_log_sigmoid(t[GL_FOX:GL_FOX + 8]), jnp.add, 0.0)
        ig = t[GL_IG:GL_IG + 8]
        fsum = _scan_lanes(_log_sigmoid(t[GL_LF:GL_LF + 8]), jnp.add, 0.0)
        beta = (ig - fsum) * LOG2E
        gmax = jnp.maximum(_scan_lanes(beta, jnp.maximum, -jnp.inf), 0.0)
        m = fsum * LOG2E + gmax
        row_ref[j, RS_G:RS_G + 8, :] = gmax
        row_ref[j, RS_M:RS_M + 8, :] = m
        col_ref[j * S:(j + 1) * S, :] = jnp.transpose(
            jnp.concatenate([beta, jnp.zeros((LANES - 8, S), F32)], axis=0))
        hi, mid, lo = _split3(c * LOG2E)
        ones = jnp.ones((8, S), F32)
        pad = jnp.zeros((LANES - AUG_LANES, S), F32)
        aq_ref[j] = jnp.concatenate([hi, mid, lo, ones, ones, ones, pad], axis=0).astype(BF16)
        ak_ref[j * S:(j + 1) * S, :] = jnp.transpose(
            jnp.concatenate([ones, ones, ones, -hi, -mid, -lo, pad], axis=0)).astype(BF16)


def _gates(zg, bias_col, B, S):
    nb = GATE_SEQS_PER_STEP
    return pl.pallas_call(
        _gate_kernel,
        grid=(B // nb,),
        in_specs=[
            pl.BlockSpec((GATE_WIDTH, nb * S), lambda b: (0, b)),
            pl.BlockSpec((GATE_WIDTH, 1), lambda b: (0, 0)),
        ],
        out_specs=[
            pl.BlockSpec((nb, 16, S), lambda b: (b, 0, 0)),
            pl.BlockSpec((nb * S, LANES), lambda b: (b, 0)),
            pl.BlockSpec((nb, LANES, S), lambda b: (b, 0, 0)),
            pl.BlockSpec((nb * S, LANES), lambda b: (b, 0)),
        ],
        out_shape=[
            jax.ShapeDtypeStruct((B, 16, S), F32),
            jax.ShapeDtypeStruct((B * S, LANES), F32),
            jax.ShapeDtypeStruct((B, LANES, S), BF16),
            jax.ShapeDtypeStruct((B * S, LANES), BF16),
        ],
        compiler_params=pltpu.CompilerParams(
            dimension_semantics=("arbitrary",), vmem_limit_bytes=VMEM_LIMIT),
        name="gates",
    )(zg, bias_col)


def _fox_kernel(qt_ref, k_ref, vt_ref, fz_ref, aqt_ref, ak_ref, o_ref, qaug_ref, kaug_ref, vaug_ref):
    S = k_ref.shape[0]
    d = FOX_HEAD_DIM
    hs = range(FOX_HEADS_PER_STEP)
    lane = lax.broadcasted_iota(jnp.int32, (1, LANES), 1)
    for j in hs:
        h = pl.program_id(1) * FOX_HEADS_PER_STEP + j
        keep = jnp.where(((lane & (FOX_HEADS - 1)) == h) & (lane < AUG_LANES), 1.0, 0.0).astype(BF16)
        qaug_ref[j, :d, :] = qt_ref[j * d:(j + 1) * d, :]
        qaug_ref[j, d:, :] = aqt_ref[...]
        kaug_ref[j, :, :LANES] = k_ref[:, j * LANES:(j + 1) * LANES]
        kaug_ref[j, :, LANES:] = ak_ref[...] * keep
        vaug_ref[j, :d, :] = vt_ref[j * d:(j + 1) * d, :]
        vaug_ref[j, d:, :] = jnp.ones((BF16_ROWS, S), BF16)
    s_idx = lax.broadcasted_iota(jnp.int32, (TQ, TQ), 0)
    t_idx = lax.broadcasted_iota(jnp.int32, (TQ, TQ), 1)
    causal = s_idx <= t_idx

    ng = S // TQ
    m = [[None] * ng for _ in hs]
    acc = [[None] * ng for _ in hs]
    tiles = [(j, kj, g) for kj in range(ng) for g in range(kj, ng) for j in hs]

    def scores(j, kj, g):
        ka = kaug_ref[j, kj * TQ:(kj + 1) * TQ, :]
        qa = qaug_ref[j, :, g * TQ:(g + 1) * TQ]
        return _dot(ka, qa)

    def finish(j, kj, g, st):
        va = vaug_ref[j, :, kj * TQ:(kj + 1) * TQ]
        if kj == g:
            st = jnp.where(causal, st, -jnp.inf)
        cm = jnp.max(st, axis=0, keepdims=True)
        if kj == 0:
            m[j][g] = cm
            acc[j][g] = _dot(va, jnp.exp2(st - cm).astype(BF16))
        else:
            m_new = jnp.maximum(m[j][g], cm)
            alpha = jnp.exp2(m[j][g] - m_new)
            acc[j][g] = alpha * acc[j][g] + _dot(va, jnp.exp2(st - m_new).astype(BF16))
            m[j][g] = m_new
        if kj == g:
            a = acc[j][g]
            o = jnp.transpose(a[:d] * (1.0 / a[d:d + 1]))
            fz = fz_ref[g * TQ:(g + 1) * TQ, j * LANES:(j + 1) * LANES].astype(F32)
            o_ref[g * TQ:(g + 1) * TQ, j * LANES:(j + 1) * LANES] = (o * _silu(fz)).astype(BF16)

    pending = [scores(*tiles[i]) for i in range(FOX_LOOKAHEAD)]
    for i, tile in enumerate(tiles):
        if i + FOX_LOOKAHEAD < len(tiles):
            pending.append(scores(*tiles[i + FOX_LOOKAHEAD]))
        finish(*tile, pending.pop(0))


def _fox(fm, z, aqt, ak, B, S):
    n = FOX_HEADS_PER_STEP
    dn = n * FOX_HEAD_DIM
    return pl.pallas_call(
        _fox_kernel,
        grid=(B, FOX_HEADS // n),
        in_specs=[
            pl.BlockSpec((dn, S), lambda b, h: (FR_FQ // dn + h, b)),
            pl.BlockSpec((S, dn), lambda b, h: (b, ZB_FK // n + h)),
            pl.BlockSpec((dn, S), lambda b, h: (FR_FV // dn + h, b)),
            pl.BlockSpec((S, dn), lambda b, h: (b, ZB_FZ // n + h)),
            pl.BlockSpec((None, LANES, S), lambda b, h: (b, 0, 0)),
            pl.BlockSpec((S, LANES), lambda b, h: (b, 0)),
        ],
        out_specs=pl.BlockSpec((S, dn), lambda b, h: (b, h)),
        out_shape=jax.ShapeDtypeStruct((B * S, FOX_WIDTH), BF16),
        scratch_shapes=[
            pltpu.VMEM((n, 2 * LANES, S), BF16),
            pltpu.VMEM((n, S, 2 * LANES), BF16),
            pltpu.VMEM((n, FOX_HEAD_DIM + BF16_ROWS, S), BF16),
        ],
        compiler_params=pltpu.CompilerParams(
            dimension_semantics=("arbitrary", "arbitrary"), vmem_limit_bytes=VMEM_LIMIT),
        name="fox",
    )(fm, z, fm, z, aqt, ak)


def _mlstm_kernel(qt_ref, k_ref, vt_ref, mot_ref, mzt_ref, col_ref, grow_ref, mrow_ref, ng_ref,
                  o_ref, vaug_ref):
    S = k_ref.shape[0]
    L = ML_L
    nc = S // L
    dk, dv = ML_QK_DIM, ML_V_DIM
    hs = range(ML_HEADS_PER_STEP)
    for j in hs:
        vaug_ref[j, :dv, :] = vt_ref[j * dv:(j + 1) * dv, :]
        vaug_ref[j, dv:, :] = jnp.ones((BF16_ROWS, S), BF16)
    s_idx = lax.broadcasted_iota(jnp.int32, (L, L), 0)
    t_idx = lax.broadcasted_iota(jnp.int32, (L, L), 1)
    causal = s_idx <= t_idx
    ng = [ng_ref[j * dv:(j + 1) * dv, :] for j in hs]
    head_shift = [(LANES - (pl.program_id(1) * ML_HEADS_PER_STEP + j)) % LANES for j in hs]

    def scores(j, c):
        kc = k_ref[c * L:(c + 1) * L, j * dk:(j + 1) * dk]
        return _dot(kc, qt_ref[j * dk:(j + 1) * dk, c * L:(c + 1) * L])

    def finish(j, c, st, state):
        ct_aug, g_prev = state
        cols = slice(c * L, (c + 1) * L)
        feats = slice(j * dv, (j + 1) * dv)
        b_col = pltpu.roll(col_ref[cols, :], head_shift[j], 1)[:, CL_BETA:CL_BETA + 1]
        g_row = grow_ref[j, c:c + 1, :]
        g_last = g_row[:, L - 1:L]
        va = vaug_ref[j, :, cols]
        kw = k_ref[cols, j * dk:(j + 1) * dk].astype(F32) * jnp.exp2(b_col - g_last)
        upd = _dot(va, kw.astype(BF16))
        sw = st * jnp.where(causal, jnp.exp2(b_col - g_row), 0.0)
        num = _dot(va, sw.astype(BF16))
        if c > 0:
            w_inter = jnp.exp2(g_prev - g_row)
            num = num + w_inter * _dot(ct_aug.astype(BF16), qt_ref[j * dk:(j + 1) * dk, cols])
        den = num[dv:dv + 1, :]
        num = num[:dv, :]
        inv = 1.0 / jnp.maximum(jnp.abs(den), jnp.exp2(-mrow_ref[j, c:c + 1, :]))
        ssq = jnp.sum(num * num, axis=0, keepdims=True)
        t_scale = inv * lax.rsqrt(inv * inv * ssq * (1.0 / dv) + EPS)
        mo = mot_ref[feats, cols].astype(F32)
        mz = mzt_ref[feats, cols].astype(F32)
        gate = mz * (1.0 / ((1.0 + jnp.exp2(mo * -LOG2E)) * (1.0 + jnp.exp2(mz * -LOG2E))))
        o_ref[cols, feats] = jnp.transpose(num * t_scale * (ng[j] * gate)).astype(BF16)
        if c == 0:
            return upd, g_last
        return jnp.exp2(g_prev - g_last) * ct_aug + upd, g_last

    work = [(j, c) for c in range(nc) for j in hs]
    pending = [scores(*work[i]) for i in range(ML_LOOKAHEAD)]
    state = [(None, None) for _ in hs]
    for i, (j, c) in enumerate(work):
        if i + ML_LOOKAHEAD < len(work):
            pending.append(scores(*work[i + ML_LOOKAHEAD]))
        state[j] = finish(j, c, pending.pop(0), state[j])


def _mlstm(fm, z, colpack, rowpack4, ng_col, B, S):
    nc = S // ML_L
    n = ML_HEADS_PER_STEP
    dk, dv = n * ML_QK_DIM, n * ML_V_DIM
    row_spec = lambda slab: pl.BlockSpec((None, n, nc, ML_L), lambda b, h: (b, slab // n + h, 0, 0))
    return pl.pallas_call(
        _mlstm_kernel,
        grid=(B, ML_HEADS // n),
        in_specs=[
            pl.BlockSpec((dk, S), lambda b, h: (FR_MQ // dk + h, b)),
            pl.BlockSpec((S, dk), lambda b, h: (b, ZB_MK * LANES // dk + h)),
            pl.BlockSpec((dv, S), lambda b, h: (FR_MV // dv + h, b)),
            pl.BlockSpec((dv, S), lambda b, h: (FR_MO // dv + h, b)),
            pl.BlockSpec((dv, S), lambda b, h: (FR_MZ // dv + h, b)),
            pl.BlockSpec((S, LANES), lambda b, h: (b, 0)),
            row_spec(RS_G), row_spec(RS_M),
            pl.BlockSpec((dv, 1), lambda b, h: (h, 0)),
        ],
        out_specs=pl.BlockSpec((S, dv), lambda b, h: (b, h)),
        out_shape=jax.ShapeDtypeStruct((B * S, ML_WIDTH), BF16),
        scratch_shapes=[pltpu.VMEM((n, ML_V_DIM + BF16_ROWS, S), BF16)],
        compiler_params=pltpu.CompilerParams(
            dimension_semantics=("arbitrary", "arbitrary"), vmem_limit_bytes=VMEM_LIMIT),
        name="mlstm",
    )(fm, z, fm, fm, fm, colpack, rowpack4, rowpack4, ng_col)


def _out_kernel(oa_ref, hb_ref, ga_ref, gb_ref, x_ref, wfd_ref, wmd_ref, wo_ref, bga_ref, bgb_ref,
                fg_ref, o_ref):
    def branches(s):
        rows = slice(s * OUT_SUB, (s + 1) * OUT_SUB)
        return _dot(oa_ref[rows, :], wfd_ref[...]), _dot(hb_ref[rows, :], wmd_ref[...])

    def finish(s, ya, yb):
        rows = slice(s * OUT_SUB, (s + 1) * OUT_SUB)
        gate_a = jax.nn.sigmoid(ga_ref[rows, :].astype(F32) + bga_ref[...])
        gate_b = jax.nn.sigmoid(gb_ref[rows, :].astype(F32) + bgb_ref[...])
        y = gate_a * ya + gate_b * yb
        r = x_ref[rows, :] + _dot(y.astype(BF16), wo_ref[...])
        ms = jnp.mean(r * r, axis=-1, keepdims=True)
        o_ref[rows, :] = r * lax.rsqrt(ms + EPS) * fg_ref[...]

    n_sub = TM_OUT // OUT_SUB
    pending = branches(0)
    for s in range(n_sub):
        nxt = branches(s + 1) if s + 1 < n_sub else None
        finish(s, *pending)
        pending = nxt


def _out_stage(oa, hb, z, x2, wfd, wmd, wo, b_gate, final_g):
    T = x2.shape[0]
    row = lambda i: (i, 0)
    const = lambda i: (0, 0)
    return pl.pallas_call(
        _out_kernel,
        grid=(T // TM_OUT,),
        in_specs=[
            pl.BlockSpec((TM_OUT, FOX_WIDTH), row),
            pl.BlockSpec((TM_OUT, ML_WIDTH), row),
            pl.BlockSpec((TM_OUT, D_MODEL), lambda i: (i, ZB_GA // 8)),
            pl.BlockSpec((TM_OUT, D_MODEL), lambda i: (i, ZB_GB // 8)),
            pl.BlockSpec((TM_OUT, D_MODEL), row),
            pl.BlockSpec((FOX_WIDTH, D_MODEL), const, pipeline_mode=pl.Buffered(1)),
            pl.BlockSpec((ML_WIDTH, D_MODEL), const, pipeline_mode=pl.Buffered(1)),
            pl.BlockSpec((D_MODEL, D_MODEL), const, pipeline_mode=pl.Buffered(1)),
            pl.BlockSpec((1, D_MODEL), lambda i: (0, 0)),
            pl.BlockSpec((1, D_MODEL), lambda i: (0, 1)),
            pl.BlockSpec((1, D_MODEL), const),
        ],
        out_specs=pl.BlockSpec((TM_OUT, D_MODEL), row),
        out_shape=jax.ShapeDtypeStruct((T, D_MODEL), F32),
        compiler_params=pltpu.CompilerParams(
            dimension_semantics=("arbitrary",), vmem_limit_bytes=VMEM_LIMIT),
        name="outstage",
    )(oa, hb, z, z, x2, wfd, wmd, wo, b_gate, b_gate, final_g)


def _layer(x2, B, S, norm_g, w_in, b_fox_f, conv_w, conv_b, b_ml_i, b_ml_f, ml_norm_g, b_gate,
           w_fox_down, w_ml_down, w_out, out_g):
    assert w_in.shape == (D_MODEL, IN_WIDTH)
    w_fm, w_tm = _prep_weights(jnp.transpose(w_in))
    zrow = lambda n: jnp.zeros((n,), F32)
    bias_col = jnp.concatenate(
        [b_fox_f, b_ml_i, zrow(8 - ML_HEADS), b_ml_f, zrow(GATE_WIDTH - GL_LF - ML_HEADS)]).reshape(GATE_WIDTH, 1)
    cwq_t = jnp.concatenate(
        [jnp.transpose(conv_w[:, :ML_QK_WIDTH]), conv_b[:ML_QK_WIDTH, None],
         jnp.zeros((ML_QK_WIDTH, LANES - CONV_WIDTH - 1), F32)], axis=1)

    fm, zg, z = _inproj(x2, norm_g.reshape(1, D_MODEL), w_fm, w_tm, cwq_t, conv_w,
                        conv_b.reshape(1, 2 * ML_QK_WIDTH), S)
    rowpack, colpack, aqt, ak = _gates(zg, bias_col, B, S)
    rowpack4 = rowpack.reshape(B, 16, S // ML_L, ML_L)
    oa = _fox(fm, z, aqt, ak, B, S)
    hb = _mlstm(fm, z, colpack, rowpack4, ml_norm_g.reshape(ML_WIDTH, 1), B, S)
    return _out_stage(oa, hb, z, x2, w_fox_down.astype(BF16), w_ml_down.astype(BF16),
                      w_out.astype(BF16), b_gate.reshape(1, 2 * D_MODEL), out_g.reshape(1, D_MODEL))


def kernel(x, norm_g, w_in, b_fox_f, conv_w, conv_b, b_ml_i, b_ml_f, ml_norm_g, b_gate,
           w_fox_down, w_ml_down, w_out, final_g):
    B, S, _ = x.shape
    depth = norm_g.shape[0]
    assert depth == 1, "the final RMSNorm is fused into the (single) layer's output stage"
    out = _layer(x.reshape(B * S, D_MODEL), B, S, norm_g[0], w_in[0], b_fox_f[0], conv_w[0], conv_b[0],
                 b_ml_i[0], b_ml_f[0], ml_norm_g[0], b_gate[0], w_fox_down[0], w_ml_down[0],
                 w_out[0], final_g)
    return out.reshape(B, S, D_MODEL)
```

```python
import functools

import jax
import jax.numpy as jnp
from jax import lax
from jax.experimental import pallas as pl
from jax.experimental.pallas import tpu as pltpu

F32 = jnp.float32
BF16 = jnp.bfloat16

D_MODEL = 1024
FOX_HEADS = 8
FOX_HEAD_DIM = 128
FOX_WIDTH = FOX_HEADS * FOX_HEAD_DIM
ML_HEADS = 4
ML_QK_DIM = 128
ML_V_DIM = 256
ML_QK_WIDTH = ML_HEADS * ML_QK_DIM
ML_WIDTH = ML_HEADS * ML_V_DIM
CONV_WIDTH = 4
EPS = 1e-6

LANES = 128
SUBLANES = 8
BF16_ROWS = 16
GATE_WIDTH = LANES
VMEM_LIMIT = 56 * 1024 * 1024
LOG2E = 1.4426950408889634
FOX_Q_SCALE = FOX_HEAD_DIM ** -0.5 * LOG2E

FR_FQ = 0
FR_FV = FR_FQ + FOX_WIDTH
FR_MV = FR_FV + FOX_WIDTH
FR_MO = FR_MV + ML_WIDTH
FR_MZ = FR_MO + ML_WIDTH
FR_MQ = FR_MZ + ML_WIDTH
FR_GATE = FR_MQ + ML_QK_WIDTH
FM_ROWS = FR_GATE + GATE_WIDTH
FM_CHUNKS = ((FR_MQ, FM_ROWS), (FR_FQ, FR_FV), (FR_FV, FR_MV), (FR_MV, FR_MO), (FR_MO, FR_MZ), (FR_MZ, FR_MQ))
ZB_FK, ZB_FZ, ZB_GA, ZB_GB, ZB_MK = 0, 8, 16, 24, 32
Z_WIDTH = 4608
TN_IN = 1536
TM_IN = 512

GL_FOX, GL_IG, GL_LF = 0, 8, 16
RS_G, RS_M = 0, 8
CL_BETA = 0
AUG_LANES = 6 * FOX_HEADS

GATE_SEQS_PER_STEP = 4
TQ = 256
FOX_HEADS_PER_STEP = 4
FOX_LOOKAHEAD = 4
ML_L = 256
ML_HEADS_PER_STEP = 4
ML_LOOKAHEAD = 2
TM_OUT = 1024
OUT_SUB = 256

NT_DIMS = (((1,), (1,)), ((), ()))


def _dot(a, b):
    return jnp.dot(a, b, preferred_element_type=F32)


O_FQ, O_FK, O_FV = 0, FOX_WIDTH, 2 * FOX_WIDTH
O_FF = 3 * FOX_WIDTH
O_FZ = O_FF + FOX_HEADS
O_MQ = O_FZ + FOX_WIDTH
O_MK = O_MQ + ML_QK_WIDTH
O_MV = O_MK + ML_QK_WIDTH
O_MI = O_MV + ML_WIDTH
O_MF = O_MI + ML_HEADS
O_MO = O_MF + ML_HEADS
O_MZ = O_MO + ML_WIDTH
O_GA = O_MZ + ML_WIDTH
O_GB = O_GA + D_MODEL
IN_WIDTH = O_GB + D_MODEL

PREP_UNIT = 512
FM_GROUPS = ((O_FQ, FOX_WIDTH), (O_FV, FOX_WIDTH), (O_MV, ML_WIDTH), (O_MO, ML_WIDTH),
             (O_MZ, ML_WIDTH), (O_MQ, ML_QK_WIDTH))
TM_GROUPS = ((O_FK, FOX_WIDTH), (O_FZ, FOX_WIDTH), (O_GA, D_MODEL), (O_GB, D_MODEL),
             (O_MK, ML_QK_WIDTH))
KIND_FM, KIND_GATE, KIND_TM = 0, 1, 2


def _prep_units():
    units = [(s, KIND_FM) for off, width in FM_GROUPS for s in range(off, off + width, PREP_UNIT)]
    units.append((0, KIND_GATE))
    units += [(s, KIND_TM) for off, width in TM_GROUPS for s in range(off, off + width, PREP_UNIT)]
    assert all(s % SUBLANES == 0 for s, _ in units)
    return [s // SUBLANES for s, _ in units], [k for _, k in units]


PREP_STARTS, PREP_KINDS = _prep_units()
N_FM_UNITS = PREP_KINDS.index(KIND_GATE) + 1
FM_W_ROWS = N_FM_UNITS * PREP_UNIT


def _prep_kernel(start_ref, kind_ref, w_ref, ff_ref, mimf_ref, fm_ref, tm_ref):
    kind = kind_ref[pl.program_id(0)]

    @pl.when(kind == KIND_FM)
    def _():
        fm_ref[...] = w_ref[...].astype(BF16)

    @pl.when(kind == KIND_TM)
    def _():
        tm_ref[...] = jnp.transpose(w_ref[...]).astype(BF16)

    @pl.when(kind == KIND_GATE)
    def _():
        mimf = mimf_ref[...]
        pad = lambda n: jnp.zeros((n, D_MODEL), F32)
        rows = jnp.concatenate(
            [ff_ref[...], mimf[:ML_HEADS], pad(GL_LF - GL_IG - ML_HEADS), mimf[ML_HEADS:],
             pad(PREP_UNIT - GL_LF - ML_HEADS)], axis=0)
        fm_ref[...] = rows.astype(BF16)


def _prep_weights(w_t):
    assert O_MF == O_MI + ML_HEADS and FOX_HEADS == SUBLANES and 2 * ML_HEADS == SUBLANES
    gate_rows = lambda off: pl.BlockSpec((pl.Element(SUBLANES), pl.Element(D_MODEL)),
                                         lambda u, start, kind: (off, 0))
    grid_spec = pltpu.PrefetchScalarGridSpec(
        num_scalar_prefetch=2,
        grid=(len(PREP_KINDS),),
        in_specs=[
            pl.BlockSpec((pl.Element(PREP_UNIT), pl.Element(D_MODEL)),
                         lambda u, start, kind: (start[u] * SUBLANES, 0)),
            gate_rows(O_FF), gate_rows(O_MI),
        ],
        out_specs=[
            pl.BlockSpec((PREP_UNIT, D_MODEL), lambda u, start, kind: (jnp.minimum(u, N_FM_UNITS - 1), 0)),
            pl.BlockSpec((D_MODEL, PREP_UNIT), lambda u, start, kind: (0, jnp.maximum(u - N_FM_UNITS, 0))),
        ],
    )
    return pl.pallas_call(
        _prep_kernel,
        grid_spec=grid_spec,
        out_shape=[
            jax.ShapeDtypeStruct((FM_W_ROWS, D_MODEL), BF16),
            jax.ShapeDtypeStruct((D_MODEL, Z_WIDTH), BF16),
        ],
        compiler_params=pltpu.CompilerParams(
            dimension_semantics=("arbitrary",), vmem_limit_bytes=VMEM_LIMIT),
        name="prep",
    )(jnp.asarray(PREP_STARTS, jnp.int32), jnp.asarray(PREP_KINDS, jnp.int32), w_t, w_t, w_t)


def _silu(x):
    return x * (1.0 / (1.0 + jnp.exp2(x * -LOG2E)))


def _causal_conv(x, taps, bias, axis, halo):
    def run(xx):
        y = xx * taps[CONV_WIDTH - 1] + bias
        for r in range(1, CONV_WIDTH):
            y = y + pltpu.roll(xx, r, axis) * taps[CONV_WIDTH - 1 - r]
        return y

    edge = halo.shape[axis]
    if axis == 0:
        head = run(jnp.concatenate([halo, x[:edge, :]], axis=0))[edge:, :]
        return jnp.concatenate([head, run(x)[edge:, :]], axis=0)
    head = run(jnp.concatenate([halo, x[:, :edge]], axis=1))[:, edge:]
    return jnp.concatenate([head, run(x)[:, edge:]], axis=1)


def _inproj_kernel(x_ref, g_ref, wfm_ref, wtm_ref, cwq_ref, cwk_ref, cbk_ref,
                   fm_ref, zg_ref, z_ref, qhalo_ref, khalo_ref, *, steps_per_seq):
    @pl.when(pl.program_id(0) % steps_per_seq == 0)
    def _():
        qhalo_ref[...] = jnp.zeros_like(qhalo_ref)
        khalo_ref[...] = jnp.zeros_like(khalo_ref)

    x = x_ref[...]
    ms = jnp.mean(x * x, axis=-1, keepdims=True)
    xh = (x * lax.rsqrt(ms + EPS) * g_ref[...]).astype(BF16)

    def token_major(c0):
        res = _dot(xh, wtm_ref[:, c0:c0 + TN_IN])

        def epilogue():
            if c0 + TN_IN == Z_WIDTH:
                k0 = TN_IN - ML_QK_WIDTH
                xk = res[:, k0:]
                cwk = cwk_ref[...]
                taps = [cwk[j:j + 1, :] for j in range(CONV_WIDTH)]
                yk = _causal_conv(xk, taps, cbk_ref[...], 0, khalo_ref[...])
                khalo_ref[...] = xk[TM_IN - SUBLANES:, :]
                z_ref[:, c0:c0 + k0] = res[:, :k0].astype(BF16)
                z_ref[:, c0 + k0:] = _silu(yk).astype(BF16)
            else:
                z_ref[:, c0:c0 + TN_IN] = res.astype(BF16)
        return epilogue

    def feature_major(r0, r1):
        res = lax.dot_general(wfm_ref[r0:r1, :], xh, NT_DIMS, preferred_element_type=F32)

        def epilogue():
            if r0 == FR_MQ:
                xq = res[:ML_QK_WIDTH, :]
                cwq = cwq_ref[...]
                taps = [cwq[:, j:j + 1] for j in range(CONV_WIDTH)]
                yq = _causal_conv(xq, taps, cwq[:, CONV_WIDTH:CONV_WIDTH + 1], 1, qhalo_ref[...])
                fm_ref[FR_MQ:FR_GATE, :] = (_silu(yq) * (ML_QK_DIM ** -0.5)).astype(BF16)
                qhalo_ref[...] = xq[:, TM_IN - LANES:]
                zg_ref[...] = res[ML_QK_WIDTH:, :]
            elif r0 == FR_FQ:
                fm_ref[r0:r1, :] = (res * FOX_Q_SCALE).astype(BF16)
            else:
                fm_ref[r0:r1, :] = res.astype(BF16)
        return epilogue

    chunks = ([functools.partial(token_major, Z_WIDTH - TN_IN)]
              + [functools.partial(feature_major, r0, r1) for r0, r1 in FM_CHUNKS]
              + [functools.partial(token_major, c0) for c0 in range(0, Z_WIDTH - TN_IN, TN_IN)])
    pending = None
    for start in chunks:
        epilogue = start()
        if pending is not None:
            pending()
        pending = epilogue
    pending()


def _inproj(x2, norm_g, w_fm, w_tm, cwq_t, conv_w, conv_b, S):
    T = x2.shape[0]
    assert S % TM_IN == 0 and ZB_MK * LANES + ML_QK_WIDTH == Z_WIDTH
    resident = pl.Buffered(1)
    return pl.pallas_call(
        functools.partial(_inproj_kernel, steps_per_seq=S // TM_IN),
        grid=(T // TM_IN,),
        in_specs=[
            pl.BlockSpec((TM_IN, D_MODEL), lambda i: (i, 0)),
            pl.BlockSpec((1, D_MODEL), lambda i: (0, 0)),
            pl.BlockSpec((FM_W_ROWS, D_MODEL), lambda i: (0, 0), pipeline_mode=resident),
            pl.BlockSpec((D_MODEL, Z_WIDTH), lambda i: (0, 0), pipeline_mode=resident),
            pl.BlockSpec((ML_QK_WIDTH, LANES), lambda i: (0, 0)),
            pl.BlockSpec((CONV_WIDTH, ML_QK_WIDTH), lambda i: (0, 1)),
            pl.BlockSpec((1, ML_QK_WIDTH), lambda i: (0, 1)),
        ],
        out_specs=[
            pl.BlockSpec((FR_GATE, TM_IN), lambda i: (0, i)),
            pl.BlockSpec((GATE_WIDTH, TM_IN), lambda i: (0, i)),
            pl.BlockSpec((TM_IN, Z_WIDTH), lambda i: (i, 0)),
        ],
        out_shape=[
            jax.ShapeDtypeStruct((FR_GATE, T), BF16),
            jax.ShapeDtypeStruct((GATE_WIDTH, T), F32),
            jax.ShapeDtypeStruct((T, Z_WIDTH), BF16),
        ],
        scratch_shapes=[
            pltpu.VMEM((ML_QK_WIDTH, LANES), F32),
            pltpu.VMEM((SUBLANES, ML_QK_WIDTH), F32),
        ],
        compiler_params=pltpu.CompilerParams(
            dimension_semantics=("arbitrary",), vmem_limit_bytes=VMEM_LIMIT),
        name="inproj",
    )(x2, norm_g, w_fm, w_tm, cwq_t, conv_w, conv_b)


def _scan_lanes(x, op, ident):
    n = x.shape[1]
    lane = lax.broadcasted_iota(jnp.int32, x.shape, 1)
    k = 1
    while k < n:
        shifted = pltpu.roll(x, k, 1)
        x = op(x, jnp.where(lane >= k, shifted, ident))
        k *= 2
    return x


def _log_sigmoid(x):
    return jnp.minimum(x, 0.0) - jnp.log1p(jnp.exp(-jnp.abs(x)))


def _split3(x):
    hi = x.astype(BF16).astype(F32)
    r = x - hi
    mid = r.astype(BF16).astype(F32)
    lo = (r - mid).astype(BF16).astype(F32)
    return hi, mid, lo


def _gate_kernel(zg_ref, bias_ref, row_ref, col_ref, aq_ref, ak_ref):
    S = row_ref.shape[2]
    for j in range(row_ref.shape[0]):
        t = zg_ref[:, j * S:(j + 1) * S] + bias_ref[...]
        c = _scan_lanes(_log_sigmoid(t[GL_FOX:GL_FOX + 8]), jnp.add, 0.0)
        ig = t[GL_IG:GL_IG + 8]
        fsum = _scan_lanes(_log_sigmoid(t[GL_LF:GL_LF + 8]), jnp.add, 0.0)
        beta = (ig - fsum) * LOG2E
        gmax = jnp.maximum(_scan_lanes(beta, jnp.maximum, -jnp.inf), 0.0)
        m = fsum * LOG2E + gmax
        row_ref[j, RS_G:RS_G + 8, :] = gmax
        row_ref[j, RS_M:RS_M + 8, :] = m
        col_ref[j * S:(j + 1) * S, :] = jnp.transpose(
            jnp.concatenate([beta, jnp.zeros((LANES - 8, S), F32)], axis=0))
        hi, mid, lo = _split3(c * LOG2E)
        ones = jnp.ones((8, S), F32)
        pad = jnp.zeros((LANES - AUG_LANES, S), F32)
        aq_ref[j] = jnp.concatenate([hi, mid, lo, ones, ones, ones, pad], axis=0).astype(BF16)
        ak_ref[j * S:(j + 1) * S, :] = jnp.transpose(
            jnp.concatenate([ones, ones, ones, -hi, -mid, -lo, pad], axis=0)).astype(BF16)


def _gates(zg, bias_col, B, S):
    nb = GATE_SEQS_PER_STEP
    return pl.pallas_call(
        _gate_kernel,
        grid=(B // nb,),
        in_specs=[
            pl.BlockSpec((GATE_WIDTH, nb * S), lambda b: (0, b)),
            pl.BlockSpec((GATE_WIDTH, 1), lambda b: (0, 0)),
        ],
        out_specs=[
            pl.BlockSpec((nb, 16, S), lambda b: (b, 0, 0)),
            pl.BlockSpec((nb * S, LANES), lambda b: (b, 0)),
            pl.BlockSpec((nb, LANES, S), lambda b: (b, 0, 0)),
            pl.BlockSpec((nb * S, LANES), lambda b: (b, 0)),
        ],
        out_shape=[
            jax.ShapeDtypeStruct((B, 16, S), F32),
            jax.ShapeDtypeStruct((B * S, LANES), F32),
            jax.ShapeDtypeStruct((B, LANES, S), BF16),
            jax.ShapeDtypeStruct((B * S, LANES), BF16),
        ],
        compiler_params=pltpu.CompilerParams(
            dimension_semantics=("arbitrary",), vmem_limit_bytes=VMEM_LIMIT),
        name="gates",
    )(zg, bias_col)


def _fox_kernel(qt_ref, k_ref, vt_ref, fz_ref, aqt_ref, ak_ref, o_ref, qaug_ref, kaug_ref, vaug_ref):
    S = k_ref.shape[0]
    d = FOX_HEAD_DIM
    hs = range(FOX_HEADS_PER_STEP)
    lane = lax.broadcasted_iota(jnp.int32, (1, LANES), 1)
    for j in hs:
        h = pl.program_id(1) * FOX_HEADS_PER_STEP + j
        keep = jnp.where(((lane & (FOX_HEADS - 1)) == h) & (lane < AUG_LANES), 1.0, 0.0).astype(BF16)
        qaug_ref[j, :d, :] = qt_ref[j * d:(j + 1) * d, :]
        qaug_ref[j, d:, :] = aqt_ref[...]
        kaug_ref[j, :, :LANES] = k_ref[:, j * LANES:(j + 1) * LANES]
        kaug_ref[j, :, LANES:] = ak_ref[...] * keep
        vaug_ref[j, :d, :] = vt_ref[j * d:(j + 1) * d, :]
        vaug_ref[j, d:, :] = jnp.ones((BF16_ROWS, S), BF16)
    s_idx = lax.broadcasted_iota(jnp.int32, (TQ, TQ), 0)
    t_idx = lax.broadcasted_iota(jnp.int32, (TQ, TQ), 1)
    causal = s_idx <= t_idx

    ng = S // TQ
    m = [[None] * ng for _ in hs]
    acc = [[None] * ng for _ in hs]
    tiles = [(j, kj, g) for kj in range(ng) for g in range(kj, ng) for j in hs]

    def scores(j, kj, g):
        ka = kaug_ref[j, kj * TQ:(kj + 1) * TQ, :]
        qa = qaug_ref[j, :, g * TQ:(g + 1) * TQ]
        return _dot(ka, qa)

    def finish(j, kj, g, st):
        va = vaug_ref[j, :, kj * TQ:(kj + 1) * TQ]
        if kj == g:
            st = jnp.where(causal, st, -jnp.inf)
        cm = jnp.max(st, axis=0, keepdims=True)
        if kj == 0:
            m[j][g] = cm
            acc[j][g] = _dot(va, jnp.exp2(st - cm).astype(BF16))
        else:
            m_new = jnp.maximum(m[j][g], cm)
            alpha = jnp.exp2(m[j][g] - m_new)
            acc[j][g] = alpha * acc[j][g] + _dot(va, jnp.exp2(st - m_new).astype(BF16))
            m[j][g] = m_new
        if kj == g:
            a = acc[j][g]
            o = jnp.transpose(a[:d] * (1.0 / a[d:d + 1]))
            fz = fz_ref[g * TQ:(g + 1) * TQ, j * LANES:(j + 1) * LANES].astype(F32)
            o_ref[g * TQ:(g + 1) * TQ, j * LANES:(j + 1) * LANES] = (o * _silu(fz)).astype(BF16)

    pending = [scores(*tiles[i]) for i in range(FOX_LOOKAHEAD)]
    for i, tile in enumerate(tiles):
        if i + FOX_LOOKAHEAD < len(tiles):
            pending.append(scores(*tiles[i + FOX_LOOKAHEAD]))
        finish(*tile, pending.pop(0))


def _fox(fm, z, aqt, ak, B, S):
    n = FOX_HEADS_PER_STEP
    dn = n * FOX_HEAD_DIM
    assert FR_FQ % dn == 0 and FR_FV % dn == 0 and ZB_FK % n == 0 and ZB_FZ % n == 0
    return pl.pallas_call(
        _fox_kernel,
        grid=(B, FOX_HEADS // n),
        in_specs=[
            pl.BlockSpec((dn, S), lambda b, h: (FR_FQ // dn + h, b)),
            pl.BlockSpec((S, dn), lambda b, h: (b, ZB_FK // n + h)),
            pl.BlockSpec((dn, S), lambda b, h: (FR_FV // dn + h, b)),
            pl.BlockSpec((S, dn), lambda b, h: (b, ZB_FZ // n + h)),
            pl.BlockSpec((None, LANES, S), lambda b, h: (b, 0, 0)),
            pl.BlockSpec((S, LANES), lambda b, h: (b, 0)),
        ],
        out_specs=pl.BlockSpec((S, dn), lambda b, h: (b, h)),
        out_shape=jax.ShapeDtypeStruct((B * S, FOX_WIDTH), BF16),
        scratch_shapes=[
            pltpu.VMEM((n, 2 * LANES, S), BF16),
            pltpu.VMEM((n, S, 2 * LANES), BF16),
            pltpu.VMEM((n, FOX_HEAD_DIM + BF16_ROWS, S), BF16),
        ],
        compiler_params=pltpu.CompilerParams(
            dimension_semantics=("arbitrary", "arbitrary"), vmem_limit_bytes=VMEM_LIMIT),
        name="fox",
    )(fm, z, fm, z, aqt, ak)


def _mlstm_kernel(qt_ref, k_ref, vt_ref, mot_ref, mzt_ref, col_ref, grow_ref, mrow_ref, ng_ref,
                  o_ref, vaug_ref):
    S = k_ref.shape[0]
    L = ML_L
    nc = S // L
    dk, dv = ML_QK_DIM, ML_V_DIM
    hs = range(ML_HEADS_PER_STEP)
    for j in hs:
        vaug_ref[j, :dv, :] = vt_ref[j * dv:(j + 1) * dv, :]
        vaug_ref[j, dv:, :] = jnp.ones((BF16_ROWS, S), BF16)
    s_idx = lax.broadcasted_iota(jnp.int32, (L, L), 0)
    t_idx = lax.broadcasted_iota(jnp.int32, (L, L), 1)
    causal = s_idx <= t_idx
    ng = [ng_ref[j * dv:(j + 1) * dv, :] for j in hs]
    head_shift = [(LANES - (pl.program_id(1) * ML_HEADS_PER_STEP + j)) % LANES for j in hs]

    def scores(j, c):
        kc = k_ref[c * L:(c + 1) * L, j * dk:(j + 1) * dk]
        return _dot(kc, qt_ref[j * dk:(j + 1) * dk, c * L:(c + 1) * L])

    def finish(j, c, st, state):
        ct_aug, g_prev = state
        cols = slice(c * L, (c + 1) * L)
        feats = slice(j * dv, (j + 1) * dv)
        b_col = pltpu.roll(col_ref[cols, :], head_shift[j], 1)[:, CL_BETA:CL_BETA + 1]
        g_row = grow_ref[j, c:c + 1, :]
        g_last = g_row[:, L - 1:L]
        va = vaug_ref[j, :, cols]
        kw = k_ref[cols, j * dk:(j + 1) * dk].astype(F32) * jnp.exp2(b_col - g_last)
        upd = _dot(va, kw.astype(BF16))
        sw = st * jnp.where(causal, jnp.exp2(b_col - g_row), 0.0)
        num = _dot(va, sw.astype(BF16))
        if c > 0:
            w_inter = jnp.exp2(g_prev - g_row)
            num = num + w_inter * _dot(ct_aug.astype(BF16), qt_ref[j * dk:(j + 1) * dk, cols])
        den = num[dv:dv + 1, :]
        num = num[:dv, :]
        inv = 1.0 / jnp.maximum(jnp.abs(den), jnp.exp2(-mrow_ref[j, c:c + 1, :]))
        ssq = jnp.sum(num * num, axis=0, keepdims=True)
        t_scale = inv * lax.rsqrt(inv * inv * ssq * (1.0 / dv) + EPS)
        mo = mot_ref[feats, cols].astype(F32)
        mz = mzt_ref[feats, cols].astype(F32)
        gate = mz * (1.0 / ((1.0 + jnp.exp2(mo * -LOG2E)) * (1.0 + jnp.exp2(mz * -LOG2E))))
        o_ref[cols, feats] = jnp.transpose(num * t_scale * (ng[j] * gate)).astype(BF16)
        if c == 0:
            return upd, g_last
        return jnp.exp2(g_prev - g_last) * ct_aug + upd, g_last

    work = [(j, c) for c in range(nc) for j in hs]
    pending = [scores(*work[i]) for i in range(ML_LOOKAHEAD)]
    state = [(None, None) for _ in hs]
    for i, (j, c) in enumerate(work):
        if i + ML_LOOKAHEAD < len(work):
            pending.append(scores(*work[i + ML_LOOKAHEAD]))
        state[j] = finish(j, c, pending.pop(0), state[j])


def _mlstm(fm, z, colpack, rowpack4, ng_col, B, S):
    nc = S // ML_L
    n = ML_HEADS_PER_STEP
    dk, dv = n * ML_QK_DIM, n * ML_V_DIM
    assert FR_MQ % dk == 0 and (ZB_MK * LANES) % dk == 0 and RS_G % n == 0 and RS_M % n == 0
    assert FR_MV % dv == 0 and FR_MO % dv == 0 and FR_MZ % dv == 0
    row_spec = lambda slab: pl.BlockSpec((None, n, nc, ML_L), lambda b, h: (b, slab // n + h, 0, 0))
    return pl.pallas_call(
        _mlstm_kernel,
        grid=(B, ML_HEADS // n),
        in_specs=[
            pl.BlockSpec((dk, S), lambda b, h: (FR_MQ // dk + h, b)),
            pl.BlockSpec((S, dk), lambda b, h: (b, ZB_MK * LANES // dk + h)),
            pl.BlockSpec((dv, S), lambda b, h: (FR_MV // dv + h, b)),
            pl.BlockSpec((dv, S), lambda b, h: (FR_MO // dv + h, b)),
            pl.BlockSpec((dv, S), lambda b, h: (FR_MZ // dv + h, b)),
            pl.BlockSpec((S, LANES), lambda b, h: (b, 0)),
            row_spec(RS_G), row_spec(RS_M),
            pl.BlockSpec((dv, 1), lambda b, h: (h, 0)),
        ],
        out_specs=pl.BlockSpec((S, dv), lambda b, h: (b, h)),
        out_shape=jax.ShapeDtypeStruct((B * S, ML_WIDTH), BF16),
        scratch_shapes=[pltpu.VMEM((n, ML_V_DIM + BF16_ROWS, S), BF16)],
        compiler_params=pltpu.CompilerParams(
            dimension_semantics=("arbitrary", "arbitrary"), vmem_limit_bytes=VMEM_LIMIT),
        name="mlstm",
    )(fm, z, fm, fm, fm, colpack, rowpack4, rowpack4, ng_col)


def _out_kernel(oa_ref, hb_ref, ga_ref, gb_ref, x_ref, wfd_ref, wmd_ref, wo_ref, bga_ref, bgb_ref,
                fg_ref, o_ref):
    def branches(s):
        rows = slice(s * OUT_SUB, (s + 1) * OUT_SUB)
        return _dot(oa_ref[rows, :], wfd_ref[...]), _dot(hb_ref[rows, :], wmd_ref[...])

    def finish(s, ya, yb):
        rows = slice(s * OUT_SUB, (s + 1) * OUT_SUB)
        gate_a = jax.nn.sigmoid(ga_ref[rows, :].astype(F32) + bga_ref[...])
        gate_b = jax.nn.sigmoid(gb_ref[rows, :].astype(F32) + bgb_ref[...])
        y = gate_a * ya + gate_b * yb
        r = x_ref[rows, :] + _dot(y.astype(BF16), wo_ref[...])
        ms = jnp.mean(r * r, axis=-1, keepdims=True)
        o_ref[rows, :] = r * lax.rsqrt(ms + EPS) * fg_ref[...]

    n_sub = TM_OUT // OUT_SUB
    pending = branches(0)
    for s in range(n_sub):
        nxt = branches(s + 1) if s + 1 < n_sub else None
        finish(s, *pending)
        pending = nxt


def _out_stage(oa, hb, z, x2, wfd, wmd, wo, b_gate, final_g):
    T = x2.shape[0]
    row = lambda i: (i, 0)
    const = lambda i: (0, 0)
    return pl.pallas_call(
        _out_kernel,
        grid=(T // TM_OUT,),
        in_specs=[
            pl.BlockSpec((TM_OUT, FOX_WIDTH), row),
            pl.BlockSpec((TM_OUT, ML_WIDTH), row),
            pl.BlockSpec((TM_OUT, D_MODEL), lambda i: (i, ZB_GA // 8)),
            pl.BlockSpec((TM_OUT, D_MODEL), lambda i: (i, ZB_GB // 8)),
            pl.BlockSpec((TM_OUT, D_MODEL), row),
            pl.BlockSpec((FOX_WIDTH, D_MODEL), const, pipeline_mode=pl.Buffered(1)),
            pl.BlockSpec((ML_WIDTH, D_MODEL), const, pipeline_mode=pl.Buffered(1)),
            pl.BlockSpec((D_MODEL, D_MODEL), const, pipeline_mode=pl.Buffered(1)),
            pl.BlockSpec((1, D_MODEL), lambda i: (0, 0)),
            pl.BlockSpec((1, D_MODEL), lambda i: (0, 1)),
            pl.BlockSpec((1, D_MODEL), const),
        ],
        out_specs=pl.BlockSpec((TM_OUT, D_MODEL), row),
        out_shape=jax.ShapeDtypeStruct((T, D_MODEL), F32),
        compiler_params=pltpu.CompilerParams(
            dimension_semantics=("arbitrary",), vmem_limit_bytes=VMEM_LIMIT),
        name="outstage",
    )(oa, hb, z, z, x2, wfd, wmd, wo, b_gate, b_gate, final_g)


def _layer(x2, B, S, norm_g, w_in, b_fox_f, conv_w, conv_b, b_ml_i, b_ml_f, ml_norm_g, b_gate,
           w_fox_down, w_ml_down, w_out, out_g):
    assert w_in.shape == (D_MODEL, IN_WIDTH)
    w_fm, w_tm = _prep_weights(jnp.transpose(w_in))
    zrow = lambda n: jnp.zeros((n,), F32)
    bias_col = jnp.concatenate(
        [b_fox_f, b_ml_i, zrow(8 - ML_HEADS), b_ml_f, zrow(GATE_WIDTH - GL_LF - ML_HEADS)]).reshape(GATE_WIDTH, 1)
    cwq_t = jnp.concatenate(
        [jnp.transpose(conv_w[:, :ML_QK_WIDTH]), conv_b[:ML_QK_WIDTH, None],
         jnp.zeros((ML_QK_WIDTH, LANES - CONV_WIDTH - 1), F32)], axis=1)

    fm, zg, z = _inproj(x2, norm_g.reshape(1, D_MODEL), w_fm, w_tm, cwq_t, conv_w,
                        conv_b.reshape(1, 2 * ML_QK_WIDTH), S)
    rowpack, colpack, aqt, ak = _gates(zg, bias_col, B, S)
    rowpack4 = rowpack.reshape(B, 16, S // ML_L, ML_L)
    oa = _fox(fm, z, aqt, ak, B, S)
    hb = _mlstm(fm, z, colpack, rowpack4, ml_norm_g.reshape(ML_WIDTH, 1), B, S)
    return _out_stage(oa, hb, z, x2, w_fox_down.astype(BF16), w_ml_down.astype(BF16),
                      w_out.astype(BF16), b_gate.reshape(1, 2 * D_MODEL), out_g.reshape(1, D_MODEL))


def kernel(x, norm_g, w_in, b_fox_f, conv_w, conv_b, b_ml_i, b_ml_f, ml_norm_g, b_gate,
           w_fox_down, w_ml_down, w_out, final_g):
    B, S, _ = x.shape
    depth = norm_g.shape[0]
    assert depth == 1, "the final RMSNorm is fused into the (single) layer's output stage"
    out = _layer(x.reshape(B * S, D_MODEL), B, S, norm_g[0], w_in[0], b_fox_f[0], conv_w[0], conv_b[0],
                 b_ml_i[0], b_ml_f[0], ml_norm_g[0], b_gate[0], w_fox_down[0], w_ml_down[0],
                 w_out[0], final_g)
    return out.reshape(B, S, D_MODEL)
```

```python
import functools

import jax
import jax.numpy as jnp
from jax import lax
from jax.experimental import pallas as pl
from jax.experimental.pallas import tpu as pltpu

F32 = jnp.float32
BF16 = jnp.bfloat16

D_MODEL = 1024
FOX_HEADS = 8
FOX_HEAD_DIM = 128
FOX_WIDTH = FOX_HEADS * FOX_HEAD_DIM
ML_HEADS = 4
ML_QK_DIM = 128
ML_V_DIM = 256
ML_QK_WIDTH = ML_HEADS * ML_QK_DIM
ML_WIDTH = ML_HEADS * ML_V_DIM
CONV_WIDTH = 4
EPS = 1e-6

LANES = 128
SUBLANES = 8
BF16_ROWS = 16
GATE_WIDTH = LANES
VMEM_LIMIT = 56 * 1024 * 1024
LOG2E = 1.4426950408889634
FOX_Q_SCALE = FOX_HEAD_DIM ** -0.5 * LOG2E

FR_FQ = 0
FR_FV = FR_FQ + FOX_WIDTH
FR_MV = FR_FV + FOX_WIDTH
FR_MO = FR_MV + ML_WIDTH
FR_MZ = FR_MO + ML_WIDTH
FR_MQ = FR_MZ + ML_WIDTH
FR_GATE = FR_MQ + ML_QK_WIDTH
FM_ROWS = FR_GATE + GATE_WIDTH
FM_CHUNKS = ((FR_FQ, FR_FV), (FR_FV, FR_MV), (FR_MQ, FM_ROWS), (FR_MV, FR_MO), (FR_MO, FR_MZ), (FR_MZ, FR_MQ))
ZB_FK, ZB_FZ, ZB_GA, ZB_GB, ZB_MK = 0, 8, 16, 24, 32
Z_WIDTH = 4608
TM_CHUNKS = ((ZB_MK * LANES, Z_WIDTH), (0, ZB_GA * LANES), (ZB_GA * LANES, ZB_MK * LANES))
TM_IN = 512

GL_FOX, GL_IG, GL_LF = 0, 8, 16
RS_G, RS_M = 0, 8
CL_BETA = 0
AUG_LANES = 6 * FOX_HEADS

GATE_SEQS_PER_STEP = 4
TQ = 256
FOX_HEADS_PER_STEP = 4
FOX_LOOKAHEAD = 4
ML_L = 256
ML_HEADS_PER_STEP = 4
ML_LOOKAHEAD = 2
TM_OUT = 1024
OUT_SUB = 256

NT_DIMS = (((1,), (1,)), ((), ()))


def _dot(a, b):
    return jnp.dot(a, b, preferred_element_type=F32)


O_FQ, O_FK, O_FV = 0, FOX_WIDTH, 2 * FOX_WIDTH
O_FF = 3 * FOX_WIDTH
O_FZ = O_FF + FOX_HEADS
O_MQ = O_FZ + FOX_WIDTH
O_MK = O_MQ + ML_QK_WIDTH
O_MV = O_MK + ML_QK_WIDTH
O_MI = O_MV + ML_WIDTH
O_MF = O_MI + ML_HEADS
O_MO = O_MF + ML_HEADS
O_MZ = O_MO + ML_WIDTH
O_GA = O_MZ + ML_WIDTH
O_GB = O_GA + D_MODEL
IN_WIDTH = O_GB + D_MODEL

PREP_UNIT = 512
FM_GROUPS = ((O_FQ, FOX_WIDTH), (O_FV, FOX_WIDTH), (O_MV, ML_WIDTH), (O_MO, ML_WIDTH),
             (O_MZ, ML_WIDTH), (O_MQ, ML_QK_WIDTH))
TM_GROUPS = ((O_FK, FOX_WIDTH), (O_FZ, FOX_WIDTH), (O_GA, D_MODEL), (O_GB, D_MODEL),
             (O_MK, ML_QK_WIDTH))
KIND_FM, KIND_GATE, KIND_TM = 0, 1, 2


def _prep_units():
    units = [(s, KIND_FM) for off, width in FM_GROUPS for s in range(off, off + width, PREP_UNIT)]
    units.append((0, KIND_GATE))
    units += [(s, KIND_TM) for off, width in TM_GROUPS for s in range(off, off + width, PREP_UNIT)]
    assert all(s % SUBLANES == 0 for s, _ in units)
    return [s // SUBLANES for s, _ in units], [k for _, k in units]


PREP_STARTS, PREP_KINDS = _prep_units()
N_FM_UNITS = PREP_KINDS.index(KIND_GATE) + 1
FM_W_ROWS = N_FM_UNITS * PREP_UNIT


def _prep_kernel(start_ref, kind_ref, w_ref, ff_ref, mimf_ref, fm_ref, tm_ref):
    kind = kind_ref[pl.program_id(0)]

    @pl.when(kind == KIND_FM)
    def _():
        fm_ref[...] = w_ref[...].astype(BF16)

    @pl.when(kind == KIND_TM)
    def _():
        tm_ref[...] = jnp.transpose(w_ref[...]).astype(BF16)

    @pl.when(kind == KIND_GATE)
    def _():
        mimf = mimf_ref[...]
        pad = lambda n: jnp.zeros((n, D_MODEL), F32)
        rows = jnp.concatenate(
            [ff_ref[...], mimf[:ML_HEADS], pad(GL_LF - GL_IG - ML_HEADS), mimf[ML_HEADS:],
             pad(PREP_UNIT - GL_LF - ML_HEADS)], axis=0)
        fm_ref[...] = rows.astype(BF16)


def _prep_weights(w_t):
    assert O_MF == O_MI + ML_HEADS and FOX_HEADS == SUBLANES and 2 * ML_HEADS == SUBLANES
    gate_rows = lambda off: pl.BlockSpec((pl.Element(SUBLANES), pl.Element(D_MODEL)),
                                         lambda u, start, kind: (off, 0))
    grid_spec = pltpu.PrefetchScalarGridSpec(
        num_scalar_prefetch=2,
        grid=(len(PREP_KINDS),),
        in_specs=[
            pl.BlockSpec((pl.Element(PREP_UNIT), pl.Element(D_MODEL)),
                         lambda u, start, kind: (start[u] * SUBLANES, 0)),
            gate_rows(O_FF), gate_rows(O_MI),
        ],
        out_specs=[
            pl.BlockSpec((PREP_UNIT, D_MODEL), lambda u, start, kind: (jnp.minimum(u, N_FM_UNITS - 1), 0)),
            pl.BlockSpec((D_MODEL, PREP_UNIT), lambda u, start, kind: (0, jnp.maximum(u - N_FM_UNITS, 0))),
        ],
    )
    return pl.pallas_call(
        _prep_kernel,
        grid_spec=grid_spec,
        out_shape=[
            jax.ShapeDtypeStruct((FM_W_ROWS, D_MODEL), BF16),
            jax.ShapeDtypeStruct((D_MODEL, Z_WIDTH), BF16),
        ],
        compiler_params=pltpu.CompilerParams(
            dimension_semantics=("arbitrary",), vmem_limit_bytes=VMEM_LIMIT),
        name="prep",
    )(jnp.asarray(PREP_STARTS, jnp.int32), jnp.asarray(PREP_KINDS, jnp.int32), w_t, w_t, w_t)


def _silu(x):
    return x * (1.0 / (1.0 + jnp.exp2(x * -LOG2E)))


def _causal_conv(x, taps, bias, axis, halo):
    def run(xx):
        y = xx * taps[CONV_WIDTH - 1] + bias
        for r in range(1, CONV_WIDTH):
            y = y + pltpu.roll(xx, r, axis) * taps[CONV_WIDTH - 1 - r]
        return y

    edge = halo.shape[axis]
    if axis == 0:
        head = run(jnp.concatenate([halo, x[:edge, :]], axis=0))[edge:, :]
        return jnp.concatenate([head, run(x)[edge:, :]], axis=0)
    head = run(jnp.concatenate([halo, x[:, :edge]], axis=1))[:, edge:]
    return jnp.concatenate([head, run(x)[:, edge:]], axis=1)


def _inproj_kernel(x_ref, g_ref, wfm_ref, wtm_ref, cwq_ref, cwk_ref, cbk_ref,
                   fm_ref, zg_ref, z_ref, qhalo_ref, khalo_ref, *, steps_per_seq):
    @pl.when(pl.program_id(0) % steps_per_seq == 0)
    def _():
        qhalo_ref[...] = jnp.zeros_like(qhalo_ref)
        khalo_ref[...] = jnp.zeros_like(khalo_ref)

    x = x_ref[...]
    ms = jnp.mean(x * x, axis=-1, keepdims=True)
    xh = (x * lax.rsqrt(ms + EPS) * g_ref[...]).astype(BF16)

    def token_major(c0, c1):
        res = _dot(xh, wtm_ref[:, c0:c1])

        def epilogue():
            if c0 == ZB_MK * LANES:
                cwk = cwk_ref[...]
                taps = [cwk[j:j + 1, :] for j in range(CONV_WIDTH)]
                yk = _causal_conv(res, taps, cbk_ref[...], 0, khalo_ref[...])
                khalo_ref[...] = res[TM_IN - SUBLANES:, :]
                z_ref[:, c0:c1] = _silu(yk).astype(BF16)
            else:
                z_ref[:, c0:c1] = res.astype(BF16)
        return epilogue

    def feature_major(r0, r1):
        res = lax.dot_general(wfm_ref[r0:r1, :], xh, NT_DIMS, preferred_element_type=F32)

        def epilogue():
            if r0 == FR_MQ:
                xq = res[:ML_QK_WIDTH, :]
                cwq = cwq_ref[...]
                taps = [cwq[:, j:j + 1] for j in range(CONV_WIDTH)]
                yq = _causal_conv(xq, taps, cwq[:, CONV_WIDTH:CONV_WIDTH + 1], 1, qhalo_ref[...])
                fm_ref[FR_MQ:FR_GATE, :] = (_silu(yq) * (ML_QK_DIM ** -0.5)).astype(BF16)
                qhalo_ref[...] = xq[:, TM_IN - LANES:]
                zg_ref[...] = res[ML_QK_WIDTH:, :]
            elif r0 == FR_FQ:
                fm_ref[r0:r1, :] = (res * FOX_Q_SCALE).astype(BF16)
            else:
                fm_ref[r0:r1, :] = res.astype(BF16)
        return epilogue

    chunks = ([functools.partial(token_major, *TM_CHUNKS[0])]
              + [functools.partial(feature_major, r0, r1) for r0, r1 in FM_CHUNKS]
              + [functools.partial(token_major, c0, c1) for c0, c1 in TM_CHUNKS[1:]])
    pending = None
    for start in chunks:
        epilogue = start()
        if pending is not None:
            pending()
        pending = epilogue
    pending()


def _inproj(x2, norm_g, w_fm, w_tm, cwq_t, conv_w, conv_b, S):
    T = x2.shape[0]
    assert S % TM_IN == 0 and ZB_MK * LANES + ML_QK_WIDTH == Z_WIDTH
    resident = pl.Buffered(1)
    return pl.pallas_call(
        functools.partial(_inproj_kernel, steps_per_seq=S // TM_IN),
        grid=(T // TM_IN,),
        in_specs=[
            pl.BlockSpec((TM_IN, D_MODEL), lambda i: (i, 0)),
            pl.BlockSpec((1, D_MODEL), lambda i: (0, 0)),
            pl.BlockSpec((FM_W_ROWS, D_MODEL), lambda i: (0, 0), pipeline_mode=resident),
            pl.BlockSpec((D_MODEL, Z_WIDTH), lambda i: (0, 0), pipeline_mode=resident),
            pl.BlockSpec((ML_QK_WIDTH, LANES), lambda i: (0, 0)),
            pl.BlockSpec((CONV_WIDTH, ML_QK_WIDTH), lambda i: (0, 1)),
            pl.BlockSpec((1, ML_QK_WIDTH), lambda i: (0, 1)),
        ],
        out_specs=[
            pl.BlockSpec((FR_GATE, TM_IN), lambda i: (0, i)),
            pl.BlockSpec((GATE_WIDTH, TM_IN), lambda i: (0, i)),
            pl.BlockSpec((TM_IN, Z_WIDTH), lambda i: (i, 0)),
        ],
        out_shape=[
            jax.ShapeDtypeStruct((FR_GATE, T), BF16),
            jax.ShapeDtypeStruct((GATE_WIDTH, T), F32),
            jax.ShapeDtypeStruct((T, Z_WIDTH), BF16),
        ],
        scratch_shapes=[
            pltpu.VMEM((ML_QK_WIDTH, LANES), F32),
            pltpu.VMEM((SUBLANES, ML_QK_WIDTH), F32),
        ],
        compiler_params=pltpu.CompilerParams(
            dimension_semantics=("arbitrary",), vmem_limit_bytes=VMEM_LIMIT),
        name="inproj",
    )(x2, norm_g, w_fm, w_tm, cwq_t, conv_w, conv_b)


def _scan_lanes(x, op, ident):
    n = x.shape[1]
    lane = lax.broadcasted_iota(jnp.int32, x.shape, 1)
    k = 1
    while k < n:
        shifted = pltpu.roll(x, k, 1)
        x = op(x, jnp.where(lane >= k, shifted, ident))
        k *= 2
    return x


def _log_sigmoid(x):
    return jnp.minimum(x, 0.0) - jnp.log1p(jnp.exp(-jnp.abs(x)))


def _split3(x):
    hi = x.astype(BF16).astype(F32)
    r = x - hi
    mid = r.astype(BF16).astype(F32)
    lo = (r - mid).astype(BF16).astype(F32)
    return hi, mid, lo


def _gate_kernel(zg_ref, bias_ref, row_ref, col_ref, aq_ref, ak_ref):
    S = row_ref.shape[2]
    for j in range(row_ref.shape[0]):
        t = zg_ref[:, j * S:(j + 1) * S] + bias_ref[...]
        c = _scan_lanes(_log_sigmoid(t[GL_FOX:GL_FOX + 8]), jnp.add, 0.0)
        ig = t[GL_IG:GL_IG + 8]
        fsum = _scan_lanes(_log_sigmoid(t[GL_LF:GL_LF + 8]), jnp.add, 0.0)
        beta = (ig - fsum) * LOG2E
        gmax = jnp.maximum(_scan_lanes(beta, jnp.maximum, -jnp.inf), 0.0)
        m = fsum * LOG2E + gmax
        row_ref[j, RS_G:RS_G + 8, :] = gmax
        row_ref[j, RS_M:RS_M + 8, :] = m
        col_ref[j * S:(j + 1) * S, :] = jnp.transpose(
            jnp.concatenate([beta, jnp.zeros((LANES - 8, S), F32)], axis=0))
        hi, mid, lo = _split3(c * LOG2E)
        ones = jnp.ones((8, S), F32)
        pad = jnp.zeros((LANES - AUG_LANES, S), F32)
        aq_ref[j] = jnp.concatenate([hi, mid, lo, ones, ones, ones, pad], axis=0).astype(BF16)
        ak_ref[j * S:(j + 1) * S, :] = jnp.transpose(
            jnp.concatenate([ones, ones, ones, -hi, -mid, -lo, pad], axis=0)).astype(BF16)


def _gates(zg, bias_col, B, S):
    nb = GATE_SEQS_PER_STEP
    return pl.pallas_call(
        _gate_kernel,
        grid=(B // nb,),
        in_specs=[
            pl.BlockSpec((GATE_WIDTH, nb * S), lambda b: (0, b)),
            pl.BlockSpec((GATE_WIDTH, 1), lambda b: (0, 0)),
        ],
        out_specs=[
            pl.BlockSpec((nb, 16, S), lambda b: (b, 0, 0)),
            pl.BlockSpec((nb * S, LANES), lambda b: (b, 0)),
            pl.BlockSpec((nb, LANES, S), lambda b: (b, 0, 0)),
            pl.BlockSpec((nb * S, LANES), lambda b: (b, 0)),
        ],
        out_shape=[
            jax.ShapeDtypeStruct((B, 16, S), F32),
            jax.ShapeDtypeStruct((B * S, LANES), F32),
            jax.ShapeDtypeStruct((B, LANES, S), BF16),
            jax.ShapeDtypeStruct((B * S, LANES), BF16),
        ],
        compiler_params=pltpu.CompilerParams(
            dimension_semantics=("arbitrary",), vmem_limit_bytes=VMEM_LIMIT),
        name="gates",
    )(zg, bias_col)


def _fox_kernel(qt_ref, k_ref, vt_ref, fz_ref, aqt_ref, ak_ref, o_ref, qaug_ref, kaug_ref, vaug_ref):
    S = k_ref.shape[0]
    d = FOX_HEAD_DIM
    hs = range(FOX_HEADS_PER_STEP)
    lane = lax.broadcasted_iota(jnp.int32, (1, LANES), 1)
    for j in hs:
        h = pl.program_id(1) * FOX_HEADS_PER_STEP + j
        keep = jnp.where(((lane & (FOX_HEADS - 1)) == h) & (lane < AUG_LANES), 1.0, 0.0).astype(BF16)
        qaug_ref[j, :d, :] = qt_ref[j * d:(j + 1) * d, :]
        qaug_ref[j, d:, :] = aqt_ref[...]
        kaug_ref[j, :, :LANES] = k_ref[:, j * LANES:(j + 1) * LANES]
        kaug_ref[j, :, LANES:] = ak_ref[...] * keep
        vaug_ref[j, :d, :] = vt_ref[j * d:(j + 1) * d, :]
        vaug_ref[j, d:, :] = jnp.ones((BF16_ROWS, S), BF16)
    s_idx = lax.broadcasted_iota(jnp.int32, (TQ, TQ), 0)
    t_idx = lax.broadcasted_iota(jnp.int32, (TQ, TQ), 1)
    causal = s_idx <= t_idx

    ng = S // TQ
    m = [[None] * ng for _ in hs]
    acc = [[None] * ng for _ in hs]
    tiles = [(j, kj, g) for kj in range(ng) for g in range(kj, ng) for j in hs]

    def scores(j, kj, g):
        ka = kaug_ref[j, kj * TQ:(kj + 1) * TQ, :]
        qa = qaug_ref[j, :, g * TQ:(g + 1) * TQ]
        return _dot(ka, qa)

    def finish(j, kj, g, st):
        va = vaug_ref[j, :, kj * TQ:(kj + 1) * TQ]
        if kj == g:
            st = jnp.where(causal, st, -jnp.inf)
        cm = jnp.max(st, axis=0, keepdims=True)
        if kj == 0:
            m[j][g] = cm
            acc[j][g] = _dot(va, jnp.exp2(st - cm).astype(BF16))
        else:
            m_new = jnp.maximum(m[j][g], cm)
            alpha = jnp.exp2(m[j][g] - m_new)
            acc[j][g] = alpha * acc[j][g] + _dot(va, jnp.exp2(st - m_new).astype(BF16))
            m[j][g] = m_new
        if kj == g:
            a = acc[j][g]
            o = jnp.transpose(a[:d] * (1.0 / a[d:d + 1]))
            fz = fz_ref[g * TQ:(g + 1) * TQ, j * LANES:(j + 1) * LANES].astype(F32)
            o_ref[g * TQ:(g + 1) * TQ, j * LANES:(j + 1) * LANES] = (o * _silu(fz)).astype(BF16)

    pending = [scores(*tiles[i]) for i in range(FOX_LOOKAHEAD)]
    for i, tile in enumerate(tiles):
        if i + FOX_LOOKAHEAD < len(tiles):
            pending.append(scores(*tiles[i + FOX_LOOKAHEAD]))
        finish(*tile, pending.pop(0))


def _fox(fm, z, aqt, ak, B, S):
    n = FOX_HEADS_PER_STEP
    dn = n * FOX_HEAD_DIM
    assert FR_FQ % dn == 0 and FR_FV % dn == 0 and ZB_FK % n == 0 and ZB_FZ % n == 0
    return pl.pallas_call(
        _fox_kernel,
        grid=(B, FOX_HEADS // n),
        in_specs=[
            pl.BlockSpec((dn, S), lambda b, h: (FR_FQ // dn + h, b)),
            pl.BlockSpec((S, dn), lambda b, h: (b, ZB_FK // n + h)),
            pl.BlockSpec((dn, S), lambda b, h: (FR_FV // dn + h, b)),
            pl.BlockSpec((S, dn), lambda b, h: (b, ZB_FZ // n + h)),
            pl.BlockSpec((None, LANES, S), lambda b, h: (b, 0, 0)),
            pl.BlockSpec((S, LANES), lambda b, h: (b, 0)),
        ],
        out_specs=pl.BlockSpec((S, dn), lambda b, h: (b, h)),
        out_shape=jax.ShapeDtypeStruct((B * S, FOX_WIDTH), BF16),
        scratch_shapes=[
            pltpu.VMEM((n, 2 * LANES, S), BF16),
            pltpu.VMEM((n, S, 2 * LANES), BF16),
            pltpu.VMEM((n, FOX_HEAD_DIM + BF16_ROWS, S), BF16),
        ],
        compiler_params=pltpu.CompilerParams(
            dimension_semantics=("arbitrary", "arbitrary"), vmem_limit_bytes=VMEM_LIMIT),
        name="fox",
    )(fm, z, fm, z, aqt, ak)


def _mlstm_kernel(qt_ref, k_ref, vt_ref, mot_ref, mzt_ref, col_ref, grow_ref, mrow_ref, ng_ref,
                  o_ref, vaug_ref):
    S = k_ref.shape[0]
    L = ML_L
    nc = S // L
    dk, dv = ML_QK_DIM, ML_V_DIM
    hs = range(ML_HEADS_PER_STEP)
    for j in hs:
        vaug_ref[j, :dv, :] = vt_ref[j * dv:(j + 1) * dv, :]
        vaug_ref[j, dv:, :] = jnp.ones((BF16_ROWS, S), BF16)
    s_idx = lax.broadcasted_iota(jnp.int32, (L, L), 0)
    t_idx = lax.broadcasted_iota(jnp.int32, (L, L), 1)
    causal = s_idx <= t_idx
    ng = [ng_ref[j * dv:(j + 1) * dv, :] for j in hs]
    head_shift = [(LANES - (pl.program_id(1) * ML_HEADS_PER_STEP + j)) % LANES for j in hs]

    def scores(j, c):
        kc = k_ref[c * L:(c + 1) * L, j * dk:(j + 1) * dk]
        return _dot(kc, qt_ref[j * dk:(j + 1) * dk, c * L:(c + 1) * L])

    def finish(j, c, st, state):
        ct_aug, g_prev = state
        cols = slice(c * L, (c + 1) * L)
        feats = slice(j * dv, (j + 1) * dv)
        b_col = pltpu.roll(col_ref[cols, :], head_shift[j], 1)[:, CL_BETA:CL_BETA + 1]
        g_row = grow_ref[j, c:c + 1, :]
        g_last = g_row[:, L - 1:L]
        va = vaug_ref[j, :, cols]
        kw = k_ref[cols, j * dk:(j + 1) * dk].astype(F32) * jnp.exp2(b_col - g_last)
        upd = _dot(va, kw.astype(BF16))
        sw = st * jnp.where(causal, jnp.exp2(b_col - g_row), 0.0)
        num = _dot(va, sw.astype(BF16))
        if c > 0:
            w_inter = jnp.exp2(g_prev - g_row)
            num = num + w_inter * _dot(ct_aug.astype(BF16), qt_ref[j * dk:(j + 1) * dk, cols])
        den = num[dv:dv + 1, :]
        num = num[:dv, :]
        inv = 1.0 / jnp.maximum(jnp.abs(den), jnp.exp2(-mrow_ref[j, c:c + 1, :]))
        ssq = jnp.sum(num * num, axis=0, keepdims=True)
        t_scale = inv * lax.rsqrt(inv * inv * ssq * (1.0 / dv) + EPS)
        mo = mot_ref[feats, cols].astype(F32)
        mz = mzt_ref[feats, cols].astype(F32)
        gate = mz * (1.0 / ((1.0 + jnp.exp2(mo * -LOG2E)) * (1.0 + jnp.exp2(mz * -LOG2E))))
        o_ref[cols, feats] = jnp.transpose(num * t_scale * (ng[j] * gate)).astype(BF16)
        if c == 0:
            return upd, g_last
        return jnp.exp2(g_prev - g_last) * ct_aug + upd, g_last

    work = [(j, c) for c in range(nc) for j in hs]
    pending = [scores(*work[i]) for i in range(ML_LOOKAHEAD)]
    state = [(None, None) for _ in hs]
    for i, (j, c) in enumerate(work):
        if i + ML_LOOKAHEAD < len(work):
            pending.append(scores(*work[i + ML_LOOKAHEAD]))
        state[j] = finish(j, c, pending.pop(0), state[j])


def _mlstm(fm, z, colpack, rowpack4, ng_col, B, S):
    nc = S // ML_L
    n = ML_HEADS_PER_STEP
    dk, dv = n * ML_QK_DIM, n * ML_V_DIM
    assert FR_MQ % dk == 0 and (ZB_MK * LANES) % dk == 0 and RS_G % n == 0 and RS_M % n == 0
    assert FR_MV % dv == 0 and FR_MO % dv == 0 and FR_MZ % dv == 0
    row_spec = lambda slab: pl.BlockSpec((None, n, nc, ML_L), lambda b, h: (b, slab // n + h, 0, 0))
    return pl.pallas_call(
        _mlstm_kernel,
        grid=(B, ML_HEADS // n),
        in_specs=[
            pl.BlockSpec((dk, S), lambda b, h: (FR_MQ // dk + h, b)),
            pl.BlockSpec((S, dk), lambda b, h: (b, ZB_MK * LANES // dk + h)),
            pl.BlockSpec((dv, S), lambda b, h: (FR_MV // dv + h, b)),
            pl.BlockSpec((dv, S), lambda b, h: (FR_MO // dv + h, b)),
            pl.BlockSpec((dv, S), lambda b, h: (FR_MZ // dv + h, b)),
            pl.BlockSpec((S, LANES), lambda b, h: (b, 0)),
            row_spec(RS_G), row_spec(RS_M),
            pl.BlockSpec((dv, 1), lambda b, h: (h, 0)),
        ],
        out_specs=pl.BlockSpec((S, dv), lambda b, h: (b, h)),
        out_shape=jax.ShapeDtypeStruct((B * S, ML_WIDTH), BF16),
        scratch_shapes=[pltpu.VMEM((n, ML_V_DIM + BF16_ROWS, S), BF16)],
        compiler_params=pltpu.CompilerParams(
            dimension_semantics=("arbitrary", "arbitrary"), vmem_limit_bytes=VMEM_LIMIT),
        name="mlstm",
    )(fm, z, fm, fm, fm, colpack, rowpack4, rowpack4, ng_col)


def _out_kernel(oa_ref, hb_ref, ga_ref, gb_ref, x_ref, wfd_ref, wmd_ref, wo_ref, bga_ref, bgb_ref,
                fg_ref, o_ref):
    def branches(s):
        rows = slice(s * OUT_SUB, (s + 1) * OUT_SUB)
        return _dot(oa_ref[rows, :], wfd_ref[...]), _dot(hb_ref[rows, :], wmd_ref[...])

    def finish(s, ya, yb):
        rows = slice(s * OUT_SUB, (s + 1) * OUT_SUB)
        gate_a = jax.nn.sigmoid(ga_ref[rows, :].astype(F32) + bga_ref[...])
        gate_b = jax.nn.sigmoid(gb_ref[rows, :].astype(F32) + bgb_ref[...])
        y = gate_a * ya + gate_b * yb
        r = x_ref[rows, :] + _dot(y.astype(BF16), wo_ref[...])
        ms = jnp.mean(r * r, axis=-1, keepdims=True)
        o_ref[rows, :] = r * lax.rsqrt(ms + EPS) * fg_ref[...]

    n_sub = TM_OUT // OUT_SUB
    pending = branches(0)
    for s in range(n_sub):
        nxt = branches(s + 1) if s + 1 < n_sub else None
        finish(s, *pending)
        pending = nxt


def _out_stage(oa, hb, z, x2, wfd, wmd, wo, b_gate, final_g):
    T = x2.shape[0]
    row = lambda i: (i, 0)
    const = lambda i: (0, 0)
    return pl.pallas_call(
        _out_kernel,
        grid=(T // TM_OUT,),
        in_specs=[
            pl.BlockSpec((TM_OUT, FOX_WIDTH), row),
            pl.BlockSpec((TM_OUT, ML_WIDTH), row),
            pl.BlockSpec((TM_OUT, D_MODEL), lambda i: (i, ZB_GA // 8)),
            pl.BlockSpec((TM_OUT, D_MODEL), lambda i: (i, ZB_GB // 8)),
            pl.BlockSpec((TM_OUT, D_MODEL), row),
            pl.BlockSpec((FOX_WIDTH, D_MODEL), const, pipeline_mode=pl.Buffered(1)),
            pl.BlockSpec((ML_WIDTH, D_MODEL), const, pipeline_mode=pl.Buffered(1)),
            pl.BlockSpec((D_MODEL, D_MODEL), const, pipeline_mode=pl.Buffered(1)),
            pl.BlockSpec((1, D_MODEL), lambda i: (0, 0)),
            pl.BlockSpec((1, D_MODEL), lambda i: (0, 1)),
            pl.BlockSpec((1, D_MODEL), const),
        ],
        out_specs=pl.BlockSpec((TM_OUT, D_MODEL), row),
        out_shape=jax.ShapeDtypeStruct((T, D_MODEL), F32),
        compiler_params=pltpu.CompilerParams(
            dimension_semantics=("arbitrary",), vmem_limit_bytes=VMEM_LIMIT),
        name="outstage",
    )(oa, hb, z, z, x2, wfd, wmd, wo, b_gate, b_gate, final_g)


def _layer(x2, B, S, norm_g, w_in, b_fox_f, conv_w, conv_b, b_ml_i, b_ml_f, ml_norm_g, b_gate,
           w_fox_down, w_ml_down, w_out, out_g):
    assert w_in.shape == (D_MODEL, IN_WIDTH)
    w_fm, w_tm = _prep_weights(jnp.transpose(w_in))
    zrow = lambda n: jnp.zeros((n,), F32)
    bias_col = jnp.concatenate(
        [b_fox_f, b_ml_i, zrow(8 - ML_HEADS), b_ml_f, zrow(GATE_WIDTH - GL_LF - ML_HEADS)]).reshape(GATE_WIDTH, 1)
    cwq_t = jnp.concatenate(
        [jnp.transpose(conv_w[:, :ML_QK_WIDTH]), conv_b[:ML_QK_WIDTH, None],
         jnp.zeros((ML_QK_WIDTH, LANES - CONV_WIDTH - 1), F32)], axis=1)

    fm, zg, z = _inproj(x2, norm_g.reshape(1, D_MODEL), w_fm, w_tm, cwq_t, conv_w,
                        conv_b.reshape(1, 2 * ML_QK_WIDTH), S)
    rowpack, colpack, aqt, ak = _gates(zg, bias_col, B, S)
    rowpack4 = rowpack.reshape(B, 16, S // ML_L, ML_L)
    oa = _fox(fm, z, aqt, ak, B, S)
    hb = _mlstm(fm, z, colpack, rowpack4, ml_norm_g.reshape(ML_WIDTH, 1), B, S)
    return _out_stage(oa, hb, z, x2, w_fox_down.astype(BF16), w_ml_down.astype(BF16),
                      w_out.astype(BF16), b_gate.reshape(1, 2 * D_MODEL), out_g.reshape(1, D_MODEL))


def kernel(x, norm_g, w_in, b_fox_f, conv_w, conv_b, b_ml_i, b_ml_f, ml_norm_g, b_gate,
           w_fox_down, w_ml_down, w_out, final_g):
    B, S, _ = x.shape
    depth = norm_g.shape[0]
    assert depth == 1, "the final RMSNorm is fused into the (single) layer's output stage"
    out = _layer(x.reshape(B * S, D_MODEL), B, S, norm_g[0], w_in[0], b_fox_f[0], conv_w[0], conv_b[0],
                 b_ml_i[0], b_ml_f[0], ml_norm_g[0], b_gate[0], w_fox_down[0], w_ml_down[0],
                 w_out[0], final_g)
    return out.reshape(B, S, D_MODEL)
```

```python
import functools

import jax
import jax.numpy as jnp
from jax import lax
from jax.experimental import pallas as pl
from jax.experimental.pallas import tpu as pltpu

F32 = jnp.float32
BF16 = jnp.bfloat16

D_MODEL = 1024
FOX_HEADS = 8
FOX_HEAD_DIM = 128
FOX_WIDTH = FOX_HEADS * FOX_HEAD_DIM
ML_HEADS = 4
ML_QK_DIM = 128
ML_V_DIM = 256
ML_QK_WIDTH = ML_HEADS * ML_QK_DIM
ML_WIDTH = ML_HEADS * ML_V_DIM
CONV_WIDTH = 4
EPS = 1e-6

LANES = 128
SUBLANES = 8
BF16_ROWS = 16
GATE_WIDTH = LANES
VMEM_LIMIT = 56 * 1024 * 1024
LOG2E = 1.4426950408889634
FOX_Q_SCALE = FOX_HEAD_DIM ** -0.5 * LOG2E

FR_FQ = 0
FR_FV = FR_FQ + FOX_WIDTH
FR_MV = FR_FV + FOX_WIDTH
FR_MO = FR_MV + ML_WIDTH
FR_MZ = FR_MO + ML_WIDTH
FR_MQ = FR_MZ + ML_WIDTH
FR_GATE = FR_MQ + ML_QK_WIDTH
FM_ROWS = FR_GATE + GATE_WIDTH
FM_CHUNKS = ((FR_FQ, FR_FV), (FR_FV, FR_MV), (FR_MQ, FM_ROWS), (FR_MV, FR_MO), (FR_MO, FR_MZ), (FR_MZ, FR_MQ))
ZB_FK, ZB_FZ, ZB_GA, ZB_GB, ZB_MK = 0, 8, 16, 24, 32
Z_WIDTH = 4608
TM_CHUNKS = ((ZB_MK * LANES, Z_WIDTH), (0, ZB_GA * LANES), (ZB_GA * LANES, ZB_MK * LANES))
TM_IN = 512

GL_FOX, GL_IG, GL_LF = 0, 8, 16
RS_BETA, RS_G, RS_M = 0, 8, 16
ROW_SLABS = 24
AUG_LANES = 6 * FOX_HEADS

GATE_SEQS_PER_STEP = 4
TQ = 256
FOX_HEADS_PER_STEP = 4
FOX_LOOKAHEAD = 4
ML_L = 256
ML_HEADS_PER_STEP = 4
ML_LOOKAHEAD = 2
TM_OUT = 1024
OUT_SUB = 256

NT_DIMS = (((1,), (1,)), ((), ()))


def _dot(a, b):
    return jnp.dot(a, b, preferred_element_type=F32)


O_FQ, O_FK, O_FV = 0, FOX_WIDTH, 2 * FOX_WIDTH
O_FF = 3 * FOX_WIDTH
O_FZ = O_FF + FOX_HEADS
O_MQ = O_FZ + FOX_WIDTH
O_MK = O_MQ + ML_QK_WIDTH
O_MV = O_MK + ML_QK_WIDTH
O_MI = O_MV + ML_WIDTH
O_MF = O_MI + ML_HEADS
O_MO = O_MF + ML_HEADS
O_MZ = O_MO + ML_WIDTH
O_GA = O_MZ + ML_WIDTH
O_GB = O_GA + D_MODEL
IN_WIDTH = O_GB + D_MODEL

PREP_UNIT = 512
FM_GROUPS = ((O_FQ, FOX_WIDTH), (O_FV, FOX_WIDTH), (O_MV, ML_WIDTH), (O_MO, ML_WIDTH),
             (O_MZ, ML_WIDTH), (O_MQ, ML_QK_WIDTH))
TM_GROUPS = ((O_FK, FOX_WIDTH), (O_FZ, FOX_WIDTH), (O_GA, D_MODEL), (O_GB, D_MODEL),
             (O_MK, ML_QK_WIDTH))
KIND_FM, KIND_GATE, KIND_TM = 0, 1, 2


def _prep_units():
    units = [(s, KIND_FM) for off, width in FM_GROUPS for s in range(off, off + width, PREP_UNIT)]
    units.append((0, KIND_GATE))
    units += [(s, KIND_TM) for off, width in TM_GROUPS for s in range(off, off + width, PREP_UNIT)]
    assert all(s % SUBLANES == 0 for s, _ in units)
    return [s // SUBLANES for s, _ in units], [k for _, k in units]


PREP_STARTS, PREP_KINDS = _prep_units()
N_FM_UNITS = PREP_KINDS.index(KIND_GATE) + 1
FM_W_ROWS = N_FM_UNITS * PREP_UNIT


def _prep_kernel(start_ref, kind_ref, w_ref, ff_ref, mimf_ref, fm_ref, tm_ref):
    kind = kind_ref[pl.program_id(0)]

    @pl.when(kind == KIND_FM)
    def _():
        fm_ref[...] = w_ref[...].astype(BF16)

    @pl.when(kind == KIND_TM)
    def _():
        tm_ref[...] = jnp.transpose(w_ref[...]).astype(BF16)

    @pl.when(kind == KIND_GATE)
    def _():
        mimf = mimf_ref[...]
        pad = lambda n: jnp.zeros((n, D_MODEL), F32)
        rows = jnp.concatenate(
            [ff_ref[...], mimf[:ML_HEADS], pad(GL_LF - GL_IG - ML_HEADS), mimf[ML_HEADS:],
             pad(PREP_UNIT - GL_LF - ML_HEADS)], axis=0)
        fm_ref[...] = rows.astype(BF16)


def _prep_weights(w_t):
    assert O_MF == O_MI + ML_HEADS and FOX_HEADS == SUBLANES and 2 * ML_HEADS == SUBLANES
    gate_rows = lambda off: pl.BlockSpec((pl.Element(SUBLANES), pl.Element(D_MODEL)),
                                         lambda u, start, kind: (off, 0))
    grid_spec = pltpu.PrefetchScalarGridSpec(
        num_scalar_prefetch=2,
        grid=(len(PREP_KINDS),),
        in_specs=[
            pl.BlockSpec((pl.Element(PREP_UNIT), pl.Element(D_MODEL)),
                         lambda u, start, kind: (start[u] * SUBLANES, 0)),
            gate_rows(O_FF), gate_rows(O_MI),
        ],
        out_specs=[
            pl.BlockSpec((PREP_UNIT, D_MODEL), lambda u, start, kind: (jnp.minimum(u, N_FM_UNITS - 1), 0)),
            pl.BlockSpec((D_MODEL, PREP_UNIT), lambda u, start, kind: (0, jnp.maximum(u - N_FM_UNITS, 0))),
        ],
    )
    return pl.pallas_call(
        _prep_kernel,
        grid_spec=grid_spec,
        out_shape=[
            jax.ShapeDtypeStruct((FM_W_ROWS, D_MODEL), BF16),
            jax.ShapeDtypeStruct((D_MODEL, Z_WIDTH), BF16),
        ],
        compiler_params=pltpu.CompilerParams(
            dimension_semantics=("arbitrary",), vmem_limit_bytes=VMEM_LIMIT),
        name="prep",
    )(jnp.asarray(PREP_STARTS, jnp.int32), jnp.asarray(PREP_KINDS, jnp.int32), w_t, w_t, w_t)


def _silu(x):
    return x * (1.0 / (1.0 + jnp.exp2(x * -LOG2E)))


def _causal_conv(x, taps, bias, axis, halo):
    def run(xx):
        y = xx * taps[CONV_WIDTH - 1] + bias
        for r in range(1, CONV_WIDTH):
            y = y + pltpu.roll(xx, r, axis) * taps[CONV_WIDTH - 1 - r]
        return y

    edge = halo.shape[axis]
    if axis == 0:
        head = run(jnp.concatenate([halo, x[:edge, :]], axis=0))[edge:, :]
        return jnp.concatenate([head, run(x)[edge:, :]], axis=0)
    head = run(jnp.concatenate([halo, x[:, :edge]], axis=1))[:, edge:]
    return jnp.concatenate([head, run(x)[:, edge:]], axis=1)


def _inproj_kernel(x_ref, g_ref, wfm_ref, wtm_ref, cwq_ref, cwk_ref, cbk_ref,
                   fm_ref, zg_ref, z_ref, qhalo_ref, khalo_ref, *, steps_per_seq):
    @pl.when(pl.program_id(0) % steps_per_seq == 0)
    def _():
        qhalo_ref[...] = jnp.zeros_like(qhalo_ref)
        khalo_ref[...] = jnp.zeros_like(khalo_ref)

    x = x_ref[...]
    ms = jnp.mean(x * x, axis=-1, keepdims=True)
    xh = (x * lax.rsqrt(ms + EPS) * g_ref[...]).astype(BF16)

    def token_major(c0, c1):
        res = _dot(xh, wtm_ref[:, c0:c1])

        def epilogue():
            if c0 == ZB_MK * LANES:
                cwk = cwk_ref[...]
                taps = [cwk[j:j + 1, :] for j in range(CONV_WIDTH)]
                yk = _causal_conv(res, taps, cbk_ref[...], 0, khalo_ref[...])
                khalo_ref[...] = res[TM_IN - SUBLANES:, :]
                z_ref[:, c0:c1] = _silu(yk).astype(BF16)
            else:
                z_ref[:, c0:c1] = res.astype(BF16)
        return epilogue

    def feature_major(r0, r1):
        res = lax.dot_general(wfm_ref[r0:r1, :], xh, NT_DIMS, preferred_element_type=F32)

        def epilogue():
            if r0 == FR_MQ:
                xq = res[:ML_QK_WIDTH, :]
                cwq = cwq_ref[...]
                taps = [cwq[:, j:j + 1] for j in range(CONV_WIDTH)]
                yq = _causal_conv(xq, taps, cwq[:, CONV_WIDTH:CONV_WIDTH + 1], 1, qhalo_ref[...])
                fm_ref[FR_MQ:FR_GATE, :] = (_silu(yq) * (ML_QK_DIM ** -0.5)).astype(BF16)
                qhalo_ref[...] = xq[:, TM_IN - LANES:]
                zg_ref[...] = res[ML_QK_WIDTH:, :]
            elif r0 == FR_FQ:
                fm_ref[r0:r1, :] = (res * FOX_Q_SCALE).astype(BF16)
            else:
                fm_ref[r0:r1, :] = res.astype(BF16)
        return epilogue

    chunks = ([functools.partial(token_major, *TM_CHUNKS[0])]
              + [functools.partial(feature_major, r0, r1) for r0, r1 in FM_CHUNKS]
              + [functools.partial(token_major, c0, c1) for c0, c1 in TM_CHUNKS[1:]])
    pending = None
    for start in chunks:
        epilogue = start()
        if pending is not None:
            pending()
        pending = epilogue
    pending()


def _inproj(x2, norm_g, w_fm, w_tm, cwq_t, conv_w, conv_b, S):
    T = x2.shape[0]
    assert S % TM_IN == 0 and ZB_MK * LANES + ML_QK_WIDTH == Z_WIDTH
    resident = pl.Buffered(1)
    return pl.pallas_call(
        functools.partial(_inproj_kernel, steps_per_seq=S // TM_IN),
        grid=(T // TM_IN,),
        in_specs=[
            pl.BlockSpec((TM_IN, D_MODEL), lambda i: (i, 0)),
            pl.BlockSpec((1, D_MODEL), lambda i: (0, 0)),
            pl.BlockSpec((FM_W_ROWS, D_MODEL), lambda i: (0, 0), pipeline_mode=resident),
            pl.BlockSpec((D_MODEL, Z_WIDTH), lambda i: (0, 0), pipeline_mode=resident),
            pl.BlockSpec((ML_QK_WIDTH, LANES), lambda i: (0, 0)),
            pl.BlockSpec((CONV_WIDTH, ML_QK_WIDTH), lambda i: (0, 1)),
            pl.BlockSpec((1, ML_QK_WIDTH), lambda i: (0, 1)),
        ],
        out_specs=[
            pl.BlockSpec((FR_GATE, TM_IN), lambda i: (0, i)),
            pl.BlockSpec((GATE_WIDTH, TM_IN), lambda i: (0, i)),
            pl.BlockSpec((TM_IN, Z_WIDTH), lambda i: (i, 0)),
        ],
        out_shape=[
            jax.ShapeDtypeStruct((FR_GATE, T), BF16),
            jax.ShapeDtypeStruct((GATE_WIDTH, T), F32),
            jax.ShapeDtypeStruct((T, Z_WIDTH), BF16),
        ],
        scratch_shapes=[
            pltpu.VMEM((ML_QK_WIDTH, LANES), F32),
            pltpu.VMEM((SUBLANES, ML_QK_WIDTH), F32),
        ],
        compiler_params=pltpu.CompilerParams(
            dimension_semantics=("arbitrary",), vmem_limit_bytes=VMEM_LIMIT),
        name="inproj",
    )(x2, norm_g, w_fm, w_tm, cwq_t, conv_w, conv_b)


def _scan_lanes(x, op, ident):
    n = x.shape[1]
    lane = lax.broadcasted_iota(jnp.int32, x.shape, 1)
    k = 1
    while k < n:
        shifted = pltpu.roll(x, k, 1)
        x = op(x, jnp.where(lane >= k, shifted, ident))
        k *= 2
    return x


def _log_sigmoid(x):
    return jnp.minimum(x, 0.0) - jnp.log1p(jnp.exp(-jnp.abs(x)))


def _split3(x):
    hi = x.astype(BF16).astype(F32)
    r = x - hi
    mid = r.astype(BF16).astype(F32)
    lo = (r - mid).astype(BF16).astype(F32)
    return hi, mid, lo


def _gate_kernel(zg_ref, bias_ref, row_ref, aq_ref):
    S = aq_ref.shape[2]
    for j in range(aq_ref.shape[0]):
        t = zg_ref[:, j * S:(j + 1) * S] + bias_ref[...]
        c = _scan_lanes(_log_sigmoid(t[GL_FOX:GL_FOX + 8]), jnp.add, 0.0)
        ig = t[GL_IG:GL_IG + 8]
        fsum = _scan_lanes(_log_sigmoid(t[GL_LF:GL_LF + 8]), jnp.add, 0.0)
        beta = (ig - fsum) * LOG2E
        gmax = jnp.maximum(_scan_lanes(beta, jnp.maximum, -jnp.inf), 0.0)
        m = fsum * LOG2E + gmax
        for ch in range(S // ML_L):
            t_ch = slice(ch * ML_L, (ch + 1) * ML_L)
            row_ref[j, ch, RS_BETA:RS_BETA + 8, :] = beta[:, t_ch]
            row_ref[j, ch, RS_G:RS_G + 8, :] = gmax[:, t_ch]
            row_ref[j, ch, RS_M:RS_M + 8, :] = m[:, t_ch]
        hi, mid, lo = _split3(c * LOG2E)
        ones = jnp.ones((8, S), F32)
        pad = jnp.zeros((LANES - AUG_LANES, S), F32)
        aq_ref[j] = jnp.concatenate([hi, mid, lo, ones, ones, ones, pad], axis=0).astype(BF16)


def _gates(zg, bias_col, B, S):
    nb = GATE_SEQS_PER_STEP
    return pl.pallas_call(
        _gate_kernel,
        grid=(B // nb,),
        in_specs=[
            pl.BlockSpec((GATE_WIDTH, nb * S), lambda b: (0, b)),
            pl.BlockSpec((GATE_WIDTH, 1), lambda b: (0, 0)),
        ],
        out_specs=[
            pl.BlockSpec((nb, S // ML_L, ROW_SLABS, ML_L), lambda b: (b, 0, 0, 0)),
            pl.BlockSpec((nb, LANES, S), lambda b: (b, 0, 0)),
        ],
        out_shape=[
            jax.ShapeDtypeStruct((B, S // ML_L, ROW_SLABS, ML_L), F32),
            jax.ShapeDtypeStruct((B, LANES, S), BF16),
        ],
        compiler_params=pltpu.CompilerParams(
            dimension_semantics=("arbitrary",), vmem_limit_bytes=VMEM_LIMIT),
        name="gates",
    )(zg, bias_col)


def _fox_kernel(qt_ref, k_ref, vt_ref, fz_ref, aqt_ref, o_ref, qaug_ref, kaug_ref, vaug_ref):
    S = k_ref.shape[0]
    d = FOX_HEAD_DIM
    hs = range(FOX_HEADS_PER_STEP)
    lane = lax.broadcasted_iota(jnp.int32, (1, LANES), 1)
    aq = aqt_ref[...]
    half = AUG_LANES // 2
    ak = jnp.transpose(jnp.concatenate([aq[half:AUG_LANES], -aq[:half], aq[AUG_LANES:]], axis=0))
    for j in hs:
        h = pl.program_id(1) * FOX_HEADS_PER_STEP + j
        keep = jnp.where(((lane & (FOX_HEADS - 1)) == h) & (lane < AUG_LANES), 1.0, 0.0).astype(BF16)
        qaug_ref[j, :d, :] = qt_ref[j * d:(j + 1) * d, :]
        qaug_ref[j, d:, :] = aqt_ref[...]
        kaug_ref[j, :, :LANES] = k_ref[:, j * LANES:(j + 1) * LANES]
        kaug_ref[j, :, LANES:] = ak * keep
        vaug_ref[j, :d, :] = vt_ref[j * d:(j + 1) * d, :]
        vaug_ref[j, d:, :] = jnp.ones((BF16_ROWS, S), BF16)
    s_idx = lax.broadcasted_iota(jnp.int32, (TQ, TQ), 0)
    t_idx = lax.broadcasted_iota(jnp.int32, (TQ, TQ), 1)
    causal = s_idx <= t_idx

    ng = S // TQ
    m = [[None] * ng for _ in hs]
    acc = [[None] * ng for _ in hs]
    tiles = [(j, kj, g) for kj in range(ng) for g in range(kj, ng) for j in hs]

    def scores(j, kj, g):
        ka = kaug_ref[j, kj * TQ:(kj + 1) * TQ, :]
        qa = qaug_ref[j, :, g * TQ:(g + 1) * TQ]
        return _dot(ka, qa)

    def finish(j, kj, g, st):
        va = vaug_ref[j, :, kj * TQ:(kj + 1) * TQ]
        if kj == g:
            st = jnp.where(causal, st, -jnp.inf)
        cm = jnp.max(st, axis=0, keepdims=True)
        if kj == 0:
            m[j][g] = cm
            acc[j][g] = _dot(va, jnp.exp2(st - cm).astype(BF16))
        else:
            m_new = jnp.maximum(m[j][g], cm)
            alpha = jnp.exp2(m[j][g] - m_new)
            acc[j][g] = alpha * acc[j][g] + _dot(va, jnp.exp2(st - m_new).astype(BF16))
            m[j][g] = m_new
        if kj == g:
            a = acc[j][g]
            o = jnp.transpose(a[:d] * (1.0 / a[d:d + 1]))
            fz = fz_ref[g * TQ:(g + 1) * TQ, j * LANES:(j + 1) * LANES].astype(F32)
            o_ref[g * TQ:(g + 1) * TQ, j * LANES:(j + 1) * LANES] = (o * _silu(fz)).astype(BF16)

    pending = [scores(*tiles[i]) for i in range(FOX_LOOKAHEAD)]
    for i, tile in enumerate(tiles):
        if i + FOX_LOOKAHEAD < len(tiles):
            pending.append(scores(*tiles[i + FOX_LOOKAHEAD]))
        finish(*tile, pending.pop(0))


def _fox(fm, z, aqt, B, S):
    n = FOX_HEADS_PER_STEP
    dn = n * FOX_HEAD_DIM
    assert FR_FQ % dn == 0 and FR_FV % dn == 0 and ZB_FK % n == 0 and ZB_FZ % n == 0
    return pl.pallas_call(
        _fox_kernel,
        grid=(B, FOX_HEADS // n),
        in_specs=[
            pl.BlockSpec((dn, S), lambda b, h: (FR_FQ // dn + h, b)),
            pl.BlockSpec((S, dn), lambda b, h: (b, ZB_FK // n + h)),
            pl.BlockSpec((dn, S), lambda b, h: (FR_FV // dn + h, b)),
            pl.BlockSpec((S, dn), lambda b, h: (b, ZB_FZ // n + h)),
            pl.BlockSpec((None, LANES, S), lambda b, h: (b, 0, 0)),
        ],
        out_specs=pl.BlockSpec((S, dn), lambda b, h: (b, h)),
        out_shape=jax.ShapeDtypeStruct((B * S, FOX_WIDTH), BF16),
        scratch_shapes=[
            pltpu.VMEM((n, 2 * LANES, S), BF16),
            pltpu.VMEM((n, S, 2 * LANES), BF16),
            pltpu.VMEM((n, FOX_HEAD_DIM + BF16_ROWS, S), BF16),
        ],
        compiler_params=pltpu.CompilerParams(
            dimension_semantics=("arbitrary", "arbitrary"), vmem_limit_bytes=VMEM_LIMIT),
        name="fox",
    )(fm, z, fm, z, aqt)


def _mlstm_kernel(qt_ref, k_ref, vt_ref, mot_ref, mzt_ref, rows_ref, ng_ref, o_ref, vaug_ref):
    S = k_ref.shape[0]
    L = ML_L
    nc = S // L
    dk, dv = ML_QK_DIM, ML_V_DIM
    hs = range(ML_HEADS_PER_STEP)
    for j in hs:
        vaug_ref[j, :dv, :] = vt_ref[j * dv:(j + 1) * dv, :]
        vaug_ref[j, dv:, :] = jnp.ones((BF16_ROWS, S), BF16)
    s_idx = lax.broadcasted_iota(jnp.int32, (L, L), 0)
    t_idx = lax.broadcasted_iota(jnp.int32, (L, L), 1)
    causal = s_idx <= t_idx
    ng = [ng_ref[j * dv:(j + 1) * dv, :] for j in hs]
    head = [pl.program_id(1) * ML_HEADS_PER_STEP + j for j in hs]
    beta_cache = {}

    def beta_cols(c):
        if c not in beta_cache:
            slab = jnp.concatenate([rows_ref[c, RS_BETA:RS_BETA + 8, :], jnp.zeros((LANES - 8, L), F32)], axis=0)
            beta_cache[c] = jnp.transpose(slab)
        return beta_cache[c]

    head_shift = [(LANES - head[j]) % LANES for j in hs]

    def scores(j, c):
        kc = k_ref[c * L:(c + 1) * L, j * dk:(j + 1) * dk]
        return _dot(kc, qt_ref[j * dk:(j + 1) * dk, c * L:(c + 1) * L])

    def finish(j, c, st, state):
        ct_aug, g_prev = state
        cols = slice(c * L, (c + 1) * L)
        feats = slice(j * dv, (j + 1) * dv)
        b_col = pltpu.roll(beta_cols(c), head_shift[j], 1)[:, :1]
        g_row = rows_ref[c, pl.ds(RS_G + head[j], 1), :]
        g_last = g_row[:, L - 1:L]
        va = vaug_ref[j, :, cols]
        kw = k_ref[cols, j * dk:(j + 1) * dk].astype(F32) * jnp.exp2(b_col - g_last)
        upd = _dot(va, kw.astype(BF16))
        sw = st * jnp.where(causal, jnp.exp2(b_col - g_row), 0.0)
        num = _dot(va, sw.astype(BF16))
        if c > 0:
            w_inter = jnp.exp2(g_prev - g_row)
            num = num + w_inter * _dot(ct_aug.astype(BF16), qt_ref[j * dk:(j + 1) * dk, cols])
        den = num[dv:dv + 1, :]
        num = num[:dv, :]
        m_row = rows_ref[c, pl.ds(RS_M + head[j], 1), :]
        inv = 1.0 / jnp.maximum(jnp.abs(den), jnp.exp2(-m_row))
        ssq = jnp.sum(num * num, axis=0, keepdims=True)
        t_scale = inv * lax.rsqrt(inv * inv * ssq * (1.0 / dv) + EPS)
        mo = mot_ref[feats, cols].astype(F32)
        mz = mzt_ref[feats, cols].astype(F32)
        gate = mz * (1.0 / ((1.0 + jnp.exp2(mo * -LOG2E)) * (1.0 + jnp.exp2(mz * -LOG2E))))
        o_ref[cols, feats] = jnp.transpose(num * t_scale * (ng[j] * gate)).astype(BF16)
        if c == 0:
            return upd, g_last
        return jnp.exp2(g_prev - g_last) * ct_aug + upd, g_last

    work = [(j, c) for c in range(nc) for j in hs]
    pending = [scores(*work[i]) for i in range(ML_LOOKAHEAD)]
    state = [(None, None) for _ in hs]
    for i, (j, c) in enumerate(work):
        if i + ML_LOOKAHEAD < len(work):
            pending.append(scores(*work[i + ML_LOOKAHEAD]))
        state[j] = finish(j, c, pending.pop(0), state[j])


def _mlstm(fm, z, rowpack, ng_col, B, S):
    nc = S // ML_L
    n = ML_HEADS_PER_STEP
    dk, dv = n * ML_QK_DIM, n * ML_V_DIM
    assert FR_MQ % dk == 0 and (ZB_MK * LANES) % dk == 0
    assert FR_MV % dv == 0 and FR_MO % dv == 0 and FR_MZ % dv == 0
    return pl.pallas_call(
        _mlstm_kernel,
        grid=(B, ML_HEADS // n),
        in_specs=[
            pl.BlockSpec((dk, S), lambda b, h: (FR_MQ // dk + h, b)),
            pl.BlockSpec((S, dk), lambda b, h: (b, ZB_MK * LANES // dk + h)),
            pl.BlockSpec((dv, S), lambda b, h: (FR_MV // dv + h, b)),
            pl.BlockSpec((dv, S), lambda b, h: (FR_MO // dv + h, b)),
            pl.BlockSpec((dv, S), lambda b, h: (FR_MZ // dv + h, b)),
            pl.BlockSpec((None, nc, ROW_SLABS, ML_L), lambda b, h: (b, 0, 0, 0)),
            pl.BlockSpec((dv, 1), lambda b, h: (h, 0)),
        ],
        out_specs=pl.BlockSpec((S, dv), lambda b, h: (b, h)),
        out_shape=jax.ShapeDtypeStruct((B * S, ML_WIDTH), BF16),
        scratch_shapes=[pltpu.VMEM((n, ML_V_DIM + BF16_ROWS, S), BF16)],
        compiler_params=pltpu.CompilerParams(
            dimension_semantics=("arbitrary", "arbitrary"), vmem_limit_bytes=VMEM_LIMIT),
        name="mlstm",
    )(fm, z, fm, fm, fm, rowpack, ng_col)


def _out_kernel(oa_ref, hb_ref, ga_ref, gb_ref, x_ref, wfd_ref, wmd_ref, wo_ref, bga_ref, bgb_ref,
                fg_ref, o_ref):
    def branches(s):
        rows = slice(s * OUT_SUB, (s + 1) * OUT_SUB)
        return _dot(oa_ref[rows, :], wfd_ref[...]), _dot(hb_ref[rows, :], wmd_ref[...])

    def finish(s, ya, yb):
        rows = slice(s * OUT_SUB, (s + 1) * OUT_SUB)
        gate_a = jax.nn.sigmoid(ga_ref[rows, :].astype(F32) + bga_ref[...])
        gate_b = jax.nn.sigmoid(gb_ref[rows, :].astype(F32) + bgb_ref[...])
        y = gate_a * ya + gate_b * yb
        r = x_ref[rows, :] + _dot(y.astype(BF16), wo_ref[...])
        ms = jnp.mean(r * r, axis=-1, keepdims=True)
        o_ref[rows, :] = r * lax.rsqrt(ms + EPS) * fg_ref[...]

    n_sub = TM_OUT // OUT_SUB
    pending = branches(0)
    for s in range(n_sub):
        nxt = branches(s + 1) if s + 1 < n_sub else None
        finish(s, *pending)
        pending = nxt


def _out_stage(oa, hb, z, x2, wfd, wmd, wo, b_gate, final_g):
    T = x2.shape[0]
    row = lambda i: (i, 0)
    const = lambda i: (0, 0)
    return pl.pallas_call(
        _out_kernel,
        grid=(T // TM_OUT,),
        in_specs=[
            pl.BlockSpec((TM_OUT, FOX_WIDTH), row),
            pl.BlockSpec((TM_OUT, ML_WIDTH), row),
            pl.BlockSpec((TM_OUT, D_MODEL), lambda i: (i, ZB_GA // 8)),
            pl.BlockSpec((TM_OUT, D_MODEL), lambda i: (i, ZB_GB // 8)),
            pl.BlockSpec((TM_OUT, D_MODEL), row),
            pl.BlockSpec((FOX_WIDTH, D_MODEL), const, pipeline_mode=pl.Buffered(1)),
            pl.BlockSpec((ML_WIDTH, D_MODEL), const, pipeline_mode=pl.Buffered(1)),
            pl.BlockSpec((D_MODEL, D_MODEL), const, pipeline_mode=pl.Buffered(1)),
            pl.BlockSpec((1, D_MODEL), lambda i: (0, 0)),
            pl.BlockSpec((1, D_MODEL), lambda i: (0, 1)),
            pl.BlockSpec((1, D_MODEL), const),
        ],
        out_specs=pl.BlockSpec((TM_OUT, D_MODEL), row),
        out_shape=jax.ShapeDtypeStruct((T, D_MODEL), F32),
        compiler_params=pltpu.CompilerParams(
            dimension_semantics=("arbitrary",), vmem_limit_bytes=VMEM_LIMIT),
        name="outstage",
    )(oa, hb, z, z, x2, wfd, wmd, wo, b_gate, b_gate, final_g)


def _layer(x2, B, S, norm_g, w_in, b_fox_f, conv_w, conv_b, b_ml_i, b_ml_f, ml_norm_g, b_gate,
           w_fox_down, w_ml_down, w_out, out_g):
    assert w_in.shape == (D_MODEL, IN_WIDTH)
    w_fm, w_tm = _prep_weights(jnp.transpose(w_in))
    zrow = lambda n: jnp.zeros((n,), F32)
    bias_col = jnp.concatenate(
        [b_fox_f, b_ml_i, zrow(8 - ML_HEADS), b_ml_f, zrow(GATE_WIDTH - GL_LF - ML_HEADS)]).reshape(GATE_WIDTH, 1)
    cwq_t = jnp.concatenate(
        [jnp.transpose(conv_w[:, :ML_QK_WIDTH]), conv_b[:ML_QK_WIDTH, None],
         jnp.zeros((ML_QK_WIDTH, LANES - CONV_WIDTH - 1), F32)], axis=1)

    fm, zg, z = _inproj(x2, norm_g.reshape(1, D_MODEL), w_fm, w_tm, cwq_t, conv_w,
                        conv_b.reshape(1, 2 * ML_QK_WIDTH), S)
    rowpack, aqt = _gates(zg, bias_col, B, S)
    oa = _fox(fm, z, aqt, B, S)
    hb = _mlstm(fm, z, rowpack, ml_norm_g.reshape(ML_WIDTH, 1), B, S)
    return _out_stage(oa, hb, z, x2, w_fox_down.astype(BF16), w_ml_down.astype(BF16),
                      w_out.astype(BF16), b_gate.reshape(1, 2 * D_MODEL), out_g.reshape(1, D_MODEL))


def kernel(x, norm_g, w_in, b_fox_f, conv_w, conv_b, b_ml_i, b_ml_f, ml_norm_g, b_gate,
           w_fox_down, w_ml_down, w_out, final_g):
    B, S, _ = x.shape
    depth = norm_g.shape[0]
    assert depth == 1, "the final RMSNorm is fused into the (single) layer's output stage"
    out = _layer(x.reshape(B * S, D_MODEL), B, S, norm_g[0], w_in[0], b_fox_f[0], conv_w[0], conv_b[0],
                 b_ml_i[0], b_ml_f[0], ml_norm_g[0], b_gate[0], w_fox_down[0], w_ml_down[0],
                 w_out[0], final_g)
    return out.reshape(B, S, D_MODEL)
```

```python
import functools

import jax
import jax.numpy as jnp
from jax import lax
from jax.experimental import pallas as pl
from jax.experimental.pallas import tpu as pltpu

F32 = jnp.float32
BF16 = jnp.bfloat16

D_MODEL = 1024
FOX_HEADS = 8
FOX_HEAD_DIM = 128
FOX_WIDTH = FOX_HEADS * FOX_HEAD_DIM
ML_HEADS = 4
ML_QK_DIM = 128
ML_V_DIM = 256
ML_QK_WIDTH = ML_HEADS * ML_QK_DIM
ML_WIDTH = ML_HEADS * ML_V_DIM
CONV_WIDTH = 4
EPS = 1e-6

LANES = 128
SUBLANES = 8
BF16_ROWS = 16
GATE_WIDTH = LANES
VMEM_LIMIT = 56 * 1024 * 1024
LOG2E = 1.4426950408889634
FOX_Q_SCALE = FOX_HEAD_DIM ** -0.5 * LOG2E

FR_FQ = 0
FR_FV = FR_FQ + FOX_WIDTH
FR_MV = FR_FV + FOX_WIDTH
FR_MO = FR_MV + ML_WIDTH
FR_MZ = FR_MO + ML_WIDTH
FR_MQ = FR_MZ + ML_WIDTH
FR_GATE = FR_MQ + ML_QK_WIDTH
FM_ROWS = FR_GATE + GATE_WIDTH
FM_CHUNKS = ((FR_FQ, FR_FV), (FR_FV, FR_MV), (FR_MQ, FM_ROWS), (FR_MV, FR_MO), (FR_MO, FR_MZ), (FR_MZ, FR_MQ))
ZB_FK, ZB_FZ, ZB_GA, ZB_GB, ZB_MK = 0, 8, 16, 24, 32
Z_WIDTH = 4608
TM_CHUNKS = ((ZB_MK * LANES, Z_WIDTH), (0, ZB_GA * LANES), (ZB_GA * LANES, ZB_MK * LANES))
TM_IN = 512

GL_FOX, GL_IG, GL_LF = 0, 8, 16
RS_G, RS_M = 0, 8
ROW_SLABS = 16
AUG_LANES = 6 * FOX_HEADS

GATE_SEQS_PER_STEP = 4
TQ = 256
FOX_HEADS_PER_STEP = 4
FOX_LOOKAHEAD = 4
ML_L = 256
ML_HEADS_PER_STEP = 4
ML_LOOKAHEAD = 2
TM_OUT = 1024
OUT_SUB = 256

NT_DIMS = (((1,), (1,)), ((), ()))


def _dot(a, b):
    return jnp.dot(a, b, preferred_element_type=F32)


O_FQ, O_FK, O_FV = 0, FOX_WIDTH, 2 * FOX_WIDTH
O_FF = 3 * FOX_WIDTH
O_FZ = O_FF + FOX_HEADS
O_MQ = O_FZ + FOX_WIDTH
O_MK = O_MQ + ML_QK_WIDTH
O_MV = O_MK + ML_QK_WIDTH
O_MI = O_MV + ML_WIDTH
O_MF = O_MI + ML_HEADS
O_MO = O_MF + ML_HEADS
O_MZ = O_MO + ML_WIDTH
O_GA = O_MZ + ML_WIDTH
O_GB = O_GA + D_MODEL
IN_WIDTH = O_GB + D_MODEL

PREP_UNIT = 512
FM_GROUPS = ((O_FQ, FOX_WIDTH), (O_FV, FOX_WIDTH), (O_MV, ML_WIDTH), (O_MO, ML_WIDTH),
             (O_MZ, ML_WIDTH), (O_MQ, ML_QK_WIDTH))
TM_GROUPS = ((O_FK, FOX_WIDTH), (O_FZ, FOX_WIDTH), (O_GA, D_MODEL), (O_GB, D_MODEL),
             (O_MK, ML_QK_WIDTH))
KIND_FM, KIND_GATE, KIND_TM = 0, 1, 2


def _prep_units():
    units = [(s, KIND_FM) for off, width in FM_GROUPS for s in range(off, off + width, PREP_UNIT)]
    units.append((0, KIND_GATE))
    units += [(s, KIND_TM) for off, width in TM_GROUPS for s in range(off, off + width, PREP_UNIT)]
    assert all(s % SUBLANES == 0 for s, _ in units)
    return [s // SUBLANES for s, _ in units], [k for _, k in units]


PREP_STARTS, PREP_KINDS = _prep_units()
N_FM_UNITS = PREP_KINDS.index(KIND_GATE) + 1
FM_W_ROWS = N_FM_UNITS * PREP_UNIT


def _prep_kernel(start_ref, kind_ref, w_ref, ff_ref, mimf_ref, fm_ref, tm_ref):
    kind = kind_ref[pl.program_id(0)]

    @pl.when(kind == KIND_FM)
    def _():
        fm_ref[...] = w_ref[...].astype(BF16)

    @pl.when(kind == KIND_TM)
    def _():
        tm_ref[...] = jnp.transpose(w_ref[...]).astype(BF16)

    @pl.when(kind == KIND_GATE)
    def _():
        mimf = mimf_ref[...]
        pad = lambda n: jnp.zeros((n, D_MODEL), F32)
        rows = jnp.concatenate(
            [ff_ref[...], mimf[:ML_HEADS], pad(GL_LF - GL_IG - ML_HEADS), mimf[ML_HEADS:],
             pad(PREP_UNIT - GL_LF - ML_HEADS)], axis=0)
        fm_ref[...] = rows.astype(BF16)


def _prep_weights(w_t):
    assert O_MF == O_MI + ML_HEADS and FOX_HEADS == SUBLANES and 2 * ML_HEADS == SUBLANES
    gate_rows = lambda off: pl.BlockSpec((pl.Element(SUBLANES), pl.Element(D_MODEL)),
                                         lambda u, start, kind: (off, 0))
    grid_spec = pltpu.PrefetchScalarGridSpec(
        num_scalar_prefetch=2,
        grid=(len(PREP_KINDS),),
        in_specs=[
            pl.BlockSpec((pl.Element(PREP_UNIT), pl.Element(D_MODEL)),
                         lambda u, start, kind: (start[u] * SUBLANES, 0)),
            gate_rows(O_FF), gate_rows(O_MI),
        ],
        out_specs=[
            pl.BlockSpec((PREP_UNIT, D_MODEL), lambda u, start, kind: (jnp.minimum(u, N_FM_UNITS - 1), 0)),
            pl.BlockSpec((D_MODEL, PREP_UNIT), lambda u, start, kind: (0, jnp.maximum(u - N_FM_UNITS, 0))),
        ],
    )
    return pl.pallas_call(
        _prep_kernel,
        grid_spec=grid_spec,
        out_shape=[
            jax.ShapeDtypeStruct((FM_W_ROWS, D_MODEL), BF16),
            jax.ShapeDtypeStruct((D_MODEL, Z_WIDTH), BF16),
        ],
        compiler_params=pltpu.CompilerParams(
            dimension_semantics=("arbitrary",), vmem_limit_bytes=VMEM_LIMIT),
        name="prep",
    )(jnp.asarray(PREP_STARTS, jnp.int32), jnp.asarray(PREP_KINDS, jnp.int32), w_t, w_t, w_t)


def _silu(x):
    return x * (1.0 / (1.0 + jnp.exp2(x * -LOG2E)))


def _causal_conv(x, taps, bias, axis, halo):
    def run(xx):
        y = xx * taps[CONV_WIDTH - 1] + bias
        for r in range(1, CONV_WIDTH):
            y = y + pltpu.roll(xx, r, axis) * taps[CONV_WIDTH - 1 - r]
        return y

    edge = halo.shape[axis]
    if axis == 0:
        head = run(jnp.concatenate([halo, x[:edge, :]], axis=0))[edge:, :]
        return jnp.concatenate([head, run(x)[edge:, :]], axis=0)
    head = run(jnp.concatenate([halo, x[:, :edge]], axis=1))[:, edge:]
    return jnp.concatenate([head, run(x)[:, edge:]], axis=1)


def _inproj_kernel(x_ref, g_ref, wfm_ref, wtm_ref, cwq_ref, cwk_ref, cbk_ref,
                   fm_ref, zg_ref, z_ref, qhalo_ref, khalo_ref, *, steps_per_seq):
    @pl.when(pl.program_id(0) % steps_per_seq == 0)
    def _():
        qhalo_ref[...] = jnp.zeros_like(qhalo_ref)
        khalo_ref[...] = jnp.zeros_like(khalo_ref)

    x = x_ref[...]
    ms = jnp.mean(x * x, axis=-1, keepdims=True)
    xh = (x * lax.rsqrt(ms + EPS) * g_ref[...]).astype(BF16)

    def token_major(c0, c1):
        res = _dot(xh, wtm_ref[:, c0:c1])

        def epilogue():
            if c0 == ZB_MK * LANES:
                cwk = cwk_ref[...]
                taps = [cwk[j:j + 1, :] for j in range(CONV_WIDTH)]
                yk = _causal_conv(res, taps, cbk_ref[...], 0, khalo_ref[...])
                khalo_ref[...] = res[TM_IN - SUBLANES:, :]
                z_ref[:, c0:c1] = _silu(yk).astype(BF16)
            else:
                z_ref[:, c0:c1] = res.astype(BF16)
        return epilogue

    def feature_major(r0, r1):
        res = lax.dot_general(wfm_ref[r0:r1, :], xh, NT_DIMS, preferred_element_type=F32)

        def epilogue():
            if r0 == FR_MQ:
                xq = res[:ML_QK_WIDTH, :]
                cwq = cwq_ref[...]
                taps = [cwq[:, j:j + 1] for j in range(CONV_WIDTH)]
                yq = _causal_conv(xq, taps, cwq[:, CONV_WIDTH:CONV_WIDTH + 1], 1, qhalo_ref[...])
                fm_ref[FR_MQ:FR_GATE, :] = (_silu(yq) * (ML_QK_DIM ** -0.5)).astype(BF16)
                qhalo_ref[...] = xq[:, TM_IN - LANES:]
                zg_ref[...] = res[ML_QK_WIDTH:, :]
            elif r0 == FR_FQ:
                fm_ref[r0:r1, :] = (res * FOX_Q_SCALE).astype(BF16)
            else:
                fm_ref[r0:r1, :] = res.astype(BF16)
        return epilogue

    chunks = ([functools.partial(token_major, *TM_CHUNKS[0])]
              + [functools.partial(feature_major, r0, r1) for r0, r1 in FM_CHUNKS]
              + [functools.partial(token_major, c0, c1) for c0, c1 in TM_CHUNKS[1:]])
    pending = None
    for start in chunks:
        epilogue = start()
        if pending is not None:
            pending()
        pending = epilogue
    pending()


def _inproj(x2, norm_g, w_fm, w_tm, cwq_t, conv_w, conv_b, S):
    T = x2.shape[0]
    assert S % TM_IN == 0 and ZB_MK * LANES + ML_QK_WIDTH == Z_WIDTH
    resident = pl.Buffered(1)
    return pl.pallas_call(
        functools.partial(_inproj_kernel, steps_per_seq=S // TM_IN),
        grid=(T // TM_IN,),
        in_specs=[
            pl.BlockSpec((TM_IN, D_MODEL), lambda i: (i, 0)),
            pl.BlockSpec((1, D_MODEL), lambda i: (0, 0)),
            pl.BlockSpec((FM_W_ROWS, D_MODEL), lambda i: (0, 0), pipeline_mode=resident),
            pl.BlockSpec((D_MODEL, Z_WIDTH), lambda i: (0, 0), pipeline_mode=resident),
            pl.BlockSpec((ML_QK_WIDTH, LANES), lambda i: (0, 0)),
            pl.BlockSpec((CONV_WIDTH, ML_QK_WIDTH), lambda i: (0, 1)),
            pl.BlockSpec((1, ML_QK_WIDTH), lambda i: (0, 1)),
        ],
        out_specs=[
            pl.BlockSpec((FR_GATE, TM_IN), lambda i: (0, i)),
            pl.BlockSpec((GATE_WIDTH, TM_IN), lambda i: (0, i)),
            pl.BlockSpec((TM_IN, Z_WIDTH), lambda i: (i, 0)),
        ],
        out_shape=[
            jax.ShapeDtypeStruct((FR_GATE, T), BF16),
            jax.ShapeDtypeStruct((GATE_WIDTH, T), F32),
            jax.ShapeDtypeStruct((T, Z_WIDTH), BF16),
        ],
        scratch_shapes=[
            pltpu.VMEM((ML_QK_WIDTH, LANES), F32),
            pltpu.VMEM((SUBLANES, ML_QK_WIDTH), F32),
        ],
        compiler_params=pltpu.CompilerParams(
            dimension_semantics=("arbitrary",), vmem_limit_bytes=VMEM_LIMIT),
        name="inproj",
    )(x2, norm_g, w_fm, w_tm, cwq_t, conv_w, conv_b)


def _scan_lanes(x, op, ident):
    n = x.shape[1]
    lane = lax.broadcasted_iota(jnp.int32, x.shape, 1)
    k = 1
    while k < n:
        shifted = pltpu.roll(x, k, 1)
        x = op(x, jnp.where(lane >= k, shifted, ident))
        k *= 2
    return x


def _log_sigmoid(x):
    return jnp.minimum(x, 0.0) - jnp.log1p(jnp.exp(-jnp.abs(x)))


def _split3(x):
    hi = x.astype(BF16).astype(F32)
    r = x - hi
    mid = r.astype(BF16).astype(F32)
    lo = (r - mid).astype(BF16).astype(F32)
    return hi, mid, lo


def _gate_kernel(zg_ref, bias_ref, row_ref, col_ref, aq_ref):
    S = aq_ref.shape[2]
    for j in range(aq_ref.shape[0]):
        t = zg_ref[:, j * S:(j + 1) * S] + bias_ref[...]
        c = _scan_lanes(_log_sigmoid(t[GL_FOX:GL_FOX + 8]), jnp.add, 0.0)
        ig = t[GL_IG:GL_IG + 8]
        fsum = _scan_lanes(_log_sigmoid(t[GL_LF:GL_LF + 8]), jnp.add, 0.0)
        beta = (ig - fsum) * LOG2E
        gmax = jnp.maximum(_scan_lanes(beta, jnp.maximum, -jnp.inf), 0.0)
        m = fsum * LOG2E + gmax
        for ch in range(S // ML_L):
            t_ch = slice(ch * ML_L, (ch + 1) * ML_L)
            row_ref[j, ch, RS_G:RS_G + 8, :] = gmax[:, t_ch]
            row_ref[j, ch, RS_M:RS_M + 8, :] = m[:, t_ch]
        col_ref[j * S:(j + 1) * S, :] = jnp.transpose(
            jnp.concatenate([beta, jnp.zeros((LANES - 8, S), F32)], axis=0))
        hi, mid, lo = _split3(c * LOG2E)
        ones = jnp.ones((8, S), F32)
        pad = jnp.zeros((LANES - AUG_LANES, S), F32)
        aq_ref[j] = jnp.concatenate([hi, mid, lo, ones, ones, ones, pad], axis=0).astype(BF16)


def _gates(zg, bias_col, B, S):
    nb = GATE_SEQS_PER_STEP
    return pl.pallas_call(
        _gate_kernel,
        grid=(B // nb,),
        in_specs=[
            pl.BlockSpec((GATE_WIDTH, nb * S), lambda b: (0, b)),
            pl.BlockSpec((GATE_WIDTH, 1), lambda b: (0, 0)),
        ],
        out_specs=[
            pl.BlockSpec((nb, S // ML_L, ROW_SLABS, ML_L), lambda b: (b, 0, 0, 0)),
            pl.BlockSpec((nb * S, LANES), lambda b: (b, 0)),
            pl.BlockSpec((nb, LANES, S), lambda b: (b, 0, 0)),
        ],
        out_shape=[
            jax.ShapeDtypeStruct((B, S // ML_L, ROW_SLABS, ML_L), F32),
            jax.ShapeDtypeStruct((B * S, LANES), F32),
            jax.ShapeDtypeStruct((B, LANES, S), BF16),
        ],
        compiler_params=pltpu.CompilerParams(
            dimension_semantics=("arbitrary",), vmem_limit_bytes=VMEM_LIMIT),
        name="gates",
    )(zg, bias_col)


def _fox_kernel(qt_ref, k_ref, vt_ref, fz_ref, aqt_ref, o_ref, qaug_ref, kaug_ref, vaug_ref):
    S = k_ref.shape[0]
    d = FOX_HEAD_DIM
    hs = range(FOX_HEADS_PER_STEP)
    lane = lax.broadcasted_iota(jnp.int32, (1, LANES), 1)
    aq = aqt_ref[...]
    half = AUG_LANES // 2
    ak = jnp.transpose(jnp.concatenate([aq[half:AUG_LANES], -aq[:half], aq[AUG_LANES:]], axis=0))
    for j in hs:
        h = pl.program_id(1) * FOX_HEADS_PER_STEP + j
        keep = jnp.where(((lane & (FOX_HEADS - 1)) == h) & (lane < AUG_LANES), 1.0, 0.0).astype(BF16)
        qaug_ref[j, :d, :] = qt_ref[j * d:(j + 1) * d, :]
        qaug_ref[j, d:, :] = aqt_ref[...]
        kaug_ref[j, :, :LANES] = k_ref[:, j * LANES:(j + 1) * LANES]
        kaug_ref[j, :, LANES:] = ak * keep
        vaug_ref[j, :d, :] = vt_ref[j * d:(j + 1) * d, :]
        vaug_ref[j, d:, :] = jnp.ones((BF16_ROWS, S), BF16)
    s_idx = lax.broadcasted_iota(jnp.int32, (TQ, TQ), 0)
    t_idx = lax.broadcasted_iota(jnp.int32, (TQ, TQ), 1)
    causal = s_idx <= t_idx

    ng = S // TQ
    m = [[None] * ng for _ in hs]
    acc = [[None] * ng for _ in hs]
    tiles = [(j, kj, g) for kj in range(ng) for g in range(kj, ng) for j in hs]

    def scores(j, kj, g):
        ka = kaug_ref[j, kj * TQ:(kj + 1) * TQ, :]
        qa = qaug_ref[j, :, g * TQ:(g + 1) * TQ]
        return _dot(ka, qa)

    def finish(j, kj, g, st):
        va = vaug_ref[j, :, kj * TQ:(kj + 1) * TQ]
        if kj == g:
            st = jnp.where(causal, st, -jnp.inf)
        cm = jnp.max(st, axis=0, keepdims=True)
        if kj == 0:
            m[j][g] = cm
            acc[j][g] = _dot(va, jnp.exp2(st - cm).astype(BF16))
        else:
            m_new = jnp.maximum(m[j][g], cm)
            alpha = jnp.exp2(m[j][g] - m_new)
            acc[j][g] = alpha * acc[j][g] + _dot(va, jnp.exp2(st - m_new).astype(BF16))
            m[j][g] = m_new
        if kj == g:
            a = acc[j][g]
            o = jnp.transpose(a[:d] * (1.0 / a[d:d + 1]))
            fz = fz_ref[g * TQ:(g + 1) * TQ, j * LANES:(j + 1) * LANES].astype(F32)
            o_ref[g * TQ:(g + 1) * TQ, j * LANES:(j + 1) * LANES] = (o * _silu(fz)).astype(BF16)

    pending = [scores(*tiles[i]) for i in range(FOX_LOOKAHEAD)]
    for i, tile in enumerate(tiles):
        if i + FOX_LOOKAHEAD < len(tiles):
            pending.append(scores(*tiles[i + FOX_LOOKAHEAD]))
        finish(*tile, pending.pop(0))


def _fox(fm, z, aqt, B, S):
    n = FOX_HEADS_PER_STEP
    dn = n * FOX_HEAD_DIM
    assert FR_FQ % dn == 0 and FR_FV % dn == 0 and ZB_FK % n == 0 and ZB_FZ % n == 0
    return pl.pallas_call(
        _fox_kernel,
        grid=(B, FOX_HEADS // n),
        in_specs=[
            pl.BlockSpec((dn, S), lambda b, h: (FR_FQ // dn + h, b)),
            pl.BlockSpec((S, dn), lambda b, h: (b, ZB_FK // n + h)),
            pl.BlockSpec((dn, S), lambda b, h: (FR_FV // dn + h, b)),
            pl.BlockSpec((S, dn), lambda b, h: (b, ZB_FZ // n + h)),
            pl.BlockSpec((None, LANES, S), lambda b, h: (b, 0, 0)),
        ],
        out_specs=pl.BlockSpec((S, dn), lambda b, h: (b, h)),
        out_shape=jax.ShapeDtypeStruct((B * S, FOX_WIDTH), BF16),
        scratch_shapes=[
            pltpu.VMEM((n, 2 * LANES, S), BF16),
            pltpu.VMEM((n, S, 2 * LANES), BF16),
            pltpu.VMEM((n, FOX_HEAD_DIM + BF16_ROWS, S), BF16),
        ],
        compiler_params=pltpu.CompilerParams(
            dimension_semantics=("arbitrary", "arbitrary"), vmem_limit_bytes=VMEM_LIMIT),
        name="fox",
    )(fm, z, fm, z, aqt)


def _mlstm_kernel(qt_ref, k_ref, vt_ref, mot_ref, mzt_ref, col_ref, rows_ref, ng_ref, o_ref, vaug_ref):
    S = k_ref.shape[0]
    L = ML_L
    nc = S // L
    dk, dv = ML_QK_DIM, ML_V_DIM
    hs = range(ML_HEADS_PER_STEP)
    for j in hs:
        vaug_ref[j, :dv, :] = vt_ref[j * dv:(j + 1) * dv, :]
        vaug_ref[j, dv:, :] = jnp.ones((BF16_ROWS, S), BF16)
    s_idx = lax.broadcasted_iota(jnp.int32, (L, L), 0)
    t_idx = lax.broadcasted_iota(jnp.int32, (L, L), 1)
    causal = s_idx <= t_idx
    ng = [ng_ref[j * dv:(j + 1) * dv, :] for j in hs]
    head = [pl.program_id(1) * ML_HEADS_PER_STEP + j for j in hs]

    head_shift = [(LANES - head[j]) % LANES for j in hs]

    def scores(j, c):
        kc = k_ref[c * L:(c + 1) * L, j * dk:(j + 1) * dk]
        return _dot(kc, qt_ref[j * dk:(j + 1) * dk, c * L:(c + 1) * L])

    def finish(j, c, st, state):
        ct_aug, g_prev = state
        cols = slice(c * L, (c + 1) * L)
        feats = slice(j * dv, (j + 1) * dv)
        b_col = pltpu.roll(col_ref[cols, :], head_shift[j], 1)[:, :1]
        g_row = rows_ref[c, pl.ds(RS_G + head[j], 1), :]
        g_last = g_row[:, L - 1:L]
        va = vaug_ref[j, :, cols]
        kw = k_ref[cols, j * dk:(j + 1) * dk].astype(F32) * jnp.exp2(b_col - g_last)
        upd = _dot(va, kw.astype(BF16))
        sw = st * jnp.where(causal, jnp.exp2(b_col - g_row), 0.0)
        num = _dot(va, sw.astype(BF16))
        if c > 0:
            w_inter = jnp.exp2(g_prev - g_row)
            num = num + w_inter * _dot(ct_aug.astype(BF16), qt_ref[j * dk:(j + 1) * dk, cols])
        den = num[dv:dv + 1, :]
        num = num[:dv, :]
        m_row = rows_ref[c, pl.ds(RS_M + head[j], 1), :]
        inv = 1.0 / jnp.maximum(jnp.abs(den), jnp.exp2(-m_row))
        ssq = jnp.sum(num * num, axis=0, keepdims=True)
        t_scale = inv * lax.rsqrt(inv * inv * ssq * (1.0 / dv) + EPS)
        mo = mot_ref[feats, cols].astype(F32)
        mz = mzt_ref[feats, cols].astype(F32)
        gate = mz * (1.0 / ((1.0 + jnp.exp2(mo * -LOG2E)) * (1.0 + jnp.exp2(mz * -LOG2E))))
        o_ref[cols, feats] = jnp.transpose(num * t_scale * (ng[j] * gate)).astype(BF16)
        if c == 0:
            return upd, g_last
        return jnp.exp2(g_prev - g_last) * ct_aug + upd, g_last

    work = [(j, c) for c in range(nc) for j in hs]
    pending = [scores(*work[i]) for i in range(ML_LOOKAHEAD)]
    state = [(None, None) for _ in hs]
    for i, (j, c) in enumerate(work):
        if i + ML_LOOKAHEAD < len(work):
            pending.append(scores(*work[i + ML_LOOKAHEAD]))
        state[j] = finish(j, c, pending.pop(0), state[j])


def _mlstm(fm, z, colpack, rowpack, ng_col, B, S):
    nc = S // ML_L
    n = ML_HEADS_PER_STEP
    dk, dv = n * ML_QK_DIM, n * ML_V_DIM
    assert FR_MQ % dk == 0 and (ZB_MK * LANES) % dk == 0
    assert FR_MV % dv == 0 and FR_MO % dv == 0 and FR_MZ % dv == 0
    return pl.pallas_call(
        _mlstm_kernel,
        grid=(B, ML_HEADS // n),
        in_specs=[
            pl.BlockSpec((dk, S), lambda b, h: (FR_MQ // dk + h, b)),
            pl.BlockSpec((S, dk), lambda b, h: (b, ZB_MK * LANES // dk + h)),
            pl.BlockSpec((dv, S), lambda b, h: (FR_MV // dv + h, b)),
            pl.BlockSpec((dv, S), lambda b, h: (FR_MO // dv + h, b)),
            pl.BlockSpec((dv, S), lambda b, h: (FR_MZ // dv + h, b)),
            pl.BlockSpec((S, LANES), lambda b, h: (b, 0)),
            pl.BlockSpec((None, nc, ROW_SLABS, ML_L), lambda b, h: (b, 0, 0, 0)),
            pl.BlockSpec((dv, 1), lambda b, h: (h, 0)),
        ],
        out_specs=pl.BlockSpec((S, dv), lambda b, h: (b, h)),
        out_shape=jax.ShapeDtypeStruct((B * S, ML_WIDTH), BF16),
        scratch_shapes=[pltpu.VMEM((n, ML_V_DIM + BF16_ROWS, S), BF16)],
        compiler_params=pltpu.CompilerParams(
            dimension_semantics=("arbitrary", "arbitrary"), vmem_limit_bytes=VMEM_LIMIT),
        name="mlstm",
    )(fm, z, fm, fm, fm, colpack, rowpack, ng_col)


def _out_kernel(oa_ref, hb_ref, ga_ref, gb_ref, x_ref, wfd_ref, wmd_ref, wo_ref, bga_ref, bgb_ref,
                fg_ref, o_ref):
    def branches(s):
        rows = slice(s * OUT_SUB, (s + 1) * OUT_SUB)
        return _dot(oa_ref[rows, :], wfd_ref[...]), _dot(hb_ref[rows, :], wmd_ref[...])

    def finish(s, ya, yb):
        rows = slice(s * OUT_SUB, (s + 1) * OUT_SUB)
        gate_a = jax.nn.sigmoid(ga_ref[rows, :].astype(F32) + bga_ref[...])
        gate_b = jax.nn.sigmoid(gb_ref[rows, :].astype(F32) + bgb_ref[...])
        y = gate_a * ya + gate_b * yb
        r = x_ref[rows, :] + _dot(y.astype(BF16), wo_ref[...])
        ms = jnp.mean(r * r, axis=-1, keepdims=True)
        o_ref[rows, :] = r * lax.rsqrt(ms + EPS) * fg_ref[...]

    n_sub = TM_OUT // OUT_SUB
    pending = branches(0)
    for s in range(n_sub):
        nxt = branches(s + 1) if s + 1 < n_sub else None
        finish(s, *pending)
        pending = nxt


def _out_stage(oa, hb, z, x2, wfd, wmd, wo, b_gate, final_g):
    T = x2.shape[0]
    row = lambda i: (i, 0)
    const = lambda i: (0, 0)
    return pl.pallas_call(
        _out_kernel,
        grid=(T // TM_OUT,),
        in_specs=[
            pl.BlockSpec((TM_OUT, FOX_WIDTH), row),
            pl.BlockSpec((TM_OUT, ML_WIDTH), row),
            pl.BlockSpec((TM_OUT, D_MODEL), lambda i: (i, ZB_GA // 8)),
            pl.BlockSpec((TM_OUT, D_MODEL), lambda i: (i, ZB_GB // 8)),
            pl.BlockSpec((TM_OUT, D_MODEL), row),
            pl.BlockSpec((FOX_WIDTH, D_MODEL), const, pipeline_mode=pl.Buffered(1)),
            pl.BlockSpec((ML_WIDTH, D_MODEL), const, pipeline_mode=pl.Buffered(1)),
            pl.BlockSpec((D_MODEL, D_MODEL), const, pipeline_mode=pl.Buffered(1)),
            pl.BlockSpec((1, D_MODEL), lambda i: (0, 0)),
            pl.BlockSpec((1, D_MODEL), lambda i: (0, 1)),
            pl.BlockSpec((1, D_MODEL), const),
        ],
        out_specs=pl.BlockSpec((TM_OUT, D_MODEL), row),
        out_shape=jax.ShapeDtypeStruct((T, D_MODEL), F32),
        compiler_params=pltpu.CompilerParams(
            dimension_semantics=("arbitrary",), vmem_limit_bytes=VMEM_LIMIT),
        name="outstage",
    )(oa, hb, z, z, x2, wfd, wmd, wo, b_gate, b_gate, final_g)


def _layer(x2, B, S, norm_g, w_in, b_fox_f, conv_w, conv_b, b_ml_i, b_ml_f, ml_norm_g, b_gate,
           w_fox_down, w_ml_down, w_out, out_g):
    assert w_in.shape == (D_MODEL, IN_WIDTH)
    w_fm, w_tm = _prep_weights(jnp.transpose(w_in))
    zrow = lambda n: jnp.zeros((n,), F32)
    bias_col = jnp.concatenate(
        [b_fox_f, b_ml_i, zrow(8 - ML_HEADS), b_ml_f, zrow(GATE_WIDTH - GL_LF - ML_HEADS)]).reshape(GATE_WIDTH, 1)
    cwq_t = jnp.concatenate(
        [jnp.transpose(conv_w[:, :ML_QK_WIDTH]), conv_b[:ML_QK_WIDTH, None],
         jnp.zeros((ML_QK_WIDTH, LANES - CONV_WIDTH - 1), F32)], axis=1)

    fm, zg, z = _inproj(x2, norm_g.reshape(1, D_MODEL), w_fm, w_tm, cwq_t, conv_w,
                        conv_b.reshape(1, 2 * ML_QK_WIDTH), S)
    rowpack, colpack, aqt = _gates(zg, bias_col, B, S)
    oa = _fox(fm, z, aqt, B, S)
    hb = _mlstm(fm, z, colpack, rowpack, ml_norm_g.reshape(ML_WIDTH, 1), B, S)
    return _out_stage(oa, hb, z, x2, w_fox_down.astype(BF16), w_ml_down.astype(BF16),
                      w_out.astype(BF16), b_gate.reshape(1, 2 * D_MODEL), out_g.reshape(1, D_MODEL))


def kernel(x, norm_g, w_in, b_fox_f, conv_w, conv_b, b_ml_i, b_ml_f, ml_norm_g, b_gate,
           w_fox_down, w_ml_down, w_out, final_g):
    B, S, _ = x.shape
    depth = norm_g.shape[0]
    assert depth == 1, "the final RMSNorm is fused into the (single) layer's output stage"
    out = _layer(x.reshape(B * S, D_MODEL), B, S, norm_g[0], w_in[0], b_fox_f[0], conv_w[0], conv_b[0],
                 b_ml_i[0], b_ml_f[0], ml_norm_g[0], b_gate[0], w_fox_down[0], w_ml_down[0],
                 w_out[0], final_g)
    return out.reshape(B, S, D_MODEL)
```

```python
import functools

import jax
import jax.numpy as jnp
from jax import lax
from jax.experimental import pallas as pl
from jax.experimental.pallas import tpu as pltpu

F32 = jnp.float32
BF16 = jnp.bfloat16

D_MODEL = 1024
FOX_HEADS = 8
FOX_HEAD_DIM = 128
FOX_WIDTH = FOX_HEADS * FOX_HEAD_DIM
ML_HEADS = 4
ML_QK_DIM = 128
ML_V_DIM = 256
ML_QK_WIDTH = ML_HEADS * ML_QK_DIM
ML_WIDTH = ML_HEADS * ML_V_DIM
CONV_WIDTH = 4
EPS = 1e-6

LANES = 128
SUBLANES = 8
BF16_ROWS = 16
GATE_WIDTH = LANES
VMEM_LIMIT = 56 * 1024 * 1024
LOG2E = 1.4426950408889634
FOX_Q_SCALE = FOX_HEAD_DIM ** -0.5 * LOG2E

FR_FQ = 0
FR_FV = FR_FQ + FOX_WIDTH
FR_MV = FR_FV + FOX_WIDTH
FR_MO = FR_MV + ML_WIDTH
FR_MZ = FR_MO + ML_WIDTH
FR_MQ = FR_MZ + ML_WIDTH
FR_GATE = FR_MQ + ML_QK_WIDTH
FM_ROWS = FR_GATE + GATE_WIDTH
FM_CHUNKS = ((FR_FQ, FR_FV), (FR_FV, FR_MV), (FR_MQ, FM_ROWS), (FR_MV, FR_MO), (FR_MO, FR_MQ))
OR_FQ, OR_FV, OR_MV = FR_FQ, FR_FV, FR_MV
OR_MG = OR_MV + ML_WIDTH
OR_MQ = OR_MG + ML_WIDTH
OR_ROWS = OR_MQ + ML_QK_WIDTH
ZB_FK, ZB_FZ, ZB_GA, ZB_GB, ZB_MK = 0, 8, 16, 24, 32
Z_WIDTH = 4608
TM_CHUNKS = ((ZB_MK * LANES, Z_WIDTH), (0, ZB_GA * LANES), (ZB_GA * LANES, ZB_MK * LANES))
TM_IN = 512

GL_FOX, GL_IG, GL_LF = 0, 8, 16
RS_G, RS_M = 0, 8
ROW_SLABS = 16
AUG_LANES = 6 * FOX_HEADS

GATE_SEQS_PER_STEP = 4
TQ = 256
FOX_HEADS_PER_STEP = 4
FOX_LOOKAHEAD = 4
ML_L = 256
ML_HEADS_PER_STEP = 4
ML_LOOKAHEAD = 2
TM_OUT = 1024
OUT_SUB = 256

NT_DIMS = (((1,), (1,)), ((), ()))


def _dot(a, b):
    return jnp.dot(a, b, preferred_element_type=F32)


O_FQ, O_FK, O_FV = 0, FOX_WIDTH, 2 * FOX_WIDTH
O_FF = 3 * FOX_WIDTH
O_FZ = O_FF + FOX_HEADS
O_MQ = O_FZ + FOX_WIDTH
O_MK = O_MQ + ML_QK_WIDTH
O_MV = O_MK + ML_QK_WIDTH
O_MI = O_MV + ML_WIDTH
O_MF = O_MI + ML_HEADS
O_MO = O_MF + ML_HEADS
O_MZ = O_MO + ML_WIDTH
O_GA = O_MZ + ML_WIDTH
O_GB = O_GA + D_MODEL
IN_WIDTH = O_GB + D_MODEL

PREP_UNIT = 512
FM_GROUPS = ((O_FQ, FOX_WIDTH), (O_FV, FOX_WIDTH), (O_MV, ML_WIDTH), (O_MO, ML_WIDTH),
             (O_MZ, ML_WIDTH), (O_MQ, ML_QK_WIDTH))
TM_GROUPS = ((O_FK, FOX_WIDTH), (O_FZ, FOX_WIDTH), (O_GA, D_MODEL), (O_GB, D_MODEL),
             (O_MK, ML_QK_WIDTH))
KIND_FM, KIND_GATE, KIND_TM = 0, 1, 2


def _prep_units():
    units = [(s, KIND_FM) for off, width in FM_GROUPS for s in range(off, off + width, PREP_UNIT)]
    units.append((0, KIND_GATE))
    units += [(s, KIND_TM) for off, width in TM_GROUPS for s in range(off, off + width, PREP_UNIT)]
    assert all(s % SUBLANES == 0 for s, _ in units)
    return [s // SUBLANES for s, _ in units], [k for _, k in units]


PREP_STARTS, PREP_KINDS = _prep_units()
N_FM_UNITS = PREP_KINDS.index(KIND_GATE) + 1
FM_W_ROWS = N_FM_UNITS * PREP_UNIT


def _prep_kernel(start_ref, kind_ref, w_ref, ff_ref, mimf_ref, fm_ref, tm_ref):
    kind = kind_ref[pl.program_id(0)]

    @pl.when(kind == KIND_FM)
    def _():
        fm_ref[...] = w_ref[...].astype(BF16)

    @pl.when(kind == KIND_TM)
    def _():
        tm_ref[...] = jnp.transpose(w_ref[...]).astype(BF16)

    @pl.when(kind == KIND_GATE)
    def _():
        mimf = mimf_ref[...]
        pad = lambda n: jnp.zeros((n, D_MODEL), F32)
        rows = jnp.concatenate(
            [ff_ref[...], mimf[:ML_HEADS], pad(GL_LF - GL_IG - ML_HEADS), mimf[ML_HEADS:],
             pad(PREP_UNIT - GL_LF - ML_HEADS)], axis=0)
        fm_ref[...] = rows.astype(BF16)


def _prep_weights(w_t):
    assert O_MF == O_MI + ML_HEADS and FOX_HEADS == SUBLANES and 2 * ML_HEADS == SUBLANES
    gate_rows = lambda off: pl.BlockSpec((pl.Element(SUBLANES), pl.Element(D_MODEL)),
                                         lambda u, start, kind: (off, 0))
    grid_spec = pltpu.PrefetchScalarGridSpec(
        num_scalar_prefetch=2,
        grid=(len(PREP_KINDS),),
        in_specs=[
            pl.BlockSpec((pl.Element(PREP_UNIT), pl.Element(D_MODEL)),
                         lambda u, start, kind: (start[u] * SUBLANES, 0)),
            gate_rows(O_FF), gate_rows(O_MI),
        ],
        out_specs=[
            pl.BlockSpec((PREP_UNIT, D_MODEL), lambda u, start, kind: (jnp.minimum(u, N_FM_UNITS - 1), 0)),
            pl.BlockSpec((D_MODEL, PREP_UNIT), lambda u, start, kind: (0, jnp.maximum(u - N_FM_UNITS, 0))),
        ],
    )
    return pl.pallas_call(
        _prep_kernel,
        grid_spec=grid_spec,
        out_shape=[
            jax.ShapeDtypeStruct((FM_W_ROWS, D_MODEL), BF16),
            jax.ShapeDtypeStruct((D_MODEL, Z_WIDTH), BF16),
        ],
        compiler_params=pltpu.CompilerParams(
            dimension_semantics=("arbitrary",), vmem_limit_bytes=VMEM_LIMIT),
        name="prep",
    )(jnp.asarray(PREP_STARTS, jnp.int32), jnp.asarray(PREP_KINDS, jnp.int32), w_t, w_t, w_t)


def _silu(x):
    return x * (1.0 / (1.0 + jnp.exp2(x * -LOG2E)))


def _causal_conv(x, taps, bias, axis, halo):
    def run(xx):
        y = xx * taps[CONV_WIDTH - 1] + bias
        for r in range(1, CONV_WIDTH):
            y = y + pltpu.roll(xx, r, axis) * taps[CONV_WIDTH - 1 - r]
        return y

    edge = halo.shape[axis]
    if axis == 0:
        head = run(jnp.concatenate([halo, x[:edge, :]], axis=0))[edge:, :]
        return jnp.concatenate([head, run(x)[edge:, :]], axis=0)
    head = run(jnp.concatenate([halo, x[:, :edge]], axis=1))[:, edge:]
    return jnp.concatenate([head, run(x)[:, edge:]], axis=1)


def _inproj_kernel(x_ref, g_ref, wfm_ref, wtm_ref, cwq_ref, cwk_ref, cbk_ref, mlg_ref,
                   fm_ref, zg_ref, z_ref, qhalo_ref, khalo_ref, *, steps_per_seq):
    @pl.when(pl.program_id(0) % steps_per_seq == 0)
    def _():
        qhalo_ref[...] = jnp.zeros_like(qhalo_ref)
        khalo_ref[...] = jnp.zeros_like(khalo_ref)

    x = x_ref[...]
    ms = jnp.mean(x * x, axis=-1, keepdims=True)
    xh = (x * lax.rsqrt(ms + EPS) * g_ref[...]).astype(BF16)

    def token_major(c0, c1):
        res = _dot(xh, wtm_ref[:, c0:c1])

        def epilogue():
            if c0 == ZB_MK * LANES:
                cwk = cwk_ref[...]
                taps = [cwk[j:j + 1, :] for j in range(CONV_WIDTH)]
                yk = _causal_conv(res, taps, cbk_ref[...], 0, khalo_ref[...])
                khalo_ref[...] = res[TM_IN - SUBLANES:, :]
                z_ref[:, c0:c1] = _silu(yk).astype(BF16)
            else:
                z_ref[:, c0:c1] = res.astype(BF16)
        return epilogue

    def feature_major(r0, r1):
        res = lax.dot_general(wfm_ref[r0:r1, :], xh, NT_DIMS, preferred_element_type=F32)

        def epilogue():
            if r0 == FR_MQ:
                xq = res[:ML_QK_WIDTH, :]
                cwq = cwq_ref[...]
                taps = [cwq[:, j:j + 1] for j in range(CONV_WIDTH)]
                yq = _causal_conv(xq, taps, cwq[:, CONV_WIDTH:CONV_WIDTH + 1], 1, qhalo_ref[...])
                fm_ref[OR_MQ:OR_ROWS, :] = (_silu(yq) * (ML_QK_DIM ** -0.5)).astype(BF16)
                qhalo_ref[...] = xq[:, TM_IN - LANES:]
                zg_ref[...] = res[ML_QK_WIDTH:, :]
            elif r0 == FR_FQ:
                fm_ref[OR_FQ:OR_FV, :] = (res * FOX_Q_SCALE).astype(BF16)
            elif r0 == FR_MO:
                mo = res[:ML_WIDTH, :]
                mz = res[ML_WIDTH:, :]
                gate = mz * (1.0 / ((1.0 + jnp.exp2(mo * -LOG2E)) * (1.0 + jnp.exp2(mz * -LOG2E))))
                fm_ref[OR_MG:OR_MQ, :] = (mlg_ref[...] * gate).astype(BF16)
            else:
                fm_ref[r0:r1, :] = res.astype(BF16)
        return epilogue

    chunks = ([functools.partial(token_major, *TM_CHUNKS[0])]
              + [functools.partial(feature_major, r0, r1) for r0, r1 in FM_CHUNKS]
              + [functools.partial(token_major, c0, c1) for c0, c1 in TM_CHUNKS[1:]])
    pending = None
    for start in chunks:
        epilogue = start()
        if pending is not None:
            pending()
        pending = epilogue
    pending()


def _inproj(x2, norm_g, w_fm, w_tm, cwq_t, conv_w, conv_b, ml_g_col, S):
    T = x2.shape[0]
    assert S % TM_IN == 0 and ZB_MK * LANES + ML_QK_WIDTH == Z_WIDTH
    resident = pl.Buffered(1)
    return pl.pallas_call(
        functools.partial(_inproj_kernel, steps_per_seq=S // TM_IN),
        grid=(T // TM_IN,),
        in_specs=[
            pl.BlockSpec((TM_IN, D_MODEL), lambda i: (i, 0)),
            pl.BlockSpec((1, D_MODEL), lambda i: (0, 0)),
            pl.BlockSpec((FM_W_ROWS, D_MODEL), lambda i: (0, 0), pipeline_mode=resident),
            pl.BlockSpec((D_MODEL, Z_WIDTH), lambda i: (0, 0), pipeline_mode=resident),
            pl.BlockSpec((ML_QK_WIDTH, LANES), lambda i: (0, 0)),
            pl.BlockSpec((CONV_WIDTH, ML_QK_WIDTH), lambda i: (0, 1)),
            pl.BlockSpec((1, ML_QK_WIDTH), lambda i: (0, 1)),
            pl.BlockSpec((ML_WIDTH, 1), lambda i: (0, 0)),
        ],
        out_specs=[
            pl.BlockSpec((OR_ROWS, TM_IN), lambda i: (0, i)),
            pl.BlockSpec((GATE_WIDTH, TM_IN), lambda i: (0, i)),
            pl.BlockSpec((TM_IN, Z_WIDTH), lambda i: (i, 0)),
        ],
        out_shape=[
            jax.ShapeDtypeStruct((OR_ROWS, T), BF16),
            jax.ShapeDtypeStruct((GATE_WIDTH, T), F32),
            jax.ShapeDtypeStruct((T, Z_WIDTH), BF16),
        ],
        scratch_shapes=[
            pltpu.VMEM((ML_QK_WIDTH, LANES), F32),
            pltpu.VMEM((SUBLANES, ML_QK_WIDTH), F32),
        ],
        compiler_params=pltpu.CompilerParams(
            dimension_semantics=("arbitrary",), vmem_limit_bytes=VMEM_LIMIT),
        name="inproj",
    )(x2, norm_g, w_fm, w_tm, cwq_t, conv_w, conv_b, ml_g_col)


def _scan_lanes(x, op, ident):
    n = x.shape[1]
    lane = lax.broadcasted_iota(jnp.int32, x.shape, 1)
    k = 1
    while k < n:
        shifted = pltpu.roll(x, k, 1)
        x = op(x, jnp.where(lane >= k, shifted, ident))
        k *= 2
    return x


def _log_sigmoid(x):
    return jnp.minimum(x, 0.0) - jnp.log1p(jnp.exp(-jnp.abs(x)))


def _split3(x):
    hi = x.astype(BF16).astype(F32)
    r = x - hi
    mid = r.astype(BF16).astype(F32)
    lo = (r - mid).astype(BF16).astype(F32)
    return hi, mid, lo


def _gate_kernel(zg_ref, bias_ref, row_ref, col_ref, aq_ref):
    S = aq_ref.shape[2]
    for j in range(aq_ref.shape[0]):
        t = zg_ref[:, j * S:(j + 1) * S] + bias_ref[...]
        c = _scan_lanes(_log_sigmoid(t[GL_FOX:GL_FOX + 8]), jnp.add, 0.0)
        ig = t[GL_IG:GL_IG + 8]
        fsum = _scan_lanes(_log_sigmoid(t[GL_LF:GL_LF + 8]), jnp.add, 0.0)
        beta = (ig - fsum) * LOG2E
        gmax = jnp.maximum(_scan_lanes(beta, jnp.maximum, -jnp.inf), 0.0)
        m = fsum * LOG2E + gmax
        for ch in range(S // ML_L):
            t_ch = slice(ch * ML_L, (ch + 1) * ML_L)
            row_ref[j, ch, RS_G:RS_G + 8, :] = gmax[:, t_ch]
            row_ref[j, ch, RS_M:RS_M + 8, :] = m[:, t_ch]
        col_ref[j * S:(j + 1) * S, :] = jnp.transpose(
            jnp.concatenate([beta, jnp.zeros((LANES - 8, S), F32)], axis=0))
        hi, mid, lo = _split3(c * LOG2E)
        ones = jnp.ones((8, S), F32)
        pad = jnp.zeros((LANES - AUG_LANES, S), F32)
        aq_ref[j] = jnp.concatenate([hi, mid, lo, ones, ones, ones, pad], axis=0).astype(BF16)


def _gates(zg, bias_col, B, S):
    nb = GATE_SEQS_PER_STEP
    return pl.pallas_call(
        _gate_kernel,
        grid=(B // nb,),
        in_specs=[
            pl.BlockSpec((GATE_WIDTH, nb * S), lambda b: (0, b)),
            pl.BlockSpec((GATE_WIDTH, 1), lambda b: (0, 0)),
        ],
        out_specs=[
            pl.BlockSpec((nb, S // ML_L, ROW_SLABS, ML_L), lambda b: (b, 0, 0, 0)),
            pl.BlockSpec((nb * S, LANES), lambda b: (b, 0)),
            pl.BlockSpec((nb, LANES, S), lambda b: (b, 0, 0)),
        ],
        out_shape=[
            jax.ShapeDtypeStruct((B, S // ML_L, ROW_SLABS, ML_L), F32),
            jax.ShapeDtypeStruct((B * S, LANES), F32),
            jax.ShapeDtypeStruct((B, LANES, S), BF16),
        ],
        compiler_params=pltpu.CompilerParams(
            dimension_semantics=("arbitrary",), vmem_limit_bytes=VMEM_LIMIT),
        name="gates",
    )(zg, bias_col)


def _fox_kernel(qt_ref, k_ref, vt_ref, fz_ref, aqt_ref, o_ref, qaug_ref, kaug_ref, vaug_ref):
    S = k_ref.shape[0]
    d = FOX_HEAD_DIM
    hs = range(FOX_HEADS_PER_STEP)
    lane = lax.broadcasted_iota(jnp.int32, (1, LANES), 1)
    aq = aqt_ref[...]
    half = AUG_LANES // 2
    ak = jnp.transpose(jnp.concatenate([aq[half:AUG_LANES], -aq[:half], aq[AUG_LANES:]], axis=0))
    for j in hs:
        h = pl.program_id(1) * FOX_HEADS_PER_STEP + j
        keep = jnp.where(((lane & (FOX_HEADS - 1)) == h) & (lane < AUG_LANES), 1.0, 0.0).astype(BF16)
        qaug_ref[j, :d, :] = qt_ref[j * d:(j + 1) * d, :]
        qaug_ref[j, d:, :] = aqt_ref[...]
        kaug_ref[j, :, :LANES] = k_ref[:, j * LANES:(j + 1) * LANES]
        kaug_ref[j, :, LANES:] = ak * keep
        vaug_ref[j, :d, :] = vt_ref[j * d:(j + 1) * d, :]
        vaug_ref[j, d:, :] = jnp.ones((BF16_ROWS, S), BF16)
    s_idx = lax.broadcasted_iota(jnp.int32, (TQ, TQ), 0)
    t_idx = lax.broadcasted_iota(jnp.int32, (TQ, TQ), 1)
    causal = s_idx <= t_idx

    ng = S // TQ
    m = [[None] * ng for _ in hs]
    acc = [[None] * ng for _ in hs]
    tiles = [(j, kj, g) for kj in range(ng) for g in range(kj, ng) for j in hs]

    def scores(j, kj, g):
        ka = kaug_ref[j, kj * TQ:(kj + 1) * TQ, :]
        qa = qaug_ref[j, :, g * TQ:(g + 1) * TQ]
        return _dot(ka, qa)

    def finish(j, kj, g, st):
        va = vaug_ref[j, :, kj * TQ:(kj + 1) * TQ]
        if kj == g:
            st = jnp.where(causal, st, -jnp.inf)
        cm = jnp.max(st, axis=0, keepdims=True)
        if kj == 0:
            m[j][g] = cm
            acc[j][g] = _dot(va, jnp.exp2(st - cm).astype(BF16))
        else:
            m_new = jnp.maximum(m[j][g], cm)
            alpha = jnp.exp2(m[j][g] - m_new)
            acc[j][g] = alpha * acc[j][g] + _dot(va, jnp.exp2(st - m_new).astype(BF16))
            m[j][g] = m_new
        if kj == g:
            a = acc[j][g]
            o = jnp.transpose(a[:d] * (1.0 / a[d:d + 1]))
            fz = fz_ref[g * TQ:(g + 1) * TQ, j * LANES:(j + 1) * LANES].astype(F32)
            o_ref[g * TQ:(g + 1) * TQ, j * LANES:(j + 1) * LANES] = (o * _silu(fz)).astype(BF16)

    pending = [scores(*tiles[i]) for i in range(FOX_LOOKAHEAD)]
    for i, tile in enumerate(tiles):
        if i + FOX_LOOKAHEAD < len(tiles):
            pending.append(scores(*tiles[i + FOX_LOOKAHEAD]))
        finish(*tile, pending.pop(0))


def _fox(fm, z, aqt, B, S):
    n = FOX_HEADS_PER_STEP
    dn = n * FOX_HEAD_DIM
    assert FR_FQ % dn == 0 and FR_FV % dn == 0 and ZB_FK % n == 0 and ZB_FZ % n == 0
    return pl.pallas_call(
        _fox_kernel,
        grid=(B, FOX_HEADS // n),
        in_specs=[
            pl.BlockSpec((dn, S), lambda b, h: (FR_FQ // dn + h, b)),
            pl.BlockSpec((S, dn), lambda b, h: (b, ZB_FK // n + h)),
            pl.BlockSpec((dn, S), lambda b, h: (FR_FV // dn + h, b)),
            pl.BlockSpec((S, dn), lambda b, h: (b, ZB_FZ // n + h)),
            pl.BlockSpec((None, LANES, S), lambda b, h: (b, 0, 0)),
        ],
        out_specs=pl.BlockSpec((S, dn), lambda b, h: (b, h)),
        out_shape=jax.ShapeDtypeStruct((B * S, FOX_WIDTH), BF16),
        scratch_shapes=[
            pltpu.VMEM((n, 2 * LANES, S), BF16),
            pltpu.VMEM((n, S, 2 * LANES), BF16),
            pltpu.VMEM((n, FOX_HEAD_DIM + BF16_ROWS, S), BF16),
        ],
        compiler_params=pltpu.CompilerParams(
            dimension_semantics=("arbitrary", "arbitrary"), vmem_limit_bytes=VMEM_LIMIT),
        name="fox",
    )(fm, z, fm, z, aqt)


def _mlstm_kernel(qt_ref, k_ref, vt_ref, mg_ref, col_ref, rows_ref, o_ref, vaug_ref):
    S = k_ref.shape[0]
    L = ML_L
    nc = S // L
    dk, dv = ML_QK_DIM, ML_V_DIM
    hs = range(ML_HEADS_PER_STEP)
    for j in hs:
        vaug_ref[j, :dv, :] = vt_ref[j * dv:(j + 1) * dv, :]
        vaug_ref[j, dv:, :] = jnp.ones((BF16_ROWS, S), BF16)
    s_idx = lax.broadcasted_iota(jnp.int32, (L, L), 0)
    t_idx = lax.broadcasted_iota(jnp.int32, (L, L), 1)
    causal = s_idx <= t_idx
    head = [pl.program_id(1) * ML_HEADS_PER_STEP + j for j in hs]

    head_shift = [(LANES - head[j]) % LANES for j in hs]

    def scores(j, c):
        kc = k_ref[c * L:(c + 1) * L, j * dk:(j + 1) * dk]
        return _dot(kc, qt_ref[j * dk:(j + 1) * dk, c * L:(c + 1) * L])

    def finish(j, c, st, state):
        ct_aug, g_prev = state
        cols = slice(c * L, (c + 1) * L)
        feats = slice(j * dv, (j + 1) * dv)
        b_col = pltpu.roll(col_ref[cols, :], head_shift[j], 1)[:, :1]
        g_row = rows_ref[c, pl.ds(RS_G + head[j], 1), :]
        g_last = g_row[:, L - 1:L]
        va = vaug_ref[j, :, cols]
        kw = k_ref[cols, j * dk:(j + 1) * dk].astype(F32) * jnp.exp2(b_col - g_last)
        upd = _dot(va, kw.astype(BF16))
        sw = st * jnp.where(causal, jnp.exp2(b_col - g_row), 0.0)
        num = _dot(va, sw.astype(BF16))
        if c > 0:
            w_inter = jnp.exp2(g_prev - g_row)
            num = num + w_inter * _dot(ct_aug.astype(BF16), qt_ref[j * dk:(j + 1) * dk, cols])
        den = num[dv:dv + 1, :]
        num = num[:dv, :]
        m_row = rows_ref[c, pl.ds(RS_M + head[j], 1), :]
        inv = 1.0 / jnp.maximum(jnp.abs(den), jnp.exp2(-m_row))
        ssq = jnp.sum(num * num, axis=0, keepdims=True)
        t_scale = inv * lax.rsqrt(inv * inv * ssq * (1.0 / dv) + EPS)
        gate = mg_ref[feats, cols].astype(F32)
        o_ref[cols, feats] = jnp.transpose(num * t_scale * gate).astype(BF16)
        if c == 0:
            return upd, g_last
        return jnp.exp2(g_prev - g_last) * ct_aug + upd, g_last

    work = [(j, c) for c in range(nc) for j in hs]
    pending = [scores(*work[i]) for i in range(ML_LOOKAHEAD)]
    state = [(None, None) for _ in hs]
    for i, (j, c) in enumerate(work):
        if i + ML_LOOKAHEAD < len(work):
            pending.append(scores(*work[i + ML_LOOKAHEAD]))
        state[j] = finish(j, c, pending.pop(0), state[j])


def _mlstm(fm, z, colpack, rowpack, B, S):
    nc = S // ML_L
    n = ML_HEADS_PER_STEP
    dk, dv = n * ML_QK_DIM, n * ML_V_DIM
    assert OR_MQ % dk == 0 and (ZB_MK * LANES) % dk == 0
    assert OR_MV % dv == 0 and OR_MG % dv == 0
    return pl.pallas_call(
        _mlstm_kernel,
        grid=(B, ML_HEADS // n),
        in_specs=[
            pl.BlockSpec((dk, S), lambda b, h: (OR_MQ // dk + h, b)),
            pl.BlockSpec((S, dk), lambda b, h: (b, ZB_MK * LANES // dk + h)),
            pl.BlockSpec((dv, S), lambda b, h: (OR_MV // dv + h, b)),
            pl.BlockSpec((dv, S), lambda b, h: (OR_MG // dv + h, b)),
            pl.BlockSpec((S, LANES), lambda b, h: (b, 0)),
            pl.BlockSpec((None, nc, ROW_SLABS, ML_L), lambda b, h: (b, 0, 0, 0)),
        ],
        out_specs=pl.BlockSpec((S, dv), lambda b, h: (b, h)),
        out_shape=jax.ShapeDtypeStruct((B * S, ML_WIDTH), BF16),
        scratch_shapes=[pltpu.VMEM((n, ML_V_DIM + BF16_ROWS, S), BF16)],
        compiler_params=pltpu.CompilerParams(
            dimension_semantics=("arbitrary", "arbitrary"), vmem_limit_bytes=VMEM_LIMIT),
        name="mlstm",
    )(fm, z, fm, fm, colpack, rowpack)


def _out_kernel(oa_ref, hb_ref, ga_ref, gb_ref, x_ref, wfd_ref, wmd_ref, wo_ref, bga_ref, bgb_ref,
                fg_ref, o_ref):
    def branches(s):
        rows = slice(s * OUT_SUB, (s + 1) * OUT_SUB)
        return _dot(oa_ref[rows, :], wfd_ref[...]), _dot(hb_ref[rows, :], wmd_ref[...])

    def finish(s, ya, yb):
        rows = slice(s * OUT_SUB, (s + 1) * OUT_SUB)
        gate_a = jax.nn.sigmoid(ga_ref[rows, :].astype(F32) + bga_ref[...])
        gate_b = jax.nn.sigmoid(gb_ref[rows, :].astype(F32) + bgb_ref[...])
        y = gate_a * ya + gate_b * yb
        r = x_ref[rows, :] + _dot(y.astype(BF16), wo_ref[...])
        ms = jnp.mean(r * r, axis=-1, keepdims=True)
        o_ref[rows, :] = r * lax.rsqrt(ms + EPS) * fg_ref[...]

    n_sub = TM_OUT // OUT_SUB
    pending = branches(0)
    for s in range(n_sub):
        nxt = branches(s + 1) if s + 1 < n_sub else None
        finish(s, *pending)
        pending = nxt


def _out_stage(oa, hb, z, x2, wfd, wmd, wo, b_gate, final_g):
    T = x2.shape[0]
    row = lambda i: (i, 0)
    const = lambda i: (0, 0)
    return pl.pallas_call(
        _out_kernel,
        grid=(T // TM_OUT,),
        in_specs=[
            pl.BlockSpec((TM_OUT, FOX_WIDTH), row),
            pl.BlockSpec((TM_OUT, ML_WIDTH), row),
            pl.BlockSpec((TM_OUT, D_MODEL), lambda i: (i, ZB_GA // 8)),
            pl.BlockSpec((TM_OUT, D_MODEL), lambda i: (i, ZB_GB // 8)),
            pl.BlockSpec((TM_OUT, D_MODEL), row),
            pl.BlockSpec((FOX_WIDTH, D_MODEL), const, pipeline_mode=pl.Buffered(1)),
            pl.BlockSpec((ML_WIDTH, D_MODEL), const, pipeline_mode=pl.Buffered(1)),
            pl.BlockSpec((D_MODEL, D_MODEL), const, pipeline_mode=pl.Buffered(1)),
            pl.BlockSpec((1, D_MODEL), lambda i: (0, 0)),
            pl.BlockSpec((1, D_MODEL), lambda i: (0, 1)),
            pl.BlockSpec((1, D_MODEL), const),
        ],
        out_specs=pl.BlockSpec((TM_OUT, D_MODEL), row),
        out_shape=jax.ShapeDtypeStruct((T, D_MODEL), F32),
        compiler_params=pltpu.CompilerParams(
            dimension_semantics=("arbitrary",), vmem_limit_bytes=VMEM_LIMIT),
        name="outstage",
    )(oa, hb, z, z, x2, wfd, wmd, wo, b_gate, b_gate, final_g)


def _layer(x2, B, S, norm_g, w_in, b_fox_f, conv_w, conv_b, b_ml_i, b_ml_f, ml_norm_g, b_gate,
           w_fox_down, w_ml_down, w_out, out_g):
    assert w_in.shape == (D_MODEL, IN_WIDTH)
    w_fm, w_tm = _prep_weights(jnp.transpose(w_in))
    zrow = lambda n: jnp.zeros((n,), F32)
    bias_col = jnp.concatenate(
        [b_fox_f, b_ml_i, zrow(8 - ML_HEADS), b_ml_f, zrow(GATE_WIDTH - GL_LF - ML_HEADS)]).reshape(GATE_WIDTH, 1)
    cwq_t = jnp.concatenate(
        [jnp.transpose(conv_w[:, :ML_QK_WIDTH]), conv_b[:ML_QK_WIDTH, None],
         jnp.zeros((ML_QK_WIDTH, LANES - CONV_WIDTH - 1), F32)], axis=1)

    fm, zg, z = _inproj(x2, norm_g.reshape(1, D_MODEL), w_fm, w_tm, cwq_t, conv_w,
                        conv_b.reshape(1, 2 * ML_QK_WIDTH), ml_norm_g.reshape(ML_WIDTH, 1), S)
    rowpack, colpack, aqt = _gates(zg, bias_col, B, S)
    oa = _fox(fm, z, aqt, B, S)
    hb = _mlstm(fm, z, colpack, rowpack, B, S)
    return _out_stage(oa, hb, z, x2, w_fox_down.astype(BF16), w_ml_down.astype(BF16),
                      w_out.astype(BF16), b_gate.reshape(1, 2 * D_MODEL), out_g.reshape(1, D_MODEL))


def kernel(x, norm_g, w_in, b_fox_f, conv_w, conv_b, b_ml_i, b_ml_f, ml_norm_g, b_gate,
           w_fox_down, w_ml_down, w_out, final_g):
    B, S, _ = x.shape
    depth = norm_g.shape[0]
    assert depth == 1, "the final RMSNorm is fused into the (single) layer's output stage"
    out = _layer(x.reshape(B * S, D_MODEL), B, S, norm_g[0], w_in[0], b_fox_f[0], conv_w[0], conv_b[0],
                 b_ml_i[0], b_ml_f[0], ml_norm_g[0], b_gate[0], w_fox_down[0], w_ml_down[0],
                 w_out[0], final_g)
    return out.reshape(B, S, D_MODEL)
```
